```python
import jax, jax.numpy as jnp
from jax import lax
import numpy as np

D_MODEL = 1024
BATCH = 8
SEQ = 4096
DEPTH = 4

GRID_W = 64
CTX_LEN = 256
N_MIXERS = 2
NA_HEADS = 16
NA_HEAD_DIM = D_MODEL // NA_HEADS
NA_KH_MAX = 8
NA_KW = 16
NA_KB = 2 * NA_KW
NA_NCB = GRID_W // NA_KW
SG_CHUNK = 128
SG_HALF = 3 * D_MODEL
SG_GROUP_CH = 128
SG_GROUPS = SG_HALF // SG_GROUP_CH
MLP_HIDDEN = 4 * D_MODEL
N_ATTN_LAYERS = (DEPTH + N_MIXERS - 1) // N_MIXERS
N_SG_LAYERS = DEPTH // N_MIXERS
EPS = 1e-6
NEG_INF = -1e30

kernel_name = "hybrid_natten_gmlp_dit_prefix"


def rms_norm(t, g):
    t32 = t.astype(jnp.float32)
    t32 = t32 * lax.rsqrt(jnp.mean(t32 * t32, axis=-1, keepdims=True) + EPS)
    return (t32 * g.astype(jnp.float32)).astype(t.dtype)


def layer_norm(t, g, b):
    t32 = t.astype(jnp.float32)
    mu = jnp.mean(t32, axis=-1, keepdims=True)
    var = jnp.mean(jnp.square(t32 - mu), axis=-1, keepdims=True)
    out = (t32 - mu) * lax.rsqrt(var + EPS) * g.astype(jnp.float32) + b.astype(jnp.float32)
    return out.astype(t.dtype)


def modulate(h, shift, scale):
    return h * (1.0 + scale) + shift


def squared_relu_mlp(h, w1, w2):
    return jnp.square(jax.nn.relu(h @ w1)) @ w2


def _na_column_tables():
    j = np.arange(NA_NCB)[:, None, None]
    qi = np.arange(NA_KW)[None, :, None]
    m = np.arange(NA_KB)[None, None, :]
    qcol = j * NA_KW + qi
    kstart = np.clip(j * NA_KW - NA_KW // 2, 0, GRID_W - NA_KB)
    kcol = kstart + m
    wstart = np.clip(qcol - NA_KW // 2, 0, GRID_W - NA_KW)
    valid = (kcol >= wstart) & (kcol < wstart + NA_KW)
    col_rel = np.clip(kcol - qcol + NA_KW - 1, 0, 2 * NA_KW - 2)
    key_cols = np.clip(np.arange(NA_NCB) * NA_KW - NA_KW // 2, 0, GRID_W - NA_KB)[:, None] + np.arange(NA_KB)[None, :]
    return key_cols, valid, col_rel


def neighbourhood_attention(h, hc, w_qkv, q_g, k_g, rpb, w_o, ctx_out):
    bsz, n_tok, _ = h.shape
    rows = n_tok // GRID_W
    kh = min(NA_KH_MAX, rows)
    scale = NA_HEAD_DIM ** -0.5

    def heads(t):
        return t.reshape(t.shape[0], t.shape[1], NA_HEADS, NA_HEAD_DIM)

    q, k, v = jnp.split(h @ w_qkv, 3, axis=-1)
    q = rms_norm(heads(q), q_g)
    k = rms_norm(heads(k), k_g)
    v = heads(v)
    if ctx_out:
        qc, kc, vc = jnp.split(hc @ w_qkv, 3, axis=-1)
        qc = rms_norm(heads(qc), q_g)
    else:
        kc, vc = jnp.split(hc @ w_qkv[:, D_MODEL:], 2, axis=-1)
    kc = rms_norm(heads(kc), k_g)
    vc = heads(vc)

    qg = q.reshape(bsz, rows, NA_NCB, NA_KW, NA_HEADS, NA_HEAD_DIM)
    kg = k.reshape(bsz, rows, GRID_W, NA_HEADS, NA_HEAD_DIM)
    vg = v.reshape(bsz, rows, GRID_W, NA_HEADS, NA_HEAD_DIM)
    key_cols, valid, col_rel = _na_column_tables()
    n_lat_keys = kh * NA_KB

    def row_step(r):
        rs = jnp.clip(r - kh // 2, 0, rows - kh)
        q_r = lax.dynamic_index_in_dim(qg, r, axis=1, keepdims=False)
        k_r = lax.dynamic_slice_in_dim(kg, rs, kh, axis=1)[:, :, key_cols]
        v_r = lax.dynamic_slice_in_dim(vg, rs, kh, axis=1)[:, :, key_cols]
        row_rel = rs + jnp.arange(kh) - r + NA_KH_MAX - 1
        bias = rpb[:, row_rel][:, :, col_rel]
        bias = jnp.where(valid[None, None], bias, NEG_INF).transpose(0, 2, 3, 1, 4)
        s_lat = jnp.einsum('bjqhd,bkjmhd->bhjqkm', q_r, k_r,
                           preferred_element_type=jnp.float32) * scale + bias[None].astype(jnp.float32)
        s_ctx = jnp.einsum('bjqhd,blhd->bhjql', q_r, kc,
                           preferred_element_type=jnp.float32) * scale
        s = jnp.concatenate([s_lat.reshape(s_lat.shape[:4] + (n_lat_keys,)), s_ctx], axis=-1)
        p = jax.nn.softmax(s, axis=-1).astype(v.dtype)
        p_lat = p[..., :n_lat_keys].reshape(s_lat.shape)
        p_ctx = p[..., n_lat_keys:]
        o = (jnp.einsum('bhjqkm,bkjmhd->bjqhd', p_lat, v_r)
             + jnp.einsum('bhjql,blhd->bjqhd', p_ctx, vc))
        return o.reshape(bsz, GRID_W, D_MODEL)

    y = lax.map(row_step, jnp.arange(rows))
    y = y.transpose(1, 0, 2, 3).reshape(bsz, n_tok, D_MODEL) @ w_o
    yc = None
    if ctx_out:
        sc = jnp.einsum('blhd,bmhd->bhlm', qc, kc, preferred_element_type=jnp.float32) * scale
        pc = jax.nn.softmax(sc, axis=-1).astype(vc.dtype)
        yc = jnp.einsum('bhlm,bmhd->blhd', pc, vc).reshape(bsz, -1, D_MODEL) @ w_o
    return y, yc


def spatial_gating(h, w_in, b_in, ln_g, ln_b, w_s, b_s, w_o):
    bsz, n, _ = h.shape
    z = jax.nn.gelu(h @ w_in + b_in, approximate=False)
    u, v = jnp.split(z, 2, axis=-1)
    v = layer_norm(v, ln_g, ln_b)
    v = v.reshape(bsz, n // SG_CHUNK, SG_CHUNK, SG_GROUPS, SG_GROUP_CH)
    s = jnp.einsum('gpq,bnqgc->bnpgc', w_s, v) + b_s.T[None, None, :, :, None]
    return (u * s.reshape(bsz, n, SG_HALF)) @ w_o


def setup_inputs(seed: int = 0) -> dict:
    key = jax.random.key(seed)
    ks = jax.random.split(key, 24)
    f32 = jnp.float32
    nrm = lambda k, shape, s: jax.random.normal(k, shape, f32) * s
    D = D_MODEL
    return {
        "x": nrm(ks[0], (BATCH, SEQ, D), 1.0),
        "c": nrm(ks[1], (BATCH, D), 1.0),
        "ctx": nrm(ks[2], (BATCH, CTX_LEN, D), 1.0),
        "c_ctx": nrm(ks[3], (D,), 1.0),
        "ada_w": nrm(ks[4], (DEPTH, D, 6 * D), 0.5 * D ** -0.5),
        "ada_b": nrm(ks[5], (DEPTH, 6 * D), 0.02),
        "norm1_g": 1.0 + nrm(ks[6], (DEPTH, D), 0.02),
        "norm2_g": 1.0 + nrm(ks[7], (DEPTH, D), 0.02),
        "mlp_w1": nrm(ks[8], (DEPTH, D, MLP_HIDDEN), D ** -0.5),
        "mlp_w2": nrm(ks[9], (DEPTH, MLP_HIDDEN, D), MLP_HIDDEN ** -0.5),
        "na_w_qkv": nrm(ks[10], (N_ATTN_LAYERS, D, 3 * D), D ** -0.5),
        "na_q_norm": 1.0 + nrm(ks[11], (N_ATTN_LAYERS, NA_HEAD_DIM), 0.02),
        "na_k_norm": 1.0 + nrm(ks[12], (N_ATTN_LAYERS, NA_HEAD_DIM), 0.02),
        "na_rpb": nrm(ks[13], (N_ATTN_LAYERS, NA_HEADS, 2 * NA_KH_MAX - 1, 2 * NA_KW - 1), 0.02),
        "na_w_o": nrm(ks[14], (N_ATTN_LAYERS, D, D), D ** -0.5),
        "sg_w_in": nrm(ks[15], (N_SG_LAYERS, D, 2 * SG_HALF), D ** -0.5),
        "sg_b_in": nrm(ks[16], (N_SG_LAYERS, 2 * SG_HALF), 0.02),
        "sg_ln_g": 1.0 + nrm(ks[17], (N_SG_LAYERS, SG_HALF), 0.02),
        "sg_ln_b": nrm(ks[18], (N_SG_LAYERS, SG_HALF), 0.02),
        "sg_w_s": nrm(ks[19], (N_SG_LAYERS, SG_GROUPS, SG_CHUNK, SG_CHUNK), SG_CHUNK ** -0.5),
        "sg_b_s": 1.0 + nrm(ks[20], (N_SG_LAYERS, SG_GROUPS, SG_CHUNK), 0.02),
        "sg_w_o": nrm(ks[21], (N_SG_LAYERS, SG_HALF, D), SG_HALF ** -0.5),
    }


def reference(x, c, ctx, c_ctx, ada_w, ada_b, norm1_g, norm2_g, mlp_w1, mlp_w2,
              na_w_qkv, na_q_norm, na_k_norm, na_rpb, na_w_o,
              sg_w_in, sg_b_in, sg_ln_g, sg_ln_b, sg_w_s, sg_b_s, sg_w_o):
    last_ctx_layer = ((DEPTH - 1) // N_MIXERS) * N_MIXERS
    silu_c = jax.nn.silu(c)
    silu_cc = jax.nn.silu(c_ctx)
    for i in range(DEPTH):
        ctx_kv = i <= last_ctx_layer
        ctx_full = i < last_ctx_layer
        sh1, sc1, g1, sh2, sc2, g2 = [t[:, None, :] for t in
                                      jnp.split(silu_c @ ada_w[i] + ada_b[i], 6, axis=-1)]
        h = modulate(rms_norm(x, norm1_g[i]), sh1, sc1)
        hc = None
        if ctx_kv:
            csh1, csc1, cg1, csh2, csc2, cg2 = jnp.split(silu_cc @ ada_w[i] + ada_b[i], 6, axis=-1)
            hc = modulate(rms_norm(ctx, norm1_g[i]), csh1, csc1)
        if i % N_MIXERS == 0:
            a = i // N_MIXERS
            y, yc = neighbourhood_attention(h, hc, na_w_qkv[a], na_q_norm[a], na_k_norm[a],
                                            na_rpb[a], na_w_o[a], ctx_full)
        else:
            s = i // N_MIXERS
            y = spatial_gating(h, sg_w_in[s], sg_b_in[s], sg_ln_g[s], sg_ln_b[s],
                               sg_w_s[s], sg_b_s[s], sg_w_o[s])
            yc = None
            if ctx_full:
                yc = spatial_gating(hc, sg_w_in[s], sg_b_in[s], sg_ln_g[s], sg_ln_b[s],
                                    sg_w_s[s], sg_b_s[s], sg_w_o[s])
        x = x + g1 * y
        x = x + g2 * squared_relu_mlp(modulate(rms_norm(x, norm2_g[i]), sh2, sc2), mlp_w1[i], mlp_w2[i])
        if ctx_full:
            ctx = ctx + cg1 * yc
            ctx = ctx + cg2 * squared_relu_mlp(modulate(rms_norm(ctx, norm2_g[i]), csh2, csc2),
                                               mlp_w1[i], mlp_w2[i])
    return x
```

```python
import functools

import numpy as np
import jax
import jax.numpy as jnp
from jax import lax
from jax.experimental import pallas as pl
from jax.experimental.pallas import tpu as pltpu

D_MODEL = 1024
BATCH = 8
SEQ = 4096
DEPTH = 4
GRID_W = 64
GRID_ROWS = SEQ // GRID_W
CTX_LEN = 256
N_MIXERS = 2
NA_HEADS = 16
NA_HEAD_DIM = D_MODEL // NA_HEADS
NA_KH = 8
NA_KW = 16
NA_KB = 2 * NA_KW
NA_NCB = GRID_W // NA_KW
SG_CHUNK = 128
SG_HALF = 3 * D_MODEL
SG_GROUP_CH = 128
SG_GROUPS = SG_HALF // SG_GROUP_CH
MLP_HIDDEN = 4 * D_MODEL
EPS = 1e-6
NEG_INF = -1e30

F32 = jnp.float32
BF16 = jnp.bfloat16

LANES = 128
HEAD_PAIRS = NA_HEADS // 2
MOD_ROWS = 16
CTX_MOD_ROW = BATCH
Q_ROWS = 8
Q_PATCH = Q_ROWS * NA_KW
KEY_ROW_GROUP = 4
KEY_GROUPS = 4
KEY_ROWS = KEY_ROW_GROUP * KEY_GROUPS
LAT_KEYS = KEY_ROWS * NA_KB
VMEM_LIMIT = 56 * 1024 * 1024


def _dot(a, b):
    return jnp.dot(a, b, preferred_element_type=F32)


def _dot_nt(a, b):
    return lax.dot_general(a, b, (((1,), (1,)), ((), ())), preferred_element_type=F32)


def _rms_mod(x, g, shift, scale):
    ms = jnp.mean(x * x, axis=-1, keepdims=True)
    h = x * lax.rsqrt(ms + EPS) * g
    return h * (1.0 + scale) + shift


def _gelu(z):
    return 0.5 * z * (1.0 + lax.erf(z * np.float32(np.sqrt(0.5))))


def _split_bf16(a):
    hi = a.astype(BF16)
    lo = (a - hi.astype(F32)).astype(BF16)
    return hi, lo


def _params(n_axes):
    return pltpu.CompilerParams(dimension_semantics=("arbitrary",) * n_axes,
                                vmem_limit_bytes=VMEM_LIMIT)


def _resident(shape):
    nd = len(shape)
    return pl.BlockSpec(shape, lambda *_: (0,) * nd, pipeline_mode=pl.Buffered(1))


def _adaln_kernel(c_ref, w_ref, b_ref, o_ref):
    c = c_ref[...]
    s = c * (1.0 / (1.0 + jnp.exp(-c)))
    s_hi, s_lo = _split_bf16(s)
    w_hi, w_lo = _split_bf16(w_ref[...])
    o_ref[...] = _dot(s_hi, w_hi) + (_dot(s_lo, w_hi) + _dot(s_hi, w_lo)) + b_ref[...]


def _adaln(cond, ada_w, ada_b):
    tn = D_MODEL
    return pl.pallas_call(
        _adaln_kernel,
        grid=(DEPTH, 6 * D_MODEL // tn),
        in_specs=[
            pl.BlockSpec((MOD_ROWS, D_MODEL), lambda i, n: (0, 0)),
            pl.BlockSpec((None, D_MODEL, tn), lambda i, n: (i, 0, n)),
            pl.BlockSpec((None, 1, tn), lambda i, n: (i, 0, n)),
        ],
        out_specs=pl.BlockSpec((None, MOD_ROWS, tn), lambda i, n: (i, 0, n)),
        out_shape=jax.ShapeDtypeStruct((DEPTH, MOD_ROWS, 6 * D_MODEL), F32),
        compiler_params=_params(2),
        name="adaln",
    )(cond, ada_w, ada_b.reshape(DEPTH, 1, 6 * D_MODEL))


def _mod_spec(layer, k, tiles_per_batch):
    base = layer * MOD_ROWS * 6 + k
    if tiles_per_batch is None:
        return pl.BlockSpec((None, 1, D_MODEL), lambda t: (base + CTX_MOD_ROW * 6, 0, 0))
    return pl.BlockSpec((None, 1, D_MODEL), lambda t: (base + (t // tiles_per_batch) * 6, 0, 0))


def _vec_spec(n):
    return pl.BlockSpec((1, n), lambda t: (0, 0))


def _key_col_start(j):
    return int(np.clip(j * NA_KW - NA_KW // 2, 0, GRID_W - NA_KB))


def _qkv_kernel(x_ref, g_ref, sh_ref, sc_ref, w_ref, qg_ref, kg_ref, e_ref, et_ref, *out_refs,
                need_q, gather):
    h = _rms_mod(x_ref[...], g_ref[...], sh_ref[...], sc_ref[...]).astype(BF16)

    def head_norm(y, gain):
        ms = _dot((y * y).astype(BF16), e_ref[...])
        r_hi, r_lo = _split_bf16(lax.rsqrt(ms + EPS))
        rb = _dot(r_hi, et_ref[...]) + _dot(r_lo, et_ref[...])
        return y * rb * gain

    refs = list(out_refs)
    if need_q:
        q = head_norm(_dot(h, w_ref[:, 0:D_MODEL]), qg_ref[...])
        refs.pop(0)[...] = q.astype(BF16)
    k_ref, v_ref = refs
    k = head_norm(_dot(h, w_ref[:, D_MODEL:2 * D_MODEL]), kg_ref[...])
    v = _dot(h, w_ref[:, 2 * D_MODEL:3 * D_MODEL])
    if not gather:
        k_ref[...] = k.astype(BF16)
        v_ref[...] = v.astype(BF16)
        return
    rows = k.shape[0] // GRID_W
    k3 = k.reshape(rows, GRID_W, D_MODEL)
    v3 = v.reshape(rows, GRID_W, D_MODEL)
    for j in range(NA_NCB):
        c0 = _key_col_start(j)
        k_ref[:, j, :, :] = k3[:, c0:c0 + NA_KB, :].astype(BF16)
        v_ref[:, j, :, :] = v3[:, c0:c0 + NA_KB, :].astype(BF16)


def _qkv(x, mods, layer, norm_g, w_qkv, q_gain, k_gain, e_mat, et_mat, *, latent, need_q, tm):
    n = x.shape[0]
    tiles_per_batch = (SEQ // tm) if latent else None
    tok = pl.BlockSpec((tm, D_MODEL), lambda t: (t, 0))
    out_specs, out_shapes = [], []
    if need_q:
        out_specs.append(tok)
        out_shapes.append(jax.ShapeDtypeStruct((n, D_MODEL), BF16))
    if latent:
        rows = tm // GRID_W
        kv_spec = pl.BlockSpec((rows, NA_NCB, NA_KB, D_MODEL), lambda t: (t, 0, 0, 0))
        kv_shape = jax.ShapeDtypeStruct((n // GRID_W, NA_NCB, NA_KB, D_MODEL), BF16)
    else:
        kv_spec, kv_shape = tok, jax.ShapeDtypeStruct((n, D_MODEL), BF16)
    out_specs += [kv_spec, kv_spec]
    out_shapes += [kv_shape, kv_shape]
    return pl.pallas_call(
        functools.partial(_qkv_kernel, need_q=need_q, gather=latent),
        grid=(n // tm,),
        in_specs=[
            tok,
            _vec_spec(D_MODEL),
            _mod_spec(layer, 0, tiles_per_batch),
            _mod_spec(layer, 1, tiles_per_batch),
            _resident((D_MODEL, 3 * D_MODEL)),
            _vec_spec(D_MODEL),
            _vec_spec(D_MODEL),
            _resident((D_MODEL, LANES)),
            _resident((LANES, D_MODEL)),
        ],
        out_specs=out_specs,
        out_shape=out_shapes,
        compiler_params=_params(1),
        name="qkv_lat" if latent else "qkv_ctx",
    )(x, norm_g, mods, mods, w_qkv, q_gain, k_gain, e_mat, et_mat)


def _head_pair_attention(q2, keys, values, biases):
    m_rows = q2.shape[0]
    lane = lax.broadcasted_iota(jnp.int32, (1, LANES), 1)
    first = lane < NA_HEAD_DIM
    zero = jnp.zeros_like(q2)
    q4 = jnp.concatenate([jnp.where(first, q2, zero), jnp.where(first, zero, q2)], axis=0)
    scores = []
    for kk, bb in zip(keys, biases):
        s = _dot_nt(q4, kk)
        if bb is not None:
            s = s + jnp.concatenate(bb, axis=0)
        scores.append(s)
    mx = functools.reduce(jnp.maximum, [jnp.max(s, axis=-1, keepdims=True) for s in scores])
    probs = [jnp.exp(s - mx) for s in scores]
    denom = functools.reduce(lambda a, b: a + b, [jnp.sum(p, axis=-1, keepdims=True) for p in probs])
    o = functools.reduce(lambda a, b: a + b, [_dot(p.astype(BF16), vv) for p, vv in zip(probs, values)])
    o = o / denom
    return jnp.where(first, o[:m_rows], o[m_rows:])


def _attn_lat_kernel(q_ref, k0, k1, k2, k3, v0, v1, v2, v3, kc_ref, vc_ref, bias_ref, o_ref):
    k_refs = (k0, k1, k2, k3)
    v_refs = (v0, v1, v2, v3)
    rows_per_group = KEY_ROW_GROUP * NA_KB
    for hp in range(HEAD_PAIRS):
        sl = slice(hp * LANES, (hp + 1) * LANES)
        q2 = q_ref[:, :, sl].reshape(Q_PATCH, LANES)
        kl = jnp.concatenate([r[:, :, sl].reshape(rows_per_group, LANES) for r in k_refs], axis=0)
        vl = jnp.concatenate([r[:, :, sl].reshape(rows_per_group, LANES) for r in v_refs], axis=0)
        o = _head_pair_attention(
            q2, [kl, kc_ref[:, sl]], [vl, vc_ref[:, sl]],
            [(bias_ref[2 * hp], bias_ref[2 * hp + 1]), None])
        o_ref[:, :, sl] = o.reshape(Q_ROWS, NA_KW, LANES).astype(BF16)


def _key_group_start(rb):
    return jnp.clip(2 * rb - 1, 0, GRID_ROWS // KEY_ROW_GROUP - KEY_GROUPS)


def _attn_lat(q, k, v, kc, vc, bias):
    n_rb = GRID_ROWS // Q_ROWS
    groups_per_batch = GRID_ROWS // KEY_ROW_GROUP
    q3 = q.reshape(BATCH * GRID_ROWS, GRID_W, D_MODEL)
    q_spec = pl.BlockSpec((Q_ROWS, NA_KW, D_MODEL), lambda rb, j, b: (b * n_rb + rb, j, 0))

    def kv_spec(i):
        return pl.BlockSpec(
            (KEY_ROW_GROUP, None, NA_KB, D_MODEL),
            lambda rb, j, b: (b * groups_per_batch + _key_group_start(rb) + i, j, 0, 0))

    ctx_spec = pl.BlockSpec((CTX_LEN, D_MODEL), lambda rb, j, b: (b, 0))

    def bias_index(rb, j, b):
        variant = (rb > 0).astype(jnp.int32) + (rb == n_rb - 1).astype(jnp.int32)
        return (variant * NA_NCB + j, 0, 0, 0)

    bias_spec = pl.BlockSpec((None, NA_HEADS, Q_PATCH, LAT_KEYS), bias_index)
    out = pl.pallas_call(
        _attn_lat_kernel,
        grid=(n_rb, NA_NCB, BATCH),
        in_specs=[q_spec] + [kv_spec(i) for i in range(KEY_GROUPS)] * 2 + [ctx_spec, ctx_spec, bias_spec],
        out_specs=q_spec,
        out_shape=jax.ShapeDtypeStruct(q3.shape, BF16),
        compiler_params=_params(3),
        name="attn_lat",
    )(q3, k, k, k, k, v, v, v, v, kc, vc, bias)
    return out.reshape(BATCH * SEQ, D_MODEL)


def _attn_ctx_kernel(q_ref, k_ref, v_ref, o_ref):
    for hp in range(HEAD_PAIRS):
        sl = slice(hp * LANES, (hp + 1) * LANES)
        o = _head_pair_attention(q_ref[:, sl], [k_ref[:, sl]], [v_ref[:, sl]], [None])
        o_ref[:, sl] = o.astype(BF16)


def _attn_ctx(qc, kc, vc):
    spec = pl.BlockSpec((CTX_LEN, D_MODEL), lambda b: (b, 0))
    return pl.pallas_call(
        _attn_ctx_kernel,
        grid=(BATCH,),
        in_specs=[spec, spec, spec],
        out_specs=spec,
        out_shape=jax.ShapeDtypeStruct(qc.shape, BF16),
        compiler_params=_params(1),
        name="attn_ctx",
    )(qc, kc, vc)


def _attention_bias_tables(rpb):
    n_rb = GRID_ROWS // Q_ROWS
    row_idx, col_idx, valid = [], [], []
    for rb in (0, 1, n_rb - 1):
        g0 = int(np.clip(2 * rb - 1, 0, GRID_ROWS // KEY_ROW_GROUP - KEY_GROUPS))
        for j in range(NA_NCB):
            qrow = (rb * Q_ROWS + np.arange(Q_ROWS))[:, None, None, None]
            qcol = (j * NA_KW + np.arange(NA_KW))[None, :, None, None]
            krow = (g0 * KEY_ROW_GROUP + np.arange(KEY_ROWS))[None, None, :, None]
            kcol = (_key_col_start(j) + np.arange(NA_KB))[None, None, None, :]
            rs = np.clip(qrow - NA_KH // 2, 0, GRID_ROWS - NA_KH)
            ws = np.clip(qcol - NA_KW // 2, 0, GRID_W - NA_KW)
            ok = (krow >= rs) & (krow < rs + NA_KH) & (kcol >= ws) & (kcol < ws + NA_KW)
            rr = np.clip(krow - qrow + NA_KH - 1, 0, 2 * NA_KH - 2)
            cr = np.clip(kcol - qcol + NA_KW - 1, 0, 2 * NA_KW - 2)
            shape = (Q_ROWS, NA_KW, KEY_ROWS, NA_KB)
            row_idx.append(np.broadcast_to(rr, shape).reshape(Q_PATCH, LAT_KEYS))
            col_idx.append(np.broadcast_to(cr, shape).reshape(Q_PATCH, LAT_KEYS))
            valid.append(np.broadcast_to(ok, shape).reshape(Q_PATCH, LAT_KEYS))
    row_idx, col_idx, valid = np.stack(row_idx), np.stack(col_idx), np.stack(valid)
    table = rpb[:, row_idx, col_idx]
    table = jnp.where(valid[None], table, NEG_INF)
    return table.transpose(1, 0, 2, 3)


def _mlp_kernel(*refs, with_proj):
    if with_proj:
        x_ref, a_ref, wo_ref, g1_ref, refs = refs[0], refs[1], refs[2], refs[3], refs[4:]
        x = x_ref[...] + g1_ref[...] * _dot(a_ref[...], wo_ref[...])
    else:
        x_ref, refs = refs[0], refs[1:]
        x = x_ref[...]
    n_ref, sh_ref, sc_ref, g2_ref, w1_ref, w2_ref, o_ref = refs
    h = _rms_mod(x, n_ref[...], sh_ref[...], sc_ref[...]).astype(BF16)
    t = jnp.maximum(_dot(h, w1_ref[...]), 0.0)
    t = (t * t).astype(BF16)
    o_ref[...] = x + g2_ref[...] * _dot(t, w2_ref[...])


def _mlp(x, mods, layer, norm_g, w1, w2, *, latent, tm, attn=None, w_o=None):
    n = x.shape[0]
    tiles_per_batch = (SEQ // tm) if latent else None
    tok = pl.BlockSpec((tm, D_MODEL), lambda t: (t, 0))
    with_proj = attn is not None
    in_specs, args = [tok], [x]
    if with_proj:
        in_specs += [tok, _resident((D_MODEL, D_MODEL)), _mod_spec(layer, 2, tiles_per_batch)]
        args += [attn, w_o, mods]
    in_specs += [
        _vec_spec(D_MODEL),
        _mod_spec(layer, 3, tiles_per_batch),
        _mod_spec(layer, 4, tiles_per_batch),
        _mod_spec(layer, 5, tiles_per_batch),
        _resident((D_MODEL, MLP_HIDDEN)),
        _resident((MLP_HIDDEN, D_MODEL)),
    ]
    args += [norm_g, mods, mods, mods, w1, w2]
    return pl.pallas_call(
        functools.partial(_mlp_kernel, with_proj=with_proj),
        grid=(n // tm,),
        in_specs=in_specs,
        out_specs=tok,
        out_shape=jax.ShapeDtypeStruct((n, D_MODEL), F32),
        compiler_params=_params(1),
        name=("proj_mlp" if with_proj else "mlp") + ("_lat" if latent else "_ctx"),
    )(*args)


SG_COLS = 512


def _sg_kernel(x_ref, n_ref, sh_ref, sc_ref, g1_ref, win_ref, bin_ref, lng_ref, lnb_ref,
               ws_ref, bs_ref, wo_ref, o_ref, v_scr, t_scr):
    x = x_ref[...]
    tm = x.shape[0]
    n_chunks = tm // SG_CHUNK
    h = _rms_mod(x, n_ref[...], sh_ref[...], sc_ref[...]).astype(BF16)
    for c in range(SG_HALF // SG_COLS):
        cols = slice(SG_HALF + c * SG_COLS, SG_HALF + (c + 1) * SG_COLS)
        v_scr[:, c * SG_COLS:(c + 1) * SG_COLS] = _gelu(_dot(h, win_ref[:, cols]) + bin_ref[:, cols])
    v = v_scr[...]
    mu = jnp.mean(v, axis=-1, keepdims=True)
    dv = v - mu
    rstd = lax.rsqrt(jnp.mean(dv * dv, axis=-1, keepdims=True) + EPS)
    for c in range(SG_HALF // SG_COLS):
        cols = slice(c * SG_COLS, (c + 1) * SG_COLS)
        u = _gelu(_dot(h, win_ref[:, cols]) + bin_ref[:, cols])
        vn = ((v_scr[:, cols] - mu) * rstd * lng_ref[:, cols] + lnb_ref[:, cols]).astype(BF16)
        for gg in range(SG_COLS // SG_GROUP_CH):
            g = c * (SG_COLS // SG_GROUP_CH) + gg
            gl = slice(gg * SG_GROUP_CH, (gg + 1) * SG_GROUP_CH)
            rhs = jnp.concatenate(
                [vn[i * SG_CHUNK:(i + 1) * SG_CHUNK, gl] for i in range(n_chunks)], axis=1)
            s = _dot(ws_ref[g], rhs)
            for i in range(n_chunks):
                rows = slice(i * SG_CHUNK, (i + 1) * SG_CHUNK)
                s_i = s[:, i * SG_GROUP_CH:(i + 1) * SG_GROUP_CH] + bs_ref[g]
                t_scr[rows, g * SG_GROUP_CH:(g + 1) * SG_GROUP_CH] = (u[rows, gl] * s_i).astype(BF16)
    o_ref[...] = x + g1_ref[...] * _dot(t_scr[...], wo_ref[...])


def _sg(x, mods, layer, norm_g, w_in, b_in, ln_g, ln_b, w_s, b_s, w_o, *, latent, tm):
    n = x.shape[0]
    tiles_per_batch = (SEQ // tm) if latent else None
    tok = pl.BlockSpec((tm, D_MODEL), lambda t: (t, 0))
    return pl.pallas_call(
        _sg_kernel,
        grid=(n // tm,),
        in_specs=[
            tok,
            _vec_spec(D_MODEL),
            _mod_spec(layer, 0, tiles_per_batch),
            _mod_spec(layer, 1, tiles_per_batch),
            _mod_spec(layer, 2, tiles_per_batch),
            _resident((D_MODEL, 2 * SG_HALF)),
            _vec_spec(2 * SG_HALF),
            _vec_spec(SG_HALF),
            _vec_spec(SG_HALF),
            _resident((SG_GROUPS, SG_CHUNK, SG_CHUNK)),
            _resident((SG_GROUPS, SG_CHUNK, SG_GROUP_CH)),
            _resident((SG_HALF, D_MODEL)),
        ],
        out_specs=tok,
        out_shape=jax.ShapeDtypeStruct((n, D_MODEL), F32),
        scratch_shapes=[pltpu.VMEM((tm, SG_HALF), F32), pltpu.VMEM((tm, SG_HALF), BF16)],
        compiler_params=_params(1),
        name="sg_lat" if latent else "sg_ctx",
    )(x, norm_g, mods, mods, mods, w_in, b_in, ln_g, ln_b, w_s, b_s, w_o)


LAT_TM = 512
CTX_TM = 256


def kernel(x, c, ctx, c_ctx, ada_w, ada_b, norm1_g, norm2_g, mlp_w1, mlp_w2,
           na_w_qkv, na_q_norm, na_k_norm, na_rpb, na_w_o,
           sg_w_in, sg_b_in, sg_ln_g, sg_ln_b, sg_w_s, sg_b_s, sg_w_o):
    last_ctx_layer = ((DEPTH - 1) // N_MIXERS) * N_MIXERS
    xl = x.reshape(BATCH * SEQ, D_MODEL)
    xc = ctx.reshape(BATCH * CTX_LEN, D_MODEL)

    cond = jnp.concatenate(
        [c, c_ctx[None, :], jnp.zeros((MOD_ROWS - BATCH - 1, D_MODEL), F32)], axis=0)
    mods = _adaln(cond, ada_w, ada_b).reshape(DEPTH * MOD_ROWS * 6, 1, D_MODEL)

    head_of = np.arange(D_MODEL) // NA_HEAD_DIM
    e_np = (head_of[:, None] == np.arange(LANES)[None, :]).astype(np.float32)
    e_mat = jnp.asarray(e_np / NA_HEAD_DIM, BF16)
    et_mat = jnp.asarray(e_np.T, BF16)

    for i in range(DEPTH):
        ctx_kv = i <= last_ctx_layer
        ctx_full = i < last_ctx_layer
        n1 = norm1_g[i][None, :]
        n2 = norm2_g[i][None, :]
        w1 = mlp_w1[i].astype(BF16)
        w2 = mlp_w2[i].astype(BF16)
        if i % N_MIXERS == 0:
            a = i // N_MIXERS
            w_qkv = na_w_qkv[a].astype(BF16)
            w_o = na_w_o[a].astype(BF16)
            q_gain = jnp.tile(na_q_norm[a] * (NA_HEAD_DIM ** -0.5), NA_HEADS)[None, :]
            k_gain = jnp.tile(na_k_norm[a], NA_HEADS)[None, :]
            bias = _attention_bias_tables(na_rpb[a])
            q, k, v = _qkv(xl, mods, i, n1, w_qkv, q_gain, k_gain, e_mat, et_mat,
                           latent=True, need_q=True, tm=LAT_TM)
            ctx_proj = _qkv(xc, mods, i, n1, w_qkv, q_gain, k_gain, e_mat, et_mat,
                            latent=False, need_q=ctx_full, tm=CTX_TM)
            kc, vc = ctx_proj[-2], ctx_proj[-1]
            att = _attn_lat(q, k, v, kc, vc, bias)
            xl = _mlp(xl, mods, i, n2, w1, w2, latent=True, tm=LAT_TM, attn=att, w_o=w_o)
            if ctx_full:
                att_c = _attn_ctx(ctx_proj[0], kc, vc)
                xc = _mlp(xc, mods, i, n2, w1, w2, latent=False, tm=CTX_TM, attn=att_c, w_o=w_o)
        else:
            s = i // N_MIXERS
            sg_args = (sg_w_in[s].astype(BF16), sg_b_in[s][None, :], sg_ln_g[s][None, :],
                       sg_ln_b[s][None, :], sg_w_s[s].astype(BF16),
                       jnp.broadcast_to(sg_b_s[s][:, :, None], (SG_GROUPS, SG_CHUNK, SG_GROUP_CH)),
                       sg_w_o[s].astype(BF16))
            xl = _sg(xl, mods, i, n1, *sg_args, latent=True, tm=LAT_TM)
            xl = _mlp(xl, mods, i, n2, w1, w2, latent=True, tm=LAT_TM)
            if ctx_full:
                xc = _sg(xc, mods, i, n1, *sg_args, latent=False, tm=CTX_TM)
                xc = _mlp(xc, mods, i, n2, w1, w2, latent=False, tm=CTX_TM)
    return xl.reshape(BATCH, SEQ, D_MODEL)
```

```python
import functools

import numpy as np
import jax
import jax.numpy as jnp
from jax import lax
from jax.experimental import pallas as pl
from jax.experimental.pallas import tpu as pltpu

D_MODEL = 1024
BATCH = 8
SEQ = 4096
DEPTH = 4
GRID_W = 64
GRID_ROWS = SEQ // GRID_W
CTX_LEN = 256
N_MIXERS = 2
NA_HEADS = 16
NA_HEAD_DIM = D_MODEL // NA_HEADS
NA_KH = 8
NA_KW = 16
NA_KB = 2 * NA_KW
NA_NCB = GRID_W // NA_KW
SG_CHUNK = 128
SG_HALF = 3 * D_MODEL
SG_GROUP_CH = 128
SG_GROUPS = SG_HALF // SG_GROUP_CH
MLP_HIDDEN = 4 * D_MODEL
EPS = 1e-6
NEG_INF = -1e30
LOG2_E = float(np.log2(np.e))

F32 = jnp.float32
BF16 = jnp.bfloat16

LANES = 128
HEAD_PAIRS = NA_HEADS // 2
MOD_ROWS = 16
CTX_MOD_ROW = BATCH
Q_ROWS = 8
Q_PATCH = Q_ROWS * NA_KW
KEY_ROW_GROUP = 4
KEY_GROUPS = 4
KEY_ROWS = KEY_ROW_GROUP * KEY_GROUPS
LAT_KEYS = KEY_ROWS * NA_KB
VMEM_LIMIT = 56 * 1024 * 1024


def _dot(a, b):
    return jnp.dot(a, b, preferred_element_type=F32)


def _dot_nt(a, b):
    return lax.dot_general(a, b, (((1,), (1,)), ((), ())), preferred_element_type=F32)


def _rms_mod(x, g, shift, scale):
    ms = jnp.mean(x * x, axis=-1, keepdims=True)
    h = x * lax.rsqrt(ms + EPS) * g
    return h * (1.0 + scale) + shift


def _gelu(z):
    return 0.5 * z * (1.0 + lax.erf(z * np.float32(np.sqrt(0.5))))


def _split_bf16(a):
    hi = a.astype(BF16)
    lo = (a - hi.astype(F32)).astype(BF16)
    return hi, lo


def _params(n_axes):
    return pltpu.CompilerParams(dimension_semantics=("arbitrary",) * n_axes,
                                vmem_limit_bytes=VMEM_LIMIT)


def _resident(shape):
    nd = len(shape)
    return pl.BlockSpec(shape, lambda *_: (0,) * nd, pipeline_mode=pl.Buffered(1))


def _adaln_kernel(c_ref, w_ref, b_ref, o_ref):
    c = c_ref[...]
    s = c * (1.0 / (1.0 + jnp.exp(-c)))
    s_hi, s_lo = _split_bf16(s)
    w_hi, w_lo = _split_bf16(w_ref[...])
    o_ref[...] = _dot(s_hi, w_hi) + (_dot(s_lo, w_hi) + _dot(s_hi, w_lo)) + b_ref[...]


def _adaln(cond, ada_w, ada_b):
    tn = D_MODEL
    return pl.pallas_call(
        _adaln_kernel,
        grid=(DEPTH, 6 * D_MODEL // tn),
        in_specs=[
            pl.BlockSpec((MOD_ROWS, D_MODEL), lambda i, n: (0, 0)),
            pl.BlockSpec((None, D_MODEL, tn), lambda i, n: (i, 0, n)),
            pl.BlockSpec((None, 1, tn), lambda i, n: (i, 0, n)),
        ],
        out_specs=pl.BlockSpec((None, MOD_ROWS, tn), lambda i, n: (i, 0, n)),
        out_shape=jax.ShapeDtypeStruct((DEPTH, MOD_ROWS, 6 * D_MODEL), F32),
        compiler_params=_params(2),
        name="adaln",
    )(cond, ada_w, ada_b.reshape(DEPTH, 1, 6 * D_MODEL))


def _mod_spec(layer, k, tiles_per_batch):
    base = layer * MOD_ROWS * 6 + k
    if tiles_per_batch is None:
        return pl.BlockSpec((None, 1, D_MODEL), lambda t: (base + CTX_MOD_ROW * 6, 0, 0))
    return pl.BlockSpec((None, 1, D_MODEL), lambda t: (base + (t // tiles_per_batch) * 6, 0, 0))


def _vec_spec(n):
    return pl.BlockSpec((1, n), lambda t: (0, 0))


def _key_col_start(j):
    return int(np.clip(j * NA_KW - NA_KW // 2, 0, GRID_W - NA_KB))


def _qkv_kernel(x_ref, g_ref, sh_ref, sc_ref, w_ref, qg_ref, kg_ref, e_ref, et_ref, *out_refs,
                need_q, gather):
    h = _rms_mod(x_ref[...], g_ref[...], sh_ref[...], sc_ref[...]).astype(BF16)

    def head_norm(y, gain):
        ms = _dot((y * y).astype(BF16), e_ref[...])
        r_hi, r_lo = _split_bf16(lax.rsqrt(ms + EPS))
        rb = _dot(r_hi, et_ref[...]) + _dot(r_lo, et_ref[...])
        return y * rb * gain

    refs = list(out_refs)
    if need_q:
        q = head_norm(_dot(h, w_ref[:, 0:D_MODEL]), qg_ref[...])
        refs.pop(0)[...] = q.astype(BF16)
    k_ref, v_ref = refs
    k = head_norm(_dot(h, w_ref[:, D_MODEL:2 * D_MODEL]), kg_ref[...])
    v = _dot(h, w_ref[:, 2 * D_MODEL:3 * D_MODEL])
    if not gather:
        k_ref[...] = k.astype(BF16)
        v_ref[...] = v.astype(BF16)
        return
    rows = k.shape[0] // GRID_W
    k3 = k.reshape(rows, GRID_W, D_MODEL)
    v3 = v.reshape(rows, GRID_W, D_MODEL)
    for j in range(NA_NCB):
        c0 = _key_col_start(j)
        k_ref[:, j, :, :] = k3[:, c0:c0 + NA_KB, :].astype(BF16)
        v_ref[:, j, :, :] = v3[:, c0:c0 + NA_KB, :].astype(BF16)


def _qkv(x, mods, layer, norm_g, w_qkv, q_gain, k_gain, e_mat, et_mat, *, latent, need_q, tm):
    n = x.shape[0]
    tiles_per_batch = (SEQ // tm) if latent else None
    tok = pl.BlockSpec((tm, D_MODEL), lambda t: (t, 0))
    out_specs, out_shapes = [], []
    if need_q:
        out_specs.append(tok)
        out_shapes.append(jax.ShapeDtypeStruct((n, D_MODEL), BF16))
    if latent:
        rows = tm // GRID_W
        kv_spec = pl.BlockSpec((rows, NA_NCB, NA_KB, D_MODEL), lambda t: (t, 0, 0, 0))
        kv_shape = jax.ShapeDtypeStruct((n // GRID_W, NA_NCB, NA_KB, D_MODEL), BF16)
    else:
        kv_spec, kv_shape = tok, jax.ShapeDtypeStruct((n, D_MODEL), BF16)
    out_specs += [kv_spec, kv_spec]
    out_shapes += [kv_shape, kv_shape]
    return pl.pallas_call(
        functools.partial(_qkv_kernel, need_q=need_q, gather=latent),
        grid=(n // tm,),
        in_specs=[
            tok,
            _vec_spec(D_MODEL),
            _mod_spec(layer, 0, tiles_per_batch),
            _mod_spec(layer, 1, tiles_per_batch),
            _resident((D_MODEL, 3 * D_MODEL)),
            _vec_spec(D_MODEL),
            _vec_spec(D_MODEL),
            _resident((D_MODEL, LANES)),
            _resident((LANES, D_MODEL)),
        ],
        out_specs=out_specs,
        out_shape=out_shapes,
        compiler_params=_params(1),
        name="qkv_lat" if latent else "qkv_ctx",
    )(x, norm_g, mods, mods, w_qkv, q_gain, k_gain, e_mat, et_mat)


def _pair_scores(q2, keys, bias_pair):
    lane = lax.broadcasted_iota(jnp.int32, (1, LANES), 1)
    first = lane < NA_HEAD_DIM
    zero = jnp.zeros_like(q2)
    q4 = jnp.concatenate([jnp.where(first, q2, zero), jnp.where(first, zero, q2)], axis=0)
    s = _dot_nt(q4, keys)
    if bias_pair is not None:
        nb = bias_pair[0].shape[1]
        s = jnp.concatenate([s[:, :nb] + jnp.concatenate(bias_pair, axis=0), s[:, nb:]], axis=1)
    return s


def _pair_softmax_pv(s, values):
    m_rows = s.shape[0] // 2
    tiles = [s[:, i:i + LANES] for i in range(0, s.shape[1], LANES)]
    mx = jnp.max(functools.reduce(jnp.maximum, tiles), axis=-1, keepdims=True)
    probs = [jnp.exp2(t - mx) for t in tiles]
    denom = jnp.sum(functools.reduce(lambda a, b: a + b, probs), axis=-1, keepdims=True)
    o = _dot(jnp.concatenate([p.astype(BF16) for p in probs], axis=1), values) / denom
    lane = lax.broadcasted_iota(jnp.int32, (1, LANES), 1)
    return jnp.where(lane < NA_HEAD_DIM, o[:m_rows], o[m_rows:])


def _attn_lat_kernel(q_ref, k0, k1, k2, k3, v0, v1, v2, v3, kc_ref, vc_ref, bias_ref, o_ref):
    kv_rows = KEY_ROW_GROUP * NA_KB

    def lanes(hp):
        return slice(hp * LANES, (hp + 1) * LANES)

    def gather(refs, ctx_ref, hp):
        return jnp.concatenate(
            [r[:, :, lanes(hp)].reshape(kv_rows, LANES) for r in refs] + [ctx_ref[:, lanes(hp)]], axis=0)

    def scores(hp):
        q2 = q_ref[:, :, lanes(hp)].reshape(Q_PATCH, LANES)
        return _pair_scores(q2, gather((k0, k1, k2, k3), kc_ref, hp),
                            (bias_ref[2 * hp], bias_ref[2 * hp + 1]))

    s_next = scores(0)
    for hp in range(HEAD_PAIRS):
        s = s_next
        if hp + 1 < HEAD_PAIRS:
            s_next = scores(hp + 1)
        o = _pair_softmax_pv(s, gather((v0, v1, v2, v3), vc_ref, hp))
        o_ref[:, :, lanes(hp)] = o.reshape(Q_ROWS, NA_KW, LANES).astype(BF16)


def _key_group_start(rb):
    return jnp.clip(2 * rb - 1, 0, GRID_ROWS // KEY_ROW_GROUP - KEY_GROUPS)


def _attn_lat(q, k, v, kc, vc, bias):
    n_rb = GRID_ROWS // Q_ROWS
    groups_per_batch = GRID_ROWS // KEY_ROW_GROUP
    q3 = q.reshape(BATCH * GRID_ROWS, GRID_W, D_MODEL)
    q_spec = pl.BlockSpec((Q_ROWS, NA_KW, D_MODEL), lambda rb, j, b: (b * n_rb + rb, j, 0))

    def kv_spec(i):
        return pl.BlockSpec(
            (KEY_ROW_GROUP, None, NA_KB, D_MODEL),
            lambda rb, j, b: (b * groups_per_batch + _key_group_start(rb) + i, j, 0, 0))

    ctx_spec = pl.BlockSpec((CTX_LEN, D_MODEL), lambda rb, j, b: (b, 0))

    def bias_index(rb, j, b):
        variant = (rb > 0).astype(jnp.int32) + (rb == n_rb - 1).astype(jnp.int32)
        return (variant * NA_NCB + j, 0, 0, 0)

    bias_spec = pl.BlockSpec((None, NA_HEADS, Q_PATCH, LAT_KEYS), bias_index)
    out = pl.pallas_call(
        _attn_lat_kernel,
        grid=(n_rb, NA_NCB, BATCH),
        in_specs=[q_spec] + [kv_spec(i) for i in range(KEY_GROUPS)] * 2 + [ctx_spec, ctx_spec, bias_spec],
        out_specs=q_spec,
        out_shape=jax.ShapeDtypeStruct(q3.shape, BF16),
        compiler_params=_params(3),
        name="attn_lat",
    )(q3, k, k, k, k, v, v, v, v, kc, vc, bias)
    return out.reshape(BATCH * SEQ, D_MODEL)


def _attn_ctx_kernel(q_ref, k_ref, v_ref, o_ref):
    for hp in range(HEAD_PAIRS):
        sl = slice(hp * LANES, (hp + 1) * LANES)
        s = _pair_scores(q_ref[:, sl], k_ref[:, sl], None)
        o_ref[:, sl] = _pair_softmax_pv(s, v_ref[:, sl]).astype(BF16)


def _attn_ctx(qc, kc, vc):
    spec = pl.BlockSpec((CTX_LEN, D_MODEL), lambda b: (b, 0))
    return pl.pallas_call(
        _attn_ctx_kernel,
        grid=(BATCH,),
        in_specs=[spec, spec, spec],
        out_specs=spec,
        out_shape=jax.ShapeDtypeStruct(qc.shape, BF16),
        compiler_params=_params(1),
        name="attn_ctx",
    )(qc, kc, vc)


def _attention_bias_tables(rpb):
    n_rb = GRID_ROWS // Q_ROWS
    col_pad, row_pad = NA_KW, Q_ROWS
    rpb_p = jnp.pad(rpb, ((0, 0), (row_pad, row_pad), (col_pad, col_pad)))
    tables = []
    for rb in (0, 1, n_rb - 1):
        g0 = int(np.clip(2 * rb - 1, 0, GRID_ROWS // KEY_ROW_GROUP - KEY_GROUPS))
        row_off = g0 * KEY_ROW_GROUP - rb * Q_ROWS + NA_KH - 1 + row_pad
        for j in range(NA_NCB):
            col_off = _key_col_start(j) - j * NA_KW + NA_KW - 1 + col_pad
            by_col = jnp.stack([rpb_p[:, :, col_off - qi:col_off - qi + NA_KB]
                                for qi in range(NA_KW)], axis=1)
            by_row = jnp.stack([by_col[:, :, row_off - ql:row_off - ql + KEY_ROWS, :]
                                for ql in range(Q_ROWS)], axis=1)
            qrow = (rb * Q_ROWS + np.arange(Q_ROWS))[:, None, None, None]
            qcol = (j * NA_KW + np.arange(NA_KW))[None, :, None, None]
            krow = (g0 * KEY_ROW_GROUP + np.arange(KEY_ROWS))[None, None, :, None]
            kcol = (_key_col_start(j) + np.arange(NA_KB))[None, None, None, :]
            rs = np.clip(qrow - NA_KH // 2, 0, GRID_ROWS - NA_KH)
            ws = np.clip(qcol - NA_KW // 2, 0, GRID_W - NA_KW)
            ok = (krow >= rs) & (krow < rs + NA_KH) & (kcol >= ws) & (kcol < ws + NA_KW)
            tables.append(jnp.where(ok[None], by_row, NEG_INF).reshape(NA_HEADS, Q_PATCH, LAT_KEYS))
    return jnp.stack(tables, axis=0)


def _mlp_kernel(*refs, with_proj):
    if with_proj:
        x_ref, a_ref, wo_ref, g1_ref, refs = refs[0], refs[1], refs[2], refs[3], refs[4:]
        x = x_ref[...] + g1_ref[...] * _dot(a_ref[...], wo_ref[...])
    else:
        x_ref, refs = refs[0], refs[1:]
        x = x_ref[...]
    n_ref, sh_ref, sc_ref, g2_ref, w1_ref, w2_ref, o_ref = refs
    h = _rms_mod(x, n_ref[...], sh_ref[...], sc_ref[...]).astype(BF16)
    t = jnp.maximum(_dot(h, w1_ref[...]), 0.0)
    t = (t * t).astype(BF16)
    o_ref[...] = x + g2_ref[...] * _dot(t, w2_ref[...])


def _mlp(x, mods, layer, norm_g, w1, w2, *, latent, tm, attn=None, w_o=None):
    n = x.shape[0]
    tiles_per_batch = (SEQ // tm) if latent else None
    tok = pl.BlockSpec((tm, D_MODEL), lambda t: (t, 0))
    with_proj = attn is not None
    in_specs, args = [tok], [x]
    if with_proj:
        in_specs += [tok, _resident((D_MODEL, D_MODEL)), _mod_spec(layer, 2, tiles_per_batch)]
        args += [attn, w_o, mods]
    in_specs += [
        _vec_spec(D_MODEL),
        _mod_spec(layer, 3, tiles_per_batch),
        _mod_spec(layer, 4, tiles_per_batch),
        _mod_spec(layer, 5, tiles_per_batch),
        _resident((D_MODEL, MLP_HIDDEN)),
        _resident((MLP_HIDDEN, D_MODEL)),
    ]
    args += [norm_g, mods, mods, mods, w1, w2]
    return pl.pallas_call(
        functools.partial(_mlp_kernel, with_proj=with_proj),
        grid=(n // tm,),
        in_specs=in_specs,
        out_specs=tok,
        out_shape=jax.ShapeDtypeStruct((n, D_MODEL), F32),
        compiler_params=_params(1),
        name=("proj_mlp" if with_proj else "mlp") + ("_lat" if latent else "_ctx"),
    )(*args)


SG_COLS = 512


def _sg_kernel(x_ref, n_ref, sh_ref, sc_ref, g1_ref, win_ref, bin_ref, lng_ref, lnb_ref,
               ws_ref, bs_ref, wo_ref, o_ref, v_scr, t_scr):
    x = x_ref[...]
    tm = x.shape[0]
    n_chunks = tm // SG_CHUNK
    h = _rms_mod(x, n_ref[...], sh_ref[...], sc_ref[...]).astype(BF16)
    for c in range(SG_HALF // SG_COLS):
        cols = slice(SG_HALF + c * SG_COLS, SG_HALF + (c + 1) * SG_COLS)
        v_scr[:, c * SG_COLS:(c + 1) * SG_COLS] = _gelu(_dot(h, win_ref[:, cols]) + bin_ref[:, cols])
    v = v_scr[...]
    mu = jnp.mean(v, axis=-1, keepdims=True)
    dv = v - mu
    rstd = lax.rsqrt(jnp.mean(dv * dv, axis=-1, keepdims=True) + EPS)
    for c in range(SG_HALF // SG_COLS):
        cols = slice(c * SG_COLS, (c + 1) * SG_COLS)
        u = _gelu(_dot(h, win_ref[:, cols]) + bin_ref[:, cols])
        vn = ((v_scr[:, cols] - mu) * rstd * lng_ref[:, cols] + lnb_ref[:, cols]).astype(BF16)
        for gg in range(SG_COLS // SG_GROUP_CH):
            g = c * (SG_COLS // SG_GROUP_CH) + gg
            gl = slice(gg * SG_GROUP_CH, (gg + 1) * SG_GROUP_CH)
            rhs = jnp.concatenate(
                [vn[i * SG_CHUNK:(i + 1) * SG_CHUNK, gl] for i in range(n_chunks)], axis=1)
            s = _dot(ws_ref[g], rhs)
            for i in range(n_chunks):
                rows = slice(i * SG_CHUNK, (i + 1) * SG_CHUNK)
                s_i = s[:, i * SG_GROUP_CH:(i + 1) * SG_GROUP_CH] + bs_ref[g]
                t_scr[rows, g * SG_GROUP_CH:(g + 1) * SG_GROUP_CH] = (u[rows, gl] * s_i).astype(BF16)
    o_ref[...] = x + g1_ref[...] * _dot(t_scr[...], wo_ref[...])


def _sg(x, mods, layer, norm_g, w_in, b_in, ln_g, ln_b, w_s, b_s, w_o, *, latent, tm):
    n = x.shape[0]
    tiles_per_batch = (SEQ // tm) if latent else None
    tok = pl.BlockSpec((tm, D_MODEL), lambda t: (t, 0))
    return pl.pallas_call(
        _sg_kernel,
        grid=(n // tm,),
        in_specs=[
            tok,
            _vec_spec(D_MODEL),
            _mod_spec(layer, 0, tiles_per_batch),
            _mod_spec(layer, 1, tiles_per_batch),
            _mod_spec(layer, 2, tiles_per_batch),
            _resident((D_MODEL, 2 * SG_HALF)),
            _vec_spec(2 * SG_HALF),
            _vec_spec(SG_HALF),
            _vec_spec(SG_HALF),
            _resident((SG_GROUPS, SG_CHUNK, SG_CHUNK)),
            _resident((SG_GROUPS, SG_CHUNK, SG_GROUP_CH)),
            _resident((SG_HALF, D_MODEL)),
        ],
        out_specs=tok,
        out_shape=jax.ShapeDtypeStruct((n, D_MODEL), F32),
        scratch_shapes=[pltpu.VMEM((tm, SG_HALF), F32), pltpu.VMEM((tm, SG_HALF), BF16)],
        compiler_params=_params(1),
        name="sg_lat" if latent else "sg_ctx",
    )(x, norm_g, mods, mods, mods, w_in, b_in, ln_g, ln_b, w_s, b_s, w_o)


LAT_TM = 512
CTX_TM = 256


def kernel(x, c, ctx, c_ctx, ada_w, ada_b, norm1_g, norm2_g, mlp_w1, mlp_w2,
           na_w_qkv, na_q_norm, na_k_norm, na_rpb, na_w_o,
           sg_w_in, sg_b_in, sg_ln_g, sg_ln_b, sg_w_s, sg_b_s, sg_w_o):
    last_ctx_layer = ((DEPTH - 1) // N_MIXERS) * N_MIXERS
    xl = x.reshape(BATCH * SEQ, D_MODEL)
    xc = ctx.reshape(BATCH * CTX_LEN, D_MODEL)

    cond = jnp.concatenate(
        [c, c_ctx[None, :], jnp.zeros((MOD_ROWS - BATCH - 1, D_MODEL), F32)], axis=0)
    mods = _adaln(cond, ada_w, ada_b).reshape(DEPTH * MOD_ROWS * 6, 1, D_MODEL)

    head_of = np.arange(D_MODEL) // NA_HEAD_DIM
    e_np = (head_of[:, None] == np.arange(LANES)[None, :]).astype(np.float32)
    e_mat = jnp.asarray(e_np / NA_HEAD_DIM, BF16)
    et_mat = jnp.asarray(e_np.T, BF16)

    for i in range(DEPTH):
        ctx_kv = i <= last_ctx_layer
        ctx_full = i < last_ctx_layer
        n1 = norm1_g[i][None, :]
        n2 = norm2_g[i][None, :]
        w1 = mlp_w1[i].astype(BF16)
        w2 = mlp_w2[i].astype(BF16)
        if i % N_MIXERS == 0:
            a = i // N_MIXERS
            w_qkv = na_w_qkv[a].astype(BF16)
            w_o = na_w_o[a].astype(BF16)
            q_gain = jnp.tile(na_q_norm[a] * (NA_HEAD_DIM ** -0.5 * LOG2_E), NA_HEADS)[None, :]
            k_gain = jnp.tile(na_k_norm[a], NA_HEADS)[None, :]
            bias = _attention_bias_tables(na_rpb[a] * LOG2_E)
            q, k, v = _qkv(xl, mods, i, n1, w_qkv, q_gain, k_gain, e_mat, et_mat,
                           latent=True, need_q=True, tm=LAT_TM)
            ctx_proj = _qkv(xc, mods, i, n1, w_qkv, q_gain, k_gain, e_mat, et_mat,
                            latent=False, need_q=ctx_full, tm=CTX_TM)
            kc, vc = ctx_proj[-2], ctx_proj[-1]
            att = _attn_lat(q, k, v, kc, vc, bias)
            xl = _mlp(xl, mods, i, n2, w1, w2, latent=True, tm=LAT_TM, attn=att, w_o=w_o)
            if ctx_full:
                att_c = _attn_ctx(ctx_proj[0], kc, vc)
                xc = _mlp(xc, mods, i, n2, w1, w2, latent=False, tm=CTX_TM, attn=att_c, w_o=w_o)
        else:
            s = i // N_MIXERS
            sg_args = (sg_w_in[s].astype(BF16), sg_b_in[s][None, :], sg_ln_g[s][None, :],
                       sg_ln_b[s][None, :], sg_w_s[s].astype(BF16),
                       jnp.broadcast_to(sg_b_s[s][:, :, None], (SG_GROUPS, SG_CHUNK, SG_GROUP_CH)),
                       sg_w_o[s].astype(BF16))
            xl = _sg(xl, mods, i, n1, *sg_args, latent=True, tm=LAT_TM)
            xl = _mlp(xl, mods, i, n2, w1, w2, latent=True, tm=LAT_TM)
            if ctx_full:
                xc = _sg(xc, mods, i, n1, *sg_args, latent=False, tm=CTX_TM)
                xc = _mlp(xc, mods, i, n2, w1, w2, latent=False, tm=CTX_TM)
    return xl.reshape(BATCH, SEQ, D_MODEL)
```

```python
import functools

import numpy as np
import jax
import jax.numpy as jnp
from jax import lax
from jax.experimental import pallas as pl
from jax.experimental.pallas import tpu as pltpu

D_MODEL = 1024
BATCH = 8
SEQ = 4096
DEPTH = 4
GRID_W = 64
GRID_ROWS = SEQ // GRID_W
CTX_LEN = 256
N_MIXERS = 2
NA_HEADS = 16
NA_HEAD_DIM = D_MODEL // NA_HEADS
NA_KH = 8
NA_KW = 16
NA_KB = 2 * NA_KW
NA_NCB = GRID_W // NA_KW
SG_CHUNK = 128
SG_HALF = 3 * D_MODEL
SG_GROUP_CH = 128
SG_GROUPS = SG_HALF // SG_GROUP_CH
MLP_HIDDEN = 4 * D_MODEL
EPS = 1e-6
NEG_INF = -1e30
LOG2_E = float(np.log2(np.e))

F32 = jnp.float32
BF16 = jnp.bfloat16

LANES = 128
HEAD_PAIRS = NA_HEADS // 2
MOD_ROWS = 16
CTX_MOD_ROW = BATCH
Q_ROWS = 8
Q_PATCH = Q_ROWS * NA_KW
KEY_ROW_GROUP = 4
KEY_GROUPS = 4
KEY_ROWS = KEY_ROW_GROUP * KEY_GROUPS
LAT_KEYS = KEY_ROWS * NA_KB
VMEM_LIMIT = 56 * 1024 * 1024


def _dot(a, b):
    return jnp.dot(a, b, preferred_element_type=F32)


def _dot_nt(a, b):
    return lax.dot_general(a, b, (((1,), (1,)), ((), ())), preferred_element_type=F32)


def _rms_mod(x, g, shift, scale):
    ms = jnp.mean(x * x, axis=-1, keepdims=True)
    h = x * lax.rsqrt(ms + EPS) * g
    return h * (1.0 + scale) + shift


def _gelu(z):
    return 0.5 * z * (1.0 + lax.erf(z * np.float32(np.sqrt(0.5))))


def _split_bf16(a):
    hi = a.astype(BF16)
    lo = (a - hi.astype(F32)).astype(BF16)
    return hi, lo


def _params(n_axes):
    return pltpu.CompilerParams(dimension_semantics=("arbitrary",) * n_axes,
                                vmem_limit_bytes=VMEM_LIMIT)


def _resident(shape):
    nd = len(shape)
    return pl.BlockSpec(shape, lambda *_: (0,) * nd, pipeline_mode=pl.Buffered(1))


def _adaln_kernel(c_ref, w_ref, b_ref, o_ref):
    c = c_ref[...]
    s = c * (1.0 / (1.0 + jnp.exp(-c)))
    s_hi, s_lo = _split_bf16(s)
    w_hi, w_lo = _split_bf16(w_ref[...])
    o_ref[...] = _dot(s_hi, w_hi) + (_dot(s_lo, w_hi) + _dot(s_hi, w_lo)) + b_ref[...]


def _adaln(cond, ada_w, ada_b):
    tn = D_MODEL
    return pl.pallas_call(
        _adaln_kernel,
        grid=(DEPTH, 6 * D_MODEL // tn),
        in_specs=[
            pl.BlockSpec((MOD_ROWS, D_MODEL), lambda i, n: (0, 0)),
            pl.BlockSpec((None, D_MODEL, tn), lambda i, n: (i, 0, n)),
            pl.BlockSpec((None, 1, tn), lambda i, n: (i, 0, n)),
        ],
        out_specs=pl.BlockSpec((None, MOD_ROWS, tn), lambda i, n: (i, 0, n)),
        out_shape=jax.ShapeDtypeStruct((DEPTH, MOD_ROWS, 6 * D_MODEL), F32),
        compiler_params=_params(2),
        name="adaln",
    )(cond, ada_w, ada_b.reshape(DEPTH, 1, 6 * D_MODEL))


def _mod_spec(layer, k, tiles_per_batch):
    base = layer * MOD_ROWS * 6 + k
    if tiles_per_batch is None:
        return pl.BlockSpec((None, 1, D_MODEL), lambda t: (base + CTX_MOD_ROW * 6, 0, 0))
    return pl.BlockSpec((None, 1, D_MODEL), lambda t: (base + (t // tiles_per_batch) * 6, 0, 0))


def _vec_spec(n):
    return pl.BlockSpec((1, n), lambda t: (0, 0))


def _key_col_start(j):
    return int(np.clip(j * NA_KW - NA_KW // 2, 0, GRID_W - NA_KB))


def _qkv_kernel(x_ref, g_ref, sh_ref, sc_ref, w_ref, qg_ref, kg_ref, e_ref, et_ref, *out_refs,
                need_q, gather):
    h = _rms_mod(x_ref[...], g_ref[...], sh_ref[...], sc_ref[...]).astype(BF16)

    def head_norm(y, gain):
        ms = _dot((y * y).astype(BF16), e_ref[...])
        r_hi, r_lo = _split_bf16(lax.rsqrt(ms + EPS))
        rb = _dot(r_hi, et_ref[...]) + _dot(r_lo, et_ref[...])
        return y * rb * gain

    refs = list(out_refs)
    if need_q:
        q = head_norm(_dot(h, w_ref[:, 0:D_MODEL]), qg_ref[...])
        refs.pop(0)[...] = q.astype(BF16)
    k_ref, v_ref = refs
    k = head_norm(_dot(h, w_ref[:, D_MODEL:2 * D_MODEL]), kg_ref[...])
    v = _dot(h, w_ref[:, 2 * D_MODEL:3 * D_MODEL])
    if not gather:
        k_ref[...] = k.astype(BF16)
        v_ref[...] = v.astype(BF16)
        return
    rows = k.shape[0] // GRID_W
    k3 = k.reshape(rows, GRID_W, D_MODEL)
    v3 = v.reshape(rows, GRID_W, D_MODEL)
    for j in range(NA_NCB):
        c0 = _key_col_start(j)
        k_ref[:, j, :, :] = k3[:, c0:c0 + NA_KB, :].astype(BF16)
        v_ref[:, j, :, :] = v3[:, c0:c0 + NA_KB, :].astype(BF16)


def _qkv(x, mods, layer, norm_g, w_qkv, q_gain, k_gain, e_mat, et_mat, *, latent, need_q, tm):
    n = x.shape[0]
    tiles_per_batch = (SEQ // tm) if latent else None
    tok = pl.BlockSpec((tm, D_MODEL), lambda t: (t, 0))
    out_specs, out_shapes = [], []
    if need_q:
        out_specs.append(tok)
        out_shapes.append(jax.ShapeDtypeStruct((n, D_MODEL), BF16))
    if latent:
        rows = tm // GRID_W
        kv_spec = pl.BlockSpec((rows, NA_NCB, NA_KB, D_MODEL), lambda t: (t, 0, 0, 0))
        kv_shape = jax.ShapeDtypeStruct((n // GRID_W, NA_NCB, NA_KB, D_MODEL), BF16)
    else:
        kv_spec, kv_shape = tok, jax.ShapeDtypeStruct((n, D_MODEL), BF16)
    out_specs += [kv_spec, kv_spec]
    out_shapes += [kv_shape, kv_shape]
    return pl.pallas_call(
        functools.partial(_qkv_kernel, need_q=need_q, gather=latent),
        grid=(n // tm,),
        in_specs=[
            tok,
            _vec_spec(D_MODEL),
            _mod_spec(layer, 0, tiles_per_batch),
            _mod_spec(layer, 1, tiles_per_batch),
            _resident((D_MODEL, 3 * D_MODEL)),
            _vec_spec(D_MODEL),
            _vec_spec(D_MODEL),
            _resident((D_MODEL, LANES)),
            _resident((LANES, D_MODEL)),
        ],
        out_specs=out_specs,
        out_shape=out_shapes,
        compiler_params=_params(1),
        name="qkv_lat" if latent else "qkv_ctx",
    )(x, norm_g, mods, mods, w_qkv, q_gain, k_gain, e_mat, et_mat)


def _pair_scores(q2, keys, bias_pair):
    lane = lax.broadcasted_iota(jnp.int32, (1, LANES), 1)
    first = lane < NA_HEAD_DIM
    zero = jnp.zeros_like(q2)
    q4 = jnp.concatenate([jnp.where(first, q2, zero), jnp.where(first, zero, q2)], axis=0)
    s = _dot_nt(q4, keys)
    if bias_pair is not None:
        nb = bias_pair[0].shape[1]
        s = jnp.concatenate([s[:, :nb] + jnp.concatenate(bias_pair, axis=0), s[:, nb:]], axis=1)
    return s


def _pair_softmax(s):
    tiles = [s[:, i:i + LANES] for i in range(0, s.shape[1], LANES)]
    mx = jnp.max(functools.reduce(jnp.maximum, tiles), axis=-1, keepdims=True)
    probs = [jnp.exp2(t - mx) for t in tiles]
    denom = jnp.sum(functools.reduce(lambda a, b: a + b, probs), axis=-1, keepdims=True)
    return jnp.concatenate([p.astype(BF16) for p in probs], axis=1), denom


def _pair_pv(p, denom, values):
    m_rows = p.shape[0] // 2
    o = _dot(p, values) / denom
    lane = lax.broadcasted_iota(jnp.int32, (1, LANES), 1)
    return jnp.where(lane < NA_HEAD_DIM, o[:m_rows], o[m_rows:])


def _attn_lat_kernel(q_ref, k0, k1, k2, k3, v0, v1, v2, v3, kc_ref, vc_ref, bias_ref, o_ref):
    kv_rows = KEY_ROW_GROUP * NA_KB

    def lanes(hp):
        return slice(hp * LANES, (hp + 1) * LANES)

    def gather(refs, ctx_ref, hp):
        return jnp.concatenate(
            [r[:, :, lanes(hp)].reshape(kv_rows, LANES) for r in refs] + [ctx_ref[:, lanes(hp)]], axis=0)

    def scores(hp):
        q2 = q_ref[:, :, lanes(hp)].reshape(Q_PATCH, LANES)
        return _pair_scores(q2, gather((k0, k1, k2, k3), kc_ref, hp),
                            (bias_ref[2 * hp], bias_ref[2 * hp + 1]))

    s_next, prev = scores(0), None
    for t in range(HEAD_PAIRS + 1):
        s = s_next
        if t + 1 < HEAD_PAIRS:
            s_next = scores(t + 1)
        if prev is not None:
            o = _pair_pv(*prev, gather((v0, v1, v2, v3), vc_ref, t - 1))
            o_ref[:, :, lanes(t - 1)] = o.reshape(Q_ROWS, NA_KW, LANES).astype(BF16)
        if t < HEAD_PAIRS:
            prev = _pair_softmax(s)


def _key_group_start(rb):
    return jnp.clip(2 * rb - 1, 0, GRID_ROWS // KEY_ROW_GROUP - KEY_GROUPS)


def _attn_lat(q, k, v, kc, vc, bias, attn_layer):
    n_rb = GRID_ROWS // Q_ROWS
    groups_per_batch = GRID_ROWS // KEY_ROW_GROUP
    q3 = q.reshape(BATCH * GRID_ROWS, GRID_W, D_MODEL)
    q_spec = pl.BlockSpec((Q_ROWS, NA_KW, D_MODEL), lambda rb, j, b: (b * n_rb + rb, j, 0))

    def kv_spec(i):
        return pl.BlockSpec(
            (KEY_ROW_GROUP, None, NA_KB, D_MODEL),
            lambda rb, j, b: (b * groups_per_batch + _key_group_start(rb) + i, j, 0, 0))

    ctx_spec = pl.BlockSpec((CTX_LEN, D_MODEL), lambda rb, j, b: (b, 0))

    def bias_index(rb, j, b):
        variant = (rb > 0).astype(jnp.int32) + (rb == n_rb - 1).astype(jnp.int32)
        return (variant * NA_NCB + j, attn_layer, 0, 0)

    bias_spec = pl.BlockSpec((None, NA_HEADS, Q_PATCH, LAT_KEYS), bias_index)
    out = pl.pallas_call(
        _attn_lat_kernel,
        grid=(n_rb, NA_NCB, BATCH),
        in_specs=[q_spec] + [kv_spec(i) for i in range(KEY_GROUPS)] * 2 + [ctx_spec, ctx_spec, bias_spec],
        out_specs=q_spec,
        out_shape=jax.ShapeDtypeStruct(q3.shape, BF16),
        compiler_params=_params(3),
        name="attn_lat",
    )(q3, k, k, k, k, v, v, v, v, kc, vc, bias)
    return out.reshape(BATCH * SEQ, D_MODEL)


def _attn_ctx_kernel(q_ref, k_ref, v_ref, o_ref):
    for hp in range(HEAD_PAIRS):
        sl = slice(hp * LANES, (hp + 1) * LANES)
        p, denom = _pair_softmax(_pair_scores(q_ref[:, sl], k_ref[:, sl], None))
        o_ref[:, sl] = _pair_pv(p, denom, v_ref[:, sl]).astype(BF16)


def _attn_ctx(qc, kc, vc):
    spec = pl.BlockSpec((CTX_LEN, D_MODEL), lambda b: (b, 0))
    return pl.pallas_call(
        _attn_ctx_kernel,
        grid=(BATCH,),
        in_specs=[spec, spec, spec],
        out_specs=spec,
        out_shape=jax.ShapeDtypeStruct(qc.shape, BF16),
        compiler_params=_params(1),
        name="attn_ctx",
    )(qc, kc, vc)


def _attention_bias_tables(rpb):
    n_rb = GRID_ROWS // Q_ROWS
    n_heads = rpb.shape[0]
    col_pad, row_pad = NA_KW, Q_ROWS
    rpb_p = jnp.pad(rpb, ((0, 0), (row_pad, row_pad), (col_pad, col_pad)))
    by_col = []
    for j in range(NA_NCB):
        col_off = _key_col_start(j) - j * NA_KW + NA_KW - 1 + col_pad
        by_col.append(jnp.stack([rpb_p[:, :, col_off - qi:col_off - qi + NA_KB]
                                 for qi in range(NA_KW)], axis=1).reshape(n_heads, NA_KW, -1))
    tables = []
    for rb in (0, 1, n_rb - 1):
        g0 = int(np.clip(2 * rb - 1, 0, GRID_ROWS // KEY_ROW_GROUP - KEY_GROUPS))
        row_off = g0 * KEY_ROW_GROUP - rb * Q_ROWS + NA_KH - 1 + row_pad
        for j in range(NA_NCB):
            by_row = jnp.stack(
                [by_col[j][:, :, (row_off - ql) * NA_KB:(row_off - ql) * NA_KB + LAT_KEYS]
                 for ql in range(Q_ROWS)], axis=1)
            by_row = by_row.reshape(n_heads, Q_PATCH, LAT_KEYS)
            qrow = (rb * Q_ROWS + np.arange(Q_ROWS))[:, None, None, None]
            qcol = (j * NA_KW + np.arange(NA_KW))[None, :, None, None]
            krow = (g0 * KEY_ROW_GROUP + np.arange(KEY_ROWS))[None, None, :, None]
            kcol = (_key_col_start(j) + np.arange(NA_KB))[None, None, None, :]
            rs = np.clip(qrow - NA_KH // 2, 0, GRID_ROWS - NA_KH)
            ws = np.clip(qcol - NA_KW // 2, 0, GRID_W - NA_KW)
            ok = (krow >= rs) & (krow < rs + NA_KH) & (kcol >= ws) & (kcol < ws + NA_KW)
            ok = np.broadcast_to(ok, (Q_ROWS, NA_KW, KEY_ROWS, NA_KB)).reshape(Q_PATCH, LAT_KEYS)
            tables.append(jnp.where(ok[None], by_row, NEG_INF))
    return jnp.stack(tables, axis=0)


def _mlp_kernel(*refs, with_proj):
    if with_proj:
        x_ref, a_ref, wo_ref, g1_ref, refs = refs[0], refs[1], refs[2], refs[3], refs[4:]
        x = x_ref[...] + g1_ref[...] * _dot(a_ref[...], wo_ref[...])
    else:
        x_ref, refs = refs[0], refs[1:]
        x = x_ref[...]
    n_ref, sh_ref, sc_ref, g2_ref, w1_ref, w2_ref, o_ref = refs
    h = _rms_mod(x, n_ref[...], sh_ref[...], sc_ref[...]).astype(BF16)
    t = jnp.maximum(_dot(h, w1_ref[...]), 0.0)
    t = (t * t).astype(BF16)
    o_ref[...] = x + g2_ref[...] * _dot(t, w2_ref[...])


def _mlp(x, mods, layer, norm_g, w1, w2, *, latent, tm, attn=None, w_o=None):
    n = x.shape[0]
    tiles_per_batch = (SEQ // tm) if latent else None
    tok = pl.BlockSpec((tm, D_MODEL), lambda t: (t, 0))
    with_proj = attn is not None
    in_specs, args = [tok], [x]
    if with_proj:
        in_specs += [tok, _resident((D_MODEL, D_MODEL)), _mod_spec(layer, 2, tiles_per_batch)]
        args += [attn, w_o, mods]
    in_specs += [
        _vec_spec(D_MODEL),
        _mod_spec(layer, 3, tiles_per_batch),
        _mod_spec(layer, 4, tiles_per_batch),
        _mod_spec(layer, 5, tiles_per_batch),
        _resident((D_MODEL, MLP_HIDDEN)),
        _resident((MLP_HIDDEN, D_MODEL)),
    ]
    args += [norm_g, mods, mods, mods, w1, w2]
    return pl.pallas_call(
        functools.partial(_mlp_kernel, with_proj=with_proj),
        grid=(n // tm,),
        in_specs=in_specs,
        out_specs=tok,
        out_shape=jax.ShapeDtypeStruct((n, D_MODEL), F32),
        compiler_params=_params(1),
        name=("proj_mlp" if with_proj else "mlp") + ("_lat" if latent else "_ctx"),
    )(*args)


SG_COLS = 512


def _sg_kernel(x_ref, n_ref, sh_ref, sc_ref, g1_ref, win_ref, bin_ref, lng_ref, lnb_ref,
               ws_ref, bs_ref, wo_ref, o_ref, v_scr, t_scr):
    x = x_ref[...]
    tm = x.shape[0]
    n_chunks = tm // SG_CHUNK
    h = _rms_mod(x, n_ref[...], sh_ref[...], sc_ref[...]).astype(BF16)
    for c in range(SG_HALF // SG_COLS):
        cols = slice(SG_HALF + c * SG_COLS, SG_HALF + (c + 1) * SG_COLS)
        v_scr[:, c * SG_COLS:(c + 1) * SG_COLS] = _gelu(_dot(h, win_ref[:, cols]) + bin_ref[:, cols])
    v = v_scr[...]
    mu = jnp.mean(v, axis=-1, keepdims=True)
    dv = v - mu
    rstd = lax.rsqrt(jnp.mean(dv * dv, axis=-1, keepdims=True) + EPS)
    for c in range(SG_HALF // SG_COLS):
        cols = slice(c * SG_COLS, (c + 1) * SG_COLS)
        u = _gelu(_dot(h, win_ref[:, cols]) + bin_ref[:, cols])
        vn = ((v_scr[:, cols] - mu) * rstd * lng_ref[:, cols] + lnb_ref[:, cols]).astype(BF16)
        for gg in range(SG_COLS // SG_GROUP_CH):
            g = c * (SG_COLS // SG_GROUP_CH) + gg
            gl = slice(gg * SG_GROUP_CH, (gg + 1) * SG_GROUP_CH)
            rhs = jnp.concatenate(
                [vn[i * SG_CHUNK:(i + 1) * SG_CHUNK, gl] for i in range(n_chunks)], axis=1)
            s = _dot(ws_ref[g], rhs)
            for i in range(n_chunks):
                rows = slice(i * SG_CHUNK, (i + 1) * SG_CHUNK)
                s_i = s[:, i * SG_GROUP_CH:(i + 1) * SG_GROUP_CH] + bs_ref[g]
                t_scr[rows, g * SG_GROUP_CH:(g + 1) * SG_GROUP_CH] = (u[rows, gl] * s_i).astype(BF16)
    o_ref[...] = x + g1_ref[...] * _dot(t_scr[...], wo_ref[...])


def _sg(x, mods, layer, norm_g, w_in, b_in, ln_g, ln_b, w_s, b_s, w_o, *, latent, tm):
    n = x.shape[0]
    tiles_per_batch = (SEQ // tm) if latent else None
    tok = pl.BlockSpec((tm, D_MODEL), lambda t: (t, 0))
    return pl.pallas_call(
        _sg_kernel,
        grid=(n // tm,),
        in_specs=[
            tok,
            _vec_spec(D_MODEL),
            _mod_spec(layer, 0, tiles_per_batch),
            _mod_spec(layer, 1, tiles_per_batch),
            _mod_spec(layer, 2, tiles_per_batch),
            _resident((D_MODEL, 2 * SG_HALF)),
            _vec_spec(2 * SG_HALF),
            _vec_spec(SG_HALF),
            _vec_spec(SG_HALF),
            _resident((SG_GROUPS, SG_CHUNK, SG_CHUNK)),
            _resident((SG_GROUPS, SG_CHUNK, SG_GROUP_CH)),
            _resident((SG_HALF, D_MODEL)),
        ],
        out_specs=tok,
        out_shape=jax.ShapeDtypeStruct((n, D_MODEL), F32),
        scratch_shapes=[pltpu.VMEM((tm, SG_HALF), F32), pltpu.VMEM((tm, SG_HALF), BF16)],
        compiler_params=_params(1),
        name="sg_lat" if latent else "sg_ctx",
    )(x, norm_g, mods, mods, mods, w_in, b_in, ln_g, ln_b, w_s, b_s, w_o)


LAT_TM = 512
CTX_TM = 256


def kernel(x, c, ctx, c_ctx, ada_w, ada_b, norm1_g, norm2_g, mlp_w1, mlp_w2,
           na_w_qkv, na_q_norm, na_k_norm, na_rpb, na_w_o,
           sg_w_in, sg_b_in, sg_ln_g, sg_ln_b, sg_w_s, sg_b_s, sg_w_o):
    last_ctx_layer = ((DEPTH - 1) // N_MIXERS) * N_MIXERS
    xl = x.reshape(BATCH * SEQ, D_MODEL)
    xc = ctx.reshape(BATCH * CTX_LEN, D_MODEL)

    cond = jnp.concatenate(
        [c, c_ctx[None, :], jnp.zeros((MOD_ROWS - BATCH - 1, D_MODEL), F32)], axis=0)
    mods = _adaln(cond, ada_w, ada_b).reshape(DEPTH * MOD_ROWS * 6, 1, D_MODEL)

    head_of = np.arange(D_MODEL) // NA_HEAD_DIM
    e_np = (head_of[:, None] == np.arange(LANES)[None, :]).astype(np.float32)
    e_mat = jnp.asarray(e_np / NA_HEAD_DIM, BF16)
    et_mat = jnp.asarray(e_np.T, BF16)
    bias = _attention_bias_tables(
        (na_rpb * LOG2_E).reshape((-1,) + na_rpb.shape[2:]))

    for i in range(DEPTH):
        ctx_kv = i <= last_ctx_layer
        ctx_full = i < last_ctx_layer
        n1 = norm1_g[i][None, :]
        n2 = norm2_g[i][None, :]
        w1 = mlp_w1[i].astype(BF16)
        w2 = mlp_w2[i].astype(BF16)
        if i % N_MIXERS == 0:
            a = i // N_MIXERS
            w_qkv = na_w_qkv[a].astype(BF16)
            w_o = na_w_o[a].astype(BF16)
            q_gain = jnp.tile(na_q_norm[a] * (NA_HEAD_DIM ** -0.5 * LOG2_E), NA_HEADS)[None, :]
            k_gain = jnp.tile(na_k_norm[a], NA_HEADS)[None, :]
            q, k, v = _qkv(xl, mods, i, n1, w_qkv, q_gain, k_gain, e_mat, et_mat,
                           latent=True, need_q=True, tm=LAT_TM)
            ctx_proj = _qkv(xc, mods, i, n1, w_qkv, q_gain, k_gain, e_mat, et_mat,
                            latent=False, need_q=ctx_full, tm=CTX_TM)
            kc, vc = ctx_proj[-2], ctx_proj[-1]
            att = _attn_lat(q, k, v, kc, vc, bias, a)
            xl = _mlp(xl, mods, i, n2, w1, w2, latent=True, tm=LAT_TM, attn=att, w_o=w_o)
            if ctx_full:
                att_c = _attn_ctx(ctx_proj[0], kc, vc)
                xc = _mlp(xc, mods, i, n2, w1, w2, latent=False, tm=CTX_TM, attn=att_c, w_o=w_o)
        else:
            s = i // N_MIXERS
            sg_args = (sg_w_in[s].astype(BF16), sg_b_in[s][None, :], sg_ln_g[s][None, :],
                       sg_ln_b[s][None, :], sg_w_s[s].astype(BF16),
                       jnp.broadcast_to(sg_b_s[s][:, :, None], (SG_GROUPS, SG_CHUNK, SG_GROUP_CH)),
                       sg_w_o[s].astype(BF16))
            xl = _sg(xl, mods, i, n1, *sg_args, latent=True, tm=LAT_TM)
            xl = _mlp(xl, mods, i, n2, w1, w2, latent=True, tm=LAT_TM)
            if ctx_full:
                xc = _sg(xc, mods, i, n1, *sg_args, latent=False, tm=CTX_TM)
                xc = _mlp(xc, mods, i, n2, w1, w2, latent=False, tm=CTX_TM)
    return xl.reshape(BATCH, SEQ, D_MODEL)
```

```python
import functools

import numpy as np
import jax
import jax.numpy as jnp
from jax import lax
from jax.experimental import pallas as pl
from jax.experimental.pallas import tpu as pltpu

D_MODEL = 1024
BATCH = 8
SEQ = 4096
DEPTH = 4
GRID_W = 64
GRID_ROWS = SEQ // GRID_W
CTX_LEN = 256
N_MIXERS = 2
NA_HEADS = 16
NA_HEAD_DIM = D_MODEL // NA_HEADS
NA_KH = 8
NA_KW = 16
NA_KB = 2 * NA_KW
NA_NCB = GRID_W // NA_KW
SG_CHUNK = 128
SG_HALF = 3 * D_MODEL
SG_GROUP_CH = 128
SG_GROUPS = SG_HALF // SG_GROUP_CH
MLP_HIDDEN = 4 * D_MODEL
EPS = 1e-6
NEG_INF = -1e30
LOG2_E = float(np.log2(np.e))
EXP2_SAFE_BOUND = 60.0

F32 = jnp.float32
BF16 = jnp.bfloat16

LANES = 128
HEAD_PAIRS = NA_HEADS // 2
MOD_ROWS = 16
CTX_MOD_ROW = BATCH
Q_ROWS = 8
Q_PATCH = Q_ROWS * NA_KW
KEY_ROW_GROUP = 4
KEY_GROUPS = 4
KEY_ROWS = KEY_ROW_GROUP * KEY_GROUPS
LAT_KEYS = KEY_ROWS * NA_KB
ATTN_BATCHES = 4
VMEM_LIMIT = 56 * 1024 * 1024


def _dot(a, b):
    return jnp.dot(a, b, preferred_element_type=F32)


def _dot_nt(a, b):
    return lax.dot_general(a, b, (((1,), (1,)), ((), ())), preferred_element_type=F32)


def _rms_mod(x, g, shift, scale):
    ms = jnp.mean(x * x, axis=-1, keepdims=True)
    h = x * lax.rsqrt(ms + EPS) * g
    return h * (1.0 + scale) + shift


def _gelu(z):
    return 0.5 * z * (1.0 + lax.erf(z * np.float32(np.sqrt(0.5))))


def _split_bf16(a):
    hi = a.astype(BF16)
    lo = (a - hi.astype(F32)).astype(BF16)
    return hi, lo


def _params(n_axes):
    return pltpu.CompilerParams(dimension_semantics=("arbitrary",) * n_axes,
                                vmem_limit_bytes=VMEM_LIMIT)


def _resident(shape):
    nd = len(shape)
    return pl.BlockSpec(shape, lambda *_: (0,) * nd, pipeline_mode=pl.Buffered(1))


def _adaln_kernel(c_ref, w_ref, b_ref, o_ref):
    c = c_ref[...]
    s = c * (1.0 / (1.0 + jnp.exp(-c)))
    s_hi, s_lo = _split_bf16(s)
    w_hi, w_lo = _split_bf16(w_ref[...])
    o_ref[...] = _dot(s_hi, w_hi) + (_dot(s_lo, w_hi) + _dot(s_hi, w_lo)) + b_ref[...]


def _adaln(cond, ada_w, ada_b):
    tn = D_MODEL
    return pl.pallas_call(
        _adaln_kernel,
        grid=(DEPTH, 6 * D_MODEL // tn),
        in_specs=[
            pl.BlockSpec((MOD_ROWS, D_MODEL), lambda i, n: (0, 0)),
            pl.BlockSpec((None, D_MODEL, tn), lambda i, n: (i, 0, n)),
            pl.BlockSpec((None, 1, tn), lambda i, n: (i, 0, n)),
        ],
        out_specs=pl.BlockSpec((None, MOD_ROWS, tn), lambda i, n: (i, 0, n)),
        out_shape=jax.ShapeDtypeStruct((DEPTH, MOD_ROWS, 6 * D_MODEL), F32),
        compiler_params=_params(2),
        name="adaln",
    )(cond, ada_w, ada_b.reshape(DEPTH, 1, 6 * D_MODEL))


def _mod_spec(layer, k, tiles_per_batch):
    base = layer * MOD_ROWS * 6 + k
    if tiles_per_batch is None:
        return pl.BlockSpec((None, 1, D_MODEL), lambda t: (base + CTX_MOD_ROW * 6, 0, 0))
    return pl.BlockSpec((None, 1, D_MODEL), lambda t: (base + (t // tiles_per_batch) * 6, 0, 0))


def _vec_spec(n):
    return pl.BlockSpec((1, n), lambda t: (0, 0))


def _key_col_start(j):
    return int(np.clip(j * NA_KW - NA_KW // 2, 0, GRID_W - NA_KB))


def _qkv_kernel(x_ref, g_ref, sh_ref, sc_ref, w_ref, qg_ref, kg_ref, e_ref, et_ref, *out_refs,
                need_q, gather):
    h = _rms_mod(x_ref[...], g_ref[...], sh_ref[...], sc_ref[...]).astype(BF16)

    def head_mean_square(y):
        return _dot((y * y).astype(BF16), e_ref[...])

    def head_norm(y, ms, gain):
        r_hi, r_lo = _split_bf16(lax.rsqrt(ms + EPS))
        rb = _dot(r_hi, et_ref[...]) + _dot(r_lo, et_ref[...])
        return y * rb * gain

    refs = list(out_refs)
    q_pre = _dot(h, w_ref[:, 0:D_MODEL]) if need_q else None
    k_pre = _dot(h, w_ref[:, D_MODEL:2 * D_MODEL])
    v = _dot(h, w_ref[:, 2 * D_MODEL:3 * D_MODEL])
    q_ms = head_mean_square(q_pre) if need_q else None
    k_ms = head_mean_square(k_pre)
    if need_q:
        refs.pop(0)[...] = head_norm(q_pre, q_ms, qg_ref[...]).astype(BF16)
    k_ref, v_ref = refs
    k = head_norm(k_pre, k_ms, kg_ref[...])
    if not gather:
        k_ref[...] = k.astype(BF16)
        v_ref[...] = v.astype(BF16)
        return
    rows = k.shape[0] // GRID_W
    k3 = k.reshape(rows, GRID_W, D_MODEL)
    v3 = v.reshape(rows, GRID_W, D_MODEL)
    for j in range(NA_NCB):
        c0 = _key_col_start(j)
        k_ref[:, j, :, :] = k3[:, c0:c0 + NA_KB, :].astype(BF16)
        v_ref[:, j, :, :] = v3[:, c0:c0 + NA_KB, :].astype(BF16)


def _qkv(x, mods, layer, norm_g, w_qkv, q_gain, k_gain, e_mat, et_mat, *, latent, need_q, tm):
    n = x.shape[0]
    tiles_per_batch = (SEQ // tm) if latent else None
    tok = pl.BlockSpec((tm, D_MODEL), lambda t: (t, 0))
    out_specs, out_shapes = [], []
    if need_q:
        out_specs.append(tok)
        out_shapes.append(jax.ShapeDtypeStruct((n, D_MODEL), BF16))
    if latent:
        rows = tm // GRID_W
        kv_spec = pl.BlockSpec((rows, NA_NCB, NA_KB, D_MODEL), lambda t: (t, 0, 0, 0))
        kv_shape = jax.ShapeDtypeStruct((n // GRID_W, NA_NCB, NA_KB, D_MODEL), BF16)
    else:
        kv_spec, kv_shape = tok, jax.ShapeDtypeStruct((n, D_MODEL), BF16)
    out_specs += [kv_spec, kv_spec]
    out_shapes += [kv_shape, kv_shape]
    return pl.pallas_call(
        functools.partial(_qkv_kernel, need_q=need_q, gather=latent),
        grid=(n // tm,),
        in_specs=[
            tok,
            _vec_spec(D_MODEL),
            _mod_spec(layer, 0, tiles_per_batch),
            _mod_spec(layer, 1, tiles_per_batch),
            _resident((D_MODEL, 3 * D_MODEL)),
            _vec_spec(D_MODEL),
            _vec_spec(D_MODEL),
            _resident((D_MODEL, LANES)),
            _resident((LANES, D_MODEL)),
        ],
        out_specs=out_specs,
        out_shape=out_shapes,
        compiler_params=_params(1),
        name="qkv_lat" if latent else "qkv_ctx",
    )(x, norm_g, mods, mods, w_qkv, q_gain, k_gain, e_mat, et_mat)


def _pair_scores(q2, keys, bias_pair=None, shift=None):
    lane = lax.broadcasted_iota(jnp.int32, (1, LANES), 1)
    first = lane < NA_HEAD_DIM
    zero = jnp.zeros_like(q2)
    q4 = jnp.concatenate([jnp.where(first, q2, zero), jnp.where(first, zero, q2)], axis=0)
    s = _dot_nt(q4, keys)
    if bias_pair is None:
        return s
    nb = bias_pair[0].shape[1]
    rest = [s[:, i:i + LANES] - shift for i in range(nb, s.shape[1], LANES)]
    return jnp.concatenate([s[:, :nb] + jnp.concatenate(bias_pair, axis=0)] + rest, axis=1)


def _pair_softmax(s, row_max=True):
    tiles = [s[:, i:i + LANES] for i in range(0, s.shape[1], LANES)]
    if row_max:
        mx = jnp.max(functools.reduce(jnp.maximum, tiles), axis=-1, keepdims=True)
        tiles = [t - mx for t in tiles]
    probs = [jnp.exp2(t) for t in tiles]
    denom = jnp.sum(functools.reduce(lambda a, b: a + b, probs), axis=-1, keepdims=True)
    return jnp.concatenate([p.astype(BF16) for p in probs], axis=1), denom


def _pair_pv(p, denom, values):
    m_rows = p.shape[0] // 2
    o = _dot(p, values) / denom
    lane = lax.broadcasted_iota(jnp.int32, (1, LANES), 1)
    return jnp.where(lane < NA_HEAD_DIM, o[:m_rows], o[m_rows:])


def _attn_lat_kernel(q_ref, k0, k1, k2, k3, v0, v1, v2, v3, kc_ref, vc_ref, bias_ref, shift_ref,
                     o_ref, *, row_max):
    kv_rows = KEY_ROW_GROUP * NA_KB

    def lanes(hp):
        return slice(hp * LANES, (hp + 1) * LANES)

    def gather(refs, ctx_ref, t):
        bi, hp = divmod(t, HEAD_PAIRS)
        return jnp.concatenate(
            [r[bi, :, :, lanes(hp)].reshape(kv_rows, LANES) for r in refs] + [ctx_ref[bi, :, lanes(hp)]],
            axis=0)

    def scores(t):
        bi, hp = divmod(t, HEAD_PAIRS)
        q2 = q_ref[bi, :, :, lanes(hp)].reshape(Q_PATCH, LANES)
        return _pair_scores(q2, gather((k0, k1, k2, k3), kc_ref, t),
                            (bias_ref[2 * hp], bias_ref[2 * hp + 1]), shift_ref[...])

    n_stages = q_ref.shape[0] * HEAD_PAIRS
    s_next, prev = scores(0), None
    for t in range(n_stages + 1):
        s = s_next
        if t + 1 < n_stages:
            s_next = scores(t + 1)
        if prev is not None:
            bi, hp = divmod(t - 1, HEAD_PAIRS)
            o = _pair_pv(*prev, gather((v0, v1, v2, v3), vc_ref, t - 1))
            o_ref[bi, :, :, lanes(hp)] = o.reshape(Q_ROWS, NA_KW, LANES).astype(BF16)
        if t < n_stages:
            prev = _pair_softmax(s, row_max)


def _key_group_start(rb):
    return jnp.clip(2 * rb - 1, 0, GRID_ROWS // KEY_ROW_GROUP - KEY_GROUPS)


def _attn_lat(q, k, v, kc, vc, bias, attn_layer, logit_bound):
    n_rb = GRID_ROWS // Q_ROWS
    nb = ATTN_BATCHES
    q4 = q.reshape(BATCH, GRID_ROWS, GRID_W, D_MODEL)
    k5 = k.reshape(BATCH, GRID_ROWS, NA_NCB, NA_KB, D_MODEL)
    v5 = v.reshape(BATCH, GRID_ROWS, NA_NCB, NA_KB, D_MODEL)
    kc3 = kc.reshape(BATCH, CTX_LEN, D_MODEL)
    vc3 = vc.reshape(BATCH, CTX_LEN, D_MODEL)
    q_spec = pl.BlockSpec((nb, Q_ROWS, NA_KW, D_MODEL), lambda rb, j, b: (b, rb, j, 0))

    def kv_spec(i):
        return pl.BlockSpec(
            (nb, KEY_ROW_GROUP, None, NA_KB, D_MODEL),
            lambda rb, j, b: (b, _key_group_start(rb) + i, j, 0, 0))

    ctx_spec = pl.BlockSpec((nb, CTX_LEN, D_MODEL), lambda rb, j, b: (b, 0, 0))

    def bias_index(rb, j, b):
        variant = (rb > 0).astype(jnp.int32) + (rb == n_rb - 1).astype(jnp.int32)
        return (variant * NA_NCB + j, attn_layer, 0, 0)

    bias_spec = pl.BlockSpec((None, NA_HEADS, Q_PATCH, LAT_KEYS), bias_index)
    shift = jnp.full((1, LANES), logit_bound, F32)

    def run(row_max):
        return pl.pallas_call(
            functools.partial(_attn_lat_kernel, row_max=row_max),
            grid=(n_rb, NA_NCB, BATCH // nb),
            in_specs=([q_spec] + [kv_spec(i) for i in range(KEY_GROUPS)] * 2
                      + [ctx_spec, ctx_spec, bias_spec, pl.BlockSpec((1, LANES), lambda rb, j, b: (0, 0))]),
            out_specs=q_spec,
            out_shape=jax.ShapeDtypeStruct(q4.shape, BF16),
            compiler_params=_params(3),
            name="attn_lat_rowmax" if row_max else "attn_lat",
        )(q4, k5, k5, k5, k5, v5, v5, v5, v5, kc3, vc3, bias, shift)

    out = lax.cond(logit_bound <= EXP2_SAFE_BOUND, lambda: run(False), lambda: run(True))
    return out.reshape(BATCH * SEQ, D_MODEL)


def _attn_ctx_kernel(q_ref, k_ref, v_ref, o_ref):
    for hp in range(HEAD_PAIRS):
        sl = slice(hp * LANES, (hp + 1) * LANES)
        p, denom = _pair_softmax(_pair_scores(q_ref[:, sl], k_ref[:, sl], None))
        o_ref[:, sl] = _pair_pv(p, denom, v_ref[:, sl]).astype(BF16)


def _attn_ctx(qc, kc, vc):
    spec = pl.BlockSpec((CTX_LEN, D_MODEL), lambda b: (b, 0))
    return pl.pallas_call(
        _attn_ctx_kernel,
        grid=(BATCH,),
        in_specs=[spec, spec, spec],
        out_specs=spec,
        out_shape=jax.ShapeDtypeStruct(qc.shape, BF16),
        compiler_params=_params(1),
        name="attn_ctx",
    )(qc, kc, vc)


def _attention_bias_tables(rpb):
    n_rb = GRID_ROWS // Q_ROWS
    n_heads = rpb.shape[0]
    col_pad, row_pad = NA_KW, Q_ROWS
    rpb_p = jnp.pad(rpb, ((0, 0), (row_pad, row_pad), (col_pad, col_pad)))
    by_col = []
    for j in range(NA_NCB):
        col_off = _key_col_start(j) - j * NA_KW + NA_KW - 1 + col_pad
        by_col.append(jnp.stack([rpb_p[:, :, col_off - qi:col_off - qi + NA_KB]
                                 for qi in range(NA_KW)], axis=1).reshape(n_heads, NA_KW, -1))
    tables = []
    for rb in (0, 1, n_rb - 1):
        g0 = int(np.clip(2 * rb - 1, 0, GRID_ROWS // KEY_ROW_GROUP - KEY_GROUPS))
        row_off = g0 * KEY_ROW_GROUP - rb * Q_ROWS + NA_KH - 1 + row_pad
        for j in range(NA_NCB):
            by_row = jnp.stack(
                [by_col[j][:, :, (row_off - ql) * NA_KB:(row_off - ql) * NA_KB + LAT_KEYS]
                 for ql in range(Q_ROWS)], axis=1)
            by_row = by_row.reshape(n_heads, Q_PATCH, LAT_KEYS)
            qrow = (rb * Q_ROWS + np.arange(Q_ROWS))[:, None, None, None]
            qcol = (j * NA_KW + np.arange(NA_KW))[None, :, None, None]
            krow = (g0 * KEY_ROW_GROUP + np.arange(KEY_ROWS))[None, None, :, None]
            kcol = (_key_col_start(j) + np.arange(NA_KB))[None, None, None, :]
            rs = np.clip(qrow - NA_KH // 2, 0, GRID_ROWS - NA_KH)
            ws = np.clip(qcol - NA_KW // 2, 0, GRID_W - NA_KW)
            ok = (krow >= rs) & (krow < rs + NA_KH) & (kcol >= ws) & (kcol < ws + NA_KW)
            ok = np.broadcast_to(ok, (Q_ROWS, NA_KW, KEY_ROWS, NA_KB)).reshape(Q_PATCH, LAT_KEYS)
            tables.append(jnp.where(ok[None], by_row, NEG_INF))
    return jnp.stack(tables, axis=0)


def _mlp_kernel(*refs, with_proj):
    if with_proj:
        x_ref, a_ref, wo_ref, g1_ref, refs = refs[0], refs[1], refs[2], refs[3], refs[4:]
        x = x_ref[...] + g1_ref[...] * _dot(a_ref[...], wo_ref[...])
    else:
        x_ref, refs = refs[0], refs[1:]
        x = x_ref[...]
    n_ref, sh_ref, sc_ref, g2_ref, w1_ref, w2_ref, o_ref = refs
    h = _rms_mod(x, n_ref[...], sh_ref[...], sc_ref[...]).astype(BF16)
    t = jnp.maximum(_dot(h, w1_ref[...]), 0.0)
    t = (t * t).astype(BF16)
    o_ref[...] = x + g2_ref[...] * _dot(t, w2_ref[...])


def _mlp(x, mods, layer, norm_g, w1, w2, *, latent, tm, attn=None, w_o=None):
    n = x.shape[0]
    tiles_per_batch = (SEQ // tm) if latent else None
    tok = pl.BlockSpec((tm, D_MODEL), lambda t: (t, 0))
    with_proj = attn is not None
    in_specs, args = [tok], [x]
    if with_proj:
        in_specs += [tok, _resident((D_MODEL, D_MODEL)), _mod_spec(layer, 2, tiles_per_batch)]
        args += [attn, w_o, mods]
    in_specs += [
        _vec_spec(D_MODEL),
        _mod_spec(layer, 3, tiles_per_batch),
        _mod_spec(layer, 4, tiles_per_batch),
        _mod_spec(layer, 5, tiles_per_batch),
        _resident((D_MODEL, MLP_HIDDEN)),
        _resident((MLP_HIDDEN, D_MODEL)),
    ]
    args += [norm_g, mods, mods, mods, w1, w2]
    return pl.pallas_call(
        functools.partial(_mlp_kernel, with_proj=with_proj),
        grid=(n // tm,),
        in_specs=in_specs,
        out_specs=tok,
        out_shape=jax.ShapeDtypeStruct((n, D_MODEL), F32),
        compiler_params=_params(1),
        name=("proj_mlp" if with_proj else "mlp") + ("_lat" if latent else "_ctx"),
    )(*args)


SG_COLS = 512


def _sg_kernel(x_ref, n_ref, sh_ref, sc_ref, g1_ref, win_ref, bin_ref, lng_ref, lnb_ref,
               ws_ref, bs_ref, wo_ref, o_ref, v_scr, t_scr):
    x = x_ref[...]
    tm = x.shape[0]
    n_chunks = tm // SG_CHUNK
    h = _rms_mod(x, n_ref[...], sh_ref[...], sc_ref[...]).astype(BF16)
    n_steps = SG_HALF // SG_COLS
    groups_per_step = SG_COLS // SG_GROUP_CH

    def cols(c):
        return slice(c * SG_COLS, (c + 1) * SG_COLS)

    def lane_tile_sum(a):
        return functools.reduce(lambda p, q: p + q,
                                [a[:, i:i + LANES] for i in range(0, a.shape[1], LANES)])

    s1 = s2 = None
    for c in range(n_steps):
        vc = slice(SG_HALF + c * SG_COLS, SG_HALF + (c + 1) * SG_COLS)
        v_blk = _gelu(_dot(h, win_ref[:, vc]) + bin_ref[:, vc])
        v_scr[:, cols(c)] = v_blk
        p1, p2 = lane_tile_sum(v_blk), lane_tile_sum(v_blk * v_blk)
        s1, s2 = (p1, p2) if s1 is None else (s1 + p1, s2 + p2)
    mu = jnp.sum(s1, axis=-1, keepdims=True) * (1.0 / SG_HALF)
    var = jnp.sum(s2, axis=-1, keepdims=True) * (1.0 / SG_HALF) - mu * mu
    rstd = lax.rsqrt(var + EPS)

    for c in range(n_steps):
        u = _gelu(_dot(h, win_ref[:, cols(c)]) + bin_ref[:, cols(c)])
        vn = ((v_scr[:, cols(c)] - mu) * rstd * lng_ref[:, cols(c)] + lnb_ref[:, cols(c)]).astype(BF16)
        for gg in range(groups_per_step):
            g = c * groups_per_step + gg
            gl = slice(gg * SG_GROUP_CH, (gg + 1) * SG_GROUP_CH)
            rhs = jnp.concatenate(
                [vn[i * SG_CHUNK:(i + 1) * SG_CHUNK, gl] for i in range(n_chunks)], axis=1)
            s = _dot(ws_ref[g], rhs)
            for i in range(n_chunks):
                rows = slice(i * SG_CHUNK, (i + 1) * SG_CHUNK)
                s_i = s[:, i * SG_GROUP_CH:(i + 1) * SG_GROUP_CH] + bs_ref[g]
                t_scr[rows, g * SG_GROUP_CH:(g + 1) * SG_GROUP_CH] = (u[rows, gl] * s_i).astype(BF16)
    o_ref[...] = x + g1_ref[...] * _dot(t_scr[...], wo_ref[...])


def _sg(x, mods, layer, norm_g, w_in, b_in, ln_g, ln_b, w_s, b_s, w_o, *, latent, tm):
    n = x.shape[0]
    tiles_per_batch = (SEQ // tm) if latent else None
    tok = pl.BlockSpec((tm, D_MODEL), lambda t: (t, 0))
    return pl.pallas_call(
        _sg_kernel,
        grid=(n // tm,),
        in_specs=[
            tok,
            _vec_spec(D_MODEL),
            _mod_spec(layer, 0, tiles_per_batch),
            _mod_spec(layer, 1, tiles_per_batch),
            _mod_spec(layer, 2, tiles_per_batch),
            _resident((D_MODEL, 2 * SG_HALF)),
            _vec_spec(2 * SG_HALF),
            _vec_spec(SG_HALF),
            _vec_spec(SG_HALF),
            _resident((SG_GROUPS, SG_CHUNK, SG_CHUNK)),
            _resident((SG_GROUPS, SG_CHUNK, SG_GROUP_CH)),
            _resident((SG_HALF, D_MODEL)),
        ],
        out_specs=tok,
        out_shape=jax.ShapeDtypeStruct((n, D_MODEL), F32),
        scratch_shapes=[pltpu.VMEM((tm, SG_HALF), F32), pltpu.VMEM((tm, SG_HALF), BF16)],
        compiler_params=_params(1),
        name="sg_lat" if latent else "sg_ctx",
    )(x, norm_g, mods, mods, mods, w_in, b_in, ln_g, ln_b, w_s, b_s, w_o)


LAT_TM = 512
CTX_TM = 256


def kernel(x, c, ctx, c_ctx, ada_w, ada_b, norm1_g, norm2_g, mlp_w1, mlp_w2,
           na_w_qkv, na_q_norm, na_k_norm, na_rpb, na_w_o,
           sg_w_in, sg_b_in, sg_ln_g, sg_ln_b, sg_w_s, sg_b_s, sg_w_o):
    last_ctx_layer = ((DEPTH - 1) // N_MIXERS) * N_MIXERS
    xl = x.reshape(BATCH * SEQ, D_MODEL)
    xc = ctx.reshape(BATCH * CTX_LEN, D_MODEL)

    cond = jnp.concatenate(
        [c, c_ctx[None, :], jnp.zeros((MOD_ROWS - BATCH - 1, D_MODEL), F32)], axis=0)
    mods = _adaln(cond, ada_w, ada_b).reshape(DEPTH * MOD_ROWS * 6, 1, D_MODEL)

    head_of = np.arange(D_MODEL) // NA_HEAD_DIM
    e_np = (head_of[:, None] == np.arange(LANES)[None, :]).astype(np.float32)
    e_mat = jnp.asarray(e_np / NA_HEAD_DIM, BF16)
    et_mat = jnp.asarray(e_np.T, BF16)
    q_gains = na_q_norm * (NA_HEAD_DIM ** -0.5 * LOG2_E)
    rpb2 = na_rpb * LOG2_E
    logit_bounds = (NA_HEAD_DIM * 1.01 * jnp.max(jnp.abs(q_gains), axis=1)
                    * jnp.max(jnp.abs(na_k_norm), axis=1) + jnp.max(jnp.abs(rpb2), axis=(1, 2, 3)))
    bias = _attention_bias_tables(
        (rpb2 - logit_bounds[:, None, None, None]).reshape((-1,) + na_rpb.shape[2:]))

    for i in range(DEPTH):
        ctx_kv = i <= last_ctx_layer
        ctx_full = i < last_ctx_layer
        n1 = norm1_g[i][None, :]
        n2 = norm2_g[i][None, :]
        w1 = mlp_w1[i].astype(BF16)
        w2 = mlp_w2[i].astype(BF16)
        if i % N_MIXERS == 0:
            a = i // N_MIXERS
            w_qkv = na_w_qkv[a].astype(BF16)
            w_o = na_w_o[a].astype(BF16)
            q_gain = jnp.tile(q_gains[a], NA_HEADS)[None, :]
            k_gain = jnp.tile(na_k_norm[a], NA_HEADS)[None, :]
            q, k, v = _qkv(xl, mods, i, n1, w_qkv, q_gain, k_gain, e_mat, et_mat,
                           latent=True, need_q=True, tm=LAT_TM)
            ctx_proj = _qkv(xc, mods, i, n1, w_qkv, q_gain, k_gain, e_mat, et_mat,
                            latent=False, need_q=ctx_full, tm=CTX_TM)
            kc, vc = ctx_proj[-2], ctx_proj[-1]
            att = _attn_lat(q, k, v, kc, vc, bias, a, logit_bounds[a])
            xl = _mlp(xl, mods, i, n2, w1, w2, latent=True, tm=LAT_TM, attn=att, w_o=w_o)
            if ctx_full:
                att_c = _attn_ctx(ctx_proj[0], kc, vc)
                xc = _mlp(xc, mods, i, n2, w1, w2, latent=False, tm=CTX_TM, attn=att_c, w_o=w_o)
        else:
            s = i // N_MIXERS
            sg_args = (sg_w_in[s].astype(BF16), sg_b_in[s][None, :], sg_ln_g[s][None, :],
                       sg_ln_b[s][None, :], sg_w_s[s].astype(BF16),
                       jnp.broadcast_to(sg_b_s[s][:, :, None], (SG_GROUPS, SG_CHUNK, SG_GROUP_CH)),
                       sg_w_o[s].astype(BF16))
            xl = _sg(xl, mods, i, n1, *sg_args, latent=True, tm=LAT_TM)
            xl = _mlp(xl, mods, i, n2, w1, w2, latent=True, tm=LAT_TM)
            if ctx_full:
                xc = _sg(xc, mods, i, n1, *sg_args, latent=False, tm=CTX_TM)
                xc = _mlp(xc, mods, i, n2, w1, w2, latent=False, tm=CTX_TM)
    return xl.reshape(BATCH, SEQ, D_MODEL)
```

```python
import functools

import numpy as np
import jax
import jax.numpy as jnp
from jax import lax
from jax.experimental import pallas as pl
from jax.experimental.pallas import tpu as pltpu

D_MODEL = 1024
BATCH = 8
SEQ = 4096
DEPTH = 4
GRID_W = 64
GRID_ROWS = SEQ // GRID_W
CTX_LEN = 256
N_MIXERS = 2
NA_HEADS = 16
NA_HEAD_DIM = D_MODEL // NA_HEADS
NA_KH = 8
NA_KW = 16
NA_KB = 2 * NA_KW
NA_NCB = GRID_W // NA_KW
SG_CHUNK = 128
SG_HALF = 3 * D_MODEL
SG_GROUP_CH = 128
SG_GROUPS = SG_HALF // SG_GROUP_CH
MLP_HIDDEN = 4 * D_MODEL
EPS = 1e-6
NEG_INF = -1e30
LOG2_E = float(np.log2(np.e))
EXP2_SAFE_BOUND = 60.0

F32 = jnp.float32
BF16 = jnp.bfloat16

LANES = 128
HEAD_PAIRS = NA_HEADS // 2
MOD_ROWS = 16
CTX_MOD_ROW = BATCH
Q_ROWS = 8
Q_PATCH = Q_ROWS * NA_KW
KEY_ROW_GROUP = 4
KEY_GROUPS = 4
KEY_ROWS = KEY_ROW_GROUP * KEY_GROUPS
LAT_KEYS = KEY_ROWS * NA_KB
ATTN_BATCHES = 4
VMEM_LIMIT = 56 * 1024 * 1024


def _dot(a, b):
    return jnp.dot(a, b, preferred_element_type=F32)


def _dot_nt(a, b):
    return lax.dot_general(a, b, (((1,), (1,)), ((), ())), preferred_element_type=F32)


def _rms_mod(x, g, shift, scale):
    ms = jnp.mean(x * x, axis=-1, keepdims=True)
    h = x * lax.rsqrt(ms + EPS) * g
    return h * (1.0 + scale) + shift


def _gelu(z):
    return 0.5 * z * (1.0 + lax.erf(z * np.float32(np.sqrt(0.5))))


def _split_bf16(a):
    hi = a.astype(BF16)
    lo = (a - hi.astype(F32)).astype(BF16)
    return hi, lo


def _params(n_axes):
    return pltpu.CompilerParams(dimension_semantics=("arbitrary",) * n_axes,
                                vmem_limit_bytes=VMEM_LIMIT)


def _resident(shape, stack_index=None):
    nd = len(shape)
    if stack_index is None:
        return pl.BlockSpec(shape, lambda *_: (0,) * nd, pipeline_mode=pl.Buffered(1))
    return pl.BlockSpec((None,) + tuple(shape), lambda *_: (stack_index,) + (0,) * nd,
                        pipeline_mode=pl.Buffered(1))


def _adaln_kernel(c_ref, w_ref, b_ref, o_ref):
    c = c_ref[...]
    s = c * (1.0 / (1.0 + jnp.exp(-c)))
    s_hi, s_lo = _split_bf16(s)
    w_hi, w_lo = _split_bf16(w_ref[...])
    o_ref[...] = _dot(s_hi, w_hi) + (_dot(s_lo, w_hi) + _dot(s_hi, w_lo)) + b_ref[...]


def _adaln(cond, ada_w, ada_b):
    tn = D_MODEL
    return pl.pallas_call(
        _adaln_kernel,
        grid=(DEPTH, 6 * D_MODEL // tn),
        in_specs=[
            pl.BlockSpec((MOD_ROWS, D_MODEL), lambda i, n: (0, 0)),
            pl.BlockSpec((None, D_MODEL, tn), lambda i, n: (i, 0, n)),
            pl.BlockSpec((None, 1, tn), lambda i, n: (i, 0, n)),
        ],
        out_specs=pl.BlockSpec((None, MOD_ROWS, tn), lambda i, n: (i, 0, n)),
        out_shape=jax.ShapeDtypeStruct((DEPTH, MOD_ROWS, 6 * D_MODEL), F32),
        compiler_params=_params(2),
        name="adaln",
    )(cond, ada_w, ada_b.reshape(DEPTH, 1, 6 * D_MODEL))


def _mod_spec(layer, k, tiles_per_batch):
    base = layer * MOD_ROWS * 6 + k
    if tiles_per_batch is None:
        return pl.BlockSpec((None, 1, D_MODEL), lambda t: (base + CTX_MOD_ROW * 6, 0, 0))
    return pl.BlockSpec((None, 1, D_MODEL), lambda t: (base + (t // tiles_per_batch) * 6, 0, 0))


def _vec_spec(n):
    return pl.BlockSpec((1, n), lambda t: (0, 0))


def _key_col_start(j):
    return int(np.clip(j * NA_KW - NA_KW // 2, 0, GRID_W - NA_KB))


def _qkv_kernel(x_ref, g_ref, sh_ref, sc_ref, w_ref, qg_ref, kg_ref, e_ref, et_ref, *out_refs,
                need_q, gather):
    h = _rms_mod(x_ref[...], g_ref[...], sh_ref[...], sc_ref[...]).astype(BF16)

    def head_mean_square(y):
        return _dot((y * y).astype(BF16), e_ref[...])

    def head_norm(y, ms, gain):
        r_hi, r_lo = _split_bf16(lax.rsqrt(ms + EPS))
        rb = _dot(jnp.concatenate([r_hi, r_lo], axis=1), et_ref[...])
        return y * rb * gain

    refs = list(out_refs)
    q_pre = _dot(h, w_ref[:, 0:D_MODEL]) if need_q else None
    k_pre = _dot(h, w_ref[:, D_MODEL:2 * D_MODEL])
    v = _dot(h, w_ref[:, 2 * D_MODEL:3 * D_MODEL])
    q_ms = head_mean_square(q_pre) if need_q else None
    k_ms = head_mean_square(k_pre)
    if need_q:
        refs.pop(0)[...] = head_norm(q_pre, q_ms, qg_ref[...]).astype(BF16)
    k_ref, v_ref = refs
    k = head_norm(k_pre, k_ms, kg_ref[...])
    if not gather:
        k_ref[...] = k.astype(BF16)
        v_ref[...] = v.astype(BF16)
        return
    rows = k.shape[0] // GRID_W
    k3 = k.reshape(rows, GRID_W, D_MODEL)
    v3 = v.reshape(rows, GRID_W, D_MODEL)
    for j in range(NA_NCB):
        c0 = _key_col_start(j)
        k_ref[:, j, :, :] = k3[:, c0:c0 + NA_KB, :].astype(BF16)
        v_ref[:, j, :, :] = v3[:, c0:c0 + NA_KB, :].astype(BF16)


def _qkv(x, mods, layer, norm_g, w_qkv, mixer, q_gain, k_gain, e_mat, et_mat, *, latent, need_q, tm):
    n = x.shape[0]
    tiles_per_batch = (SEQ // tm) if latent else None
    tok = pl.BlockSpec((tm, D_MODEL), lambda t: (t, 0))
    out_specs, out_shapes = [], []
    if need_q:
        out_specs.append(tok)
        out_shapes.append(jax.ShapeDtypeStruct((n, D_MODEL), BF16))
    if latent:
        rows = tm // GRID_W
        kv_spec = pl.BlockSpec((rows, NA_NCB, NA_KB, D_MODEL), lambda t: (t, 0, 0, 0))
        kv_shape = jax.ShapeDtypeStruct((n // GRID_W, NA_NCB, NA_KB, D_MODEL), BF16)
    else:
        kv_spec, kv_shape = tok, jax.ShapeDtypeStruct((n, D_MODEL), BF16)
    out_specs += [kv_spec, kv_spec]
    out_shapes += [kv_shape, kv_shape]
    return pl.pallas_call(
        functools.partial(_qkv_kernel, need_q=need_q, gather=latent),
        grid=(n // tm,),
        in_specs=[
            tok,
            _vec_spec(D_MODEL),
            _mod_spec(layer, 0, tiles_per_batch),
            _mod_spec(layer, 1, tiles_per_batch),
            _resident((D_MODEL, 3 * D_MODEL), mixer),
            _vec_spec(D_MODEL),
            _vec_spec(D_MODEL),
            _resident((D_MODEL, LANES)),
            _resident((2 * LANES, D_MODEL)),
        ],
        out_specs=out_specs,
        out_shape=out_shapes,
        compiler_params=_params(1),
        name="qkv_lat" if latent else "qkv_ctx",
    )(x, norm_g, mods, mods, w_qkv, q_gain, k_gain, e_mat, et_mat)


def _pair_scores(q2, keys, bias_pair=None, shift=None):
    lane = lax.broadcasted_iota(jnp.int32, (1, LANES), 1)
    first = lane < NA_HEAD_DIM
    zero = jnp.zeros_like(q2)
    q4 = jnp.concatenate([jnp.where(first, q2, zero), jnp.where(first, zero, q2)], axis=0)
    s = _dot_nt(q4, keys)
    if bias_pair is None:
        return s
    nb = bias_pair[0].shape[1]
    rest = [s[:, i:i + LANES] - shift for i in range(nb, s.shape[1], LANES)]
    return jnp.concatenate([s[:, :nb] + jnp.concatenate(bias_pair, axis=0)] + rest, axis=1)


def _pair_softmax(s, row_max=True):
    tiles = [s[:, i:i + LANES] for i in range(0, s.shape[1], LANES)]
    if row_max:
        mx = jnp.max(functools.reduce(jnp.maximum, tiles), axis=-1, keepdims=True)
        tiles = [t - mx for t in tiles]
    probs = [jnp.exp2(t) for t in tiles]
    denom = jnp.sum(functools.reduce(lambda a, b: a + b, probs), axis=-1, keepdims=True)
    return jnp.concatenate([p.astype(BF16) for p in probs], axis=1), denom


def _pair_pv(p, denom, values):
    m_rows = p.shape[0] // 2
    o = _dot(p, values) / denom
    lane = lax.broadcasted_iota(jnp.int32, (1, LANES), 1)
    return jnp.where(lane < NA_HEAD_DIM, o[:m_rows], o[m_rows:])


def _attn_lat_kernel(q_ref, k0, k1, k2, k3, v0, v1, v2, v3, kc_ref, vc_ref, bias_ref, shift_ref,
                     o_ref, *, row_max):
    kv_rows = KEY_ROW_GROUP * NA_KB

    def lanes(hp):
        return slice(hp * LANES, (hp + 1) * LANES)

    def gather(refs, ctx_ref, t):
        bi, hp = divmod(t, HEAD_PAIRS)
        return jnp.concatenate(
            [r[bi, :, :, lanes(hp)].reshape(kv_rows, LANES) for r in refs] + [ctx_ref[bi, :, lanes(hp)]],
            axis=0)

    def scores(t):
        bi, hp = divmod(t, HEAD_PAIRS)
        q2 = q_ref[bi, :, :, lanes(hp)].reshape(Q_PATCH, LANES)
        return _pair_scores(q2, gather((k0, k1, k2, k3), kc_ref, t),
                            (bias_ref[2 * hp], bias_ref[2 * hp + 1]), shift_ref[...])

    n_stages = q_ref.shape[0] * HEAD_PAIRS
    s_next, prev = scores(0), None
    for t in range(n_stages + 1):
        s = s_next
        if t + 1 < n_stages:
            s_next = scores(t + 1)
        if prev is not None:
            bi, hp = divmod(t - 1, HEAD_PAIRS)
            o = _pair_pv(*prev, gather((v0, v1, v2, v3), vc_ref, t - 1))
            o_ref[bi, :, :, lanes(hp)] = o.reshape(Q_ROWS, NA_KW, LANES).astype(BF16)
        if t < n_stages:
            prev = _pair_softmax(s, row_max)


def _key_group_start(rb):
    return jnp.clip(2 * rb - 1, 0, GRID_ROWS // KEY_ROW_GROUP - KEY_GROUPS)


def _attn_lat(q, k, v, kc, vc, bias, attn_layer, logit_bound):
    n_rb = GRID_ROWS // Q_ROWS
    nb = ATTN_BATCHES
    q4 = q.reshape(BATCH, GRID_ROWS, GRID_W, D_MODEL)
    k5 = k.reshape(BATCH, GRID_ROWS, NA_NCB, NA_KB, D_MODEL)
    v5 = v.reshape(BATCH, GRID_ROWS, NA_NCB, NA_KB, D_MODEL)
    kc3 = kc.reshape(BATCH, CTX_LEN, D_MODEL)
    vc3 = vc.reshape(BATCH, CTX_LEN, D_MODEL)
    q_spec = pl.BlockSpec((nb, Q_ROWS, NA_KW, D_MODEL), lambda rb, j, b: (b, rb, j, 0))

    def kv_spec(i):
        return pl.BlockSpec(
            (nb, KEY_ROW_GROUP, None, NA_KB, D_MODEL),
            lambda rb, j, b: (b, _key_group_start(rb) + i, j, 0, 0))

    ctx_spec = pl.BlockSpec((nb, CTX_LEN, D_MODEL), lambda rb, j, b: (b, 0, 0))

    def bias_index(rb, j, b):
        variant = (rb > 0).astype(jnp.int32) + (rb == n_rb - 1).astype(jnp.int32)
        return (variant * NA_NCB + j, attn_layer, 0, 0)

    bias_spec = pl.BlockSpec((None, NA_HEADS, Q_PATCH, LAT_KEYS), bias_index)
    shift = jnp.full((1, LANES), logit_bound, F32)

    def run(row_max):
        return pl.pallas_call(
            functools.partial(_attn_lat_kernel, row_max=row_max),
            grid=(n_rb, NA_NCB, BATCH // nb),
            in_specs=([q_spec] + [kv_spec(i) for i in range(KEY_GROUPS)] * 2
                      + [ctx_spec, ctx_spec, bias_spec, pl.BlockSpec((1, LANES), lambda rb, j, b: (0, 0))]),
            out_specs=q_spec,
            out_shape=jax.ShapeDtypeStruct(q4.shape, BF16),
            compiler_params=_params(3),
            name="attn_lat_rowmax" if row_max else "attn_lat",
        )(q4, k5, k5, k5, k5, v5, v5, v5, v5, kc3, vc3, bias, shift)

    out = lax.cond(logit_bound <= EXP2_SAFE_BOUND, lambda: run(False), lambda: run(True))
    return out.reshape(BATCH * SEQ, D_MODEL)


def _attn_ctx_kernel(q_ref, k_ref, v_ref, o_ref):
    for hp in range(HEAD_PAIRS):
        sl = slice(hp * LANES, (hp + 1) * LANES)
        p, denom = _pair_softmax(_pair_scores(q_ref[:, sl], k_ref[:, sl], None))
        o_ref[:, sl] = _pair_pv(p, denom, v_ref[:, sl]).astype(BF16)


def _attn_ctx(qc, kc, vc):
    spec = pl.BlockSpec((CTX_LEN, D_MODEL), lambda b: (b, 0))
    return pl.pallas_call(
        _attn_ctx_kernel,
        grid=(BATCH,),
        in_specs=[spec, spec, spec],
        out_specs=spec,
        out_shape=jax.ShapeDtypeStruct(qc.shape, BF16),
        compiler_params=_params(1),
        name="attn_ctx",
    )(qc, kc, vc)


def _attention_bias_tables(rpb):
    n_rb = GRID_ROWS // Q_ROWS
    n_heads = rpb.shape[0]
    col_pad, row_pad = NA_KW, Q_ROWS
    rpb_p = jnp.pad(rpb, ((0, 0), (row_pad, row_pad), (col_pad, col_pad)))
    by_col = []
    for j in range(NA_NCB):
        col_off = _key_col_start(j) - j * NA_KW + NA_KW - 1 + col_pad
        by_col.append(jnp.stack([rpb_p[:, :, col_off - qi:col_off - qi + NA_KB]
                                 for qi in range(NA_KW)], axis=1).reshape(n_heads, NA_KW, -1))
    tables = []
    for rb in (0, 1, n_rb - 1):
        g0 = int(np.clip(2 * rb - 1, 0, GRID_ROWS // KEY_ROW_GROUP - KEY_GROUPS))
        row_off = g0 * KEY_ROW_GROUP - rb * Q_ROWS + NA_KH - 1 + row_pad
        for j in range(NA_NCB):
            by_row = jnp.stack(
                [by_col[j][:, :, (row_off - ql) * NA_KB:(row_off - ql) * NA_KB + LAT_KEYS]
                 for ql in range(Q_ROWS)], axis=1)
            by_row = by_row.reshape(n_heads, Q_PATCH, LAT_KEYS)
            qrow = (rb * Q_ROWS + np.arange(Q_ROWS))[:, None, None, None]
            qcol = (j * NA_KW + np.arange(NA_KW))[None, :, None, None]
            krow = (g0 * KEY_ROW_GROUP + np.arange(KEY_ROWS))[None, None, :, None]
            kcol = (_key_col_start(j) + np.arange(NA_KB))[None, None, None, :]
            rs = np.clip(qrow - NA_KH // 2, 0, GRID_ROWS - NA_KH)
            ws = np.clip(qcol - NA_KW // 2, 0, GRID_W - NA_KW)
            ok = (krow >= rs) & (krow < rs + NA_KH) & (kcol >= ws) & (kcol < ws + NA_KW)
            ok = np.broadcast_to(ok, (Q_ROWS, NA_KW, KEY_ROWS, NA_KB)).reshape(Q_PATCH, LAT_KEYS)
            tables.append(jnp.where(ok[None], by_row, NEG_INF))
    return jnp.stack(tables, axis=0)


def _mlp_kernel(*refs, with_proj):
    if with_proj:
        x_ref, a_ref, wo_ref, g1_ref, refs = refs[0], refs[1], refs[2], refs[3], refs[4:]
        x = x_ref[...] + g1_ref[...] * _dot(a_ref[...], wo_ref[...])
    else:
        x_ref, refs = refs[0], refs[1:]
        x = x_ref[...]
    n_ref, sh_ref, sc_ref, g2_ref, w1_ref, w2_ref, o_ref = refs
    h = _rms_mod(x, n_ref[...], sh_ref[...], sc_ref[...]).astype(BF16)
    t = jnp.maximum(_dot(h, w1_ref[...]), 0.0)
    t = (t * t).astype(BF16)
    o_ref[...] = x + g2_ref[...] * _dot(t, w2_ref[...])


def _mlp(x, mods, layer, norm_g, w1, w2, *, latent, tm, attn=None, w_o=None, mixer=None):
    n = x.shape[0]
    tiles_per_batch = (SEQ // tm) if latent else None
    tok = pl.BlockSpec((tm, D_MODEL), lambda t: (t, 0))
    with_proj = attn is not None
    in_specs, args = [tok], [x]
    if with_proj:
        in_specs += [tok, _resident((D_MODEL, D_MODEL), mixer), _mod_spec(layer, 2, tiles_per_batch)]
        args += [attn, w_o, mods]
    in_specs += [
        _vec_spec(D_MODEL),
        _mod_spec(layer, 3, tiles_per_batch),
        _mod_spec(layer, 4, tiles_per_batch),
        _mod_spec(layer, 5, tiles_per_batch),
        _resident((D_MODEL, MLP_HIDDEN), layer),
        _resident((MLP_HIDDEN, D_MODEL), layer),
    ]
    args += [norm_g, mods, mods, mods, w1, w2]
    return pl.pallas_call(
        functools.partial(_mlp_kernel, with_proj=with_proj),
        grid=(n // tm,),
        in_specs=in_specs,
        out_specs=tok,
        out_shape=jax.ShapeDtypeStruct((n, D_MODEL), F32),
        compiler_params=_params(1),
        name=("proj_mlp" if with_proj else "mlp") + ("_lat" if latent else "_ctx"),
    )(*args)


SG_COLS = 512


def _sg_kernel(x_ref, n_ref, sh_ref, sc_ref, g1_ref, win_ref, bin_ref, lng_ref, lnb_ref,
               ws_ref, bs_ref, wo_ref, o_ref, v_scr, t_scr):
    x = x_ref[...]
    tm = x.shape[0]
    n_chunks = tm // SG_CHUNK
    h = _rms_mod(x, n_ref[...], sh_ref[...], sc_ref[...]).astype(BF16)
    n_steps = SG_HALF // SG_COLS
    groups_per_step = SG_COLS // SG_GROUP_CH

    def cols(c):
        return slice(c * SG_COLS, (c + 1) * SG_COLS)

    def lane_tile_sum(a):
        return functools.reduce(lambda p, q: p + q,
                                [a[:, i:i + LANES] for i in range(0, a.shape[1], LANES)])

    s1 = s2 = None
    for c in range(n_steps):
        vc = slice(SG_HALF + c * SG_COLS, SG_HALF + (c + 1) * SG_COLS)
        v_blk = _gelu(_dot(h, win_ref[:, vc]) + bin_ref[:, vc])
        v_scr[:, cols(c)] = v_blk
        p1, p2 = lane_tile_sum(v_blk), lane_tile_sum(v_blk * v_blk)
        s1, s2 = (p1, p2) if s1 is None else (s1 + p1, s2 + p2)
    mu = jnp.sum(s1, axis=-1, keepdims=True) * (1.0 / SG_HALF)
    var = jnp.sum(s2, axis=-1, keepdims=True) * (1.0 / SG_HALF) - mu * mu
    rstd = lax.rsqrt(var + EPS)

    for c in range(n_steps):
        u = _gelu(_dot(h, win_ref[:, cols(c)]) + bin_ref[:, cols(c)])
        vn = ((v_scr[:, cols(c)] - mu) * rstd * lng_ref[:, cols(c)] + lnb_ref[:, cols(c)]).astype(BF16)
        for gg in range(groups_per_step):
            g = c * groups_per_step + gg
            gl = slice(gg * SG_GROUP_CH, (gg + 1) * SG_GROUP_CH)
            rhs = jnp.concatenate(
                [vn[i * SG_CHUNK:(i + 1) * SG_CHUNK, gl] for i in range(n_chunks)], axis=1)
            s = _dot(ws_ref[g], rhs)
            for i in range(n_chunks):
                rows = slice(i * SG_CHUNK, (i + 1) * SG_CHUNK)
                s_i = s[:, i * SG_GROUP_CH:(i + 1) * SG_GROUP_CH] + bs_ref[g]
                t_scr[rows, g * SG_GROUP_CH:(g + 1) * SG_GROUP_CH] = (u[rows, gl] * s_i).astype(BF16)
    o_ref[...] = x + g1_ref[...] * _dot(t_scr[...], wo_ref[...])


def _sg(x, mods, layer, norm_g, mixer, w_in, b_in, ln_g, ln_b, w_s, b_s, w_o, *, latent, tm):
    n = x.shape[0]
    tiles_per_batch = (SEQ // tm) if latent else None
    tok = pl.BlockSpec((tm, D_MODEL), lambda t: (t, 0))
    return pl.pallas_call(
        _sg_kernel,
        grid=(n // tm,),
        in_specs=[
            tok,
            _vec_spec(D_MODEL),
            _mod_spec(layer, 0, tiles_per_batch),
            _mod_spec(layer, 1, tiles_per_batch),
            _mod_spec(layer, 2, tiles_per_batch),
            _resident((D_MODEL, 2 * SG_HALF), mixer),
            _vec_spec(2 * SG_HALF),
            _vec_spec(SG_HALF),
            _vec_spec(SG_HALF),
            _resident((SG_GROUPS, SG_CHUNK, SG_CHUNK), mixer),
            _resident((SG_GROUPS, SG_CHUNK, SG_GROUP_CH)),
            _resident((SG_HALF, D_MODEL), mixer),
        ],
        out_specs=tok,
        out_shape=jax.ShapeDtypeStruct((n, D_MODEL), F32),
        scratch_shapes=[pltpu.VMEM((tm, SG_HALF), F32), pltpu.VMEM((tm, SG_HALF), BF16)],
        compiler_params=_params(1),
        name="sg_lat" if latent else "sg_ctx",
    )(x, norm_g, mods, mods, mods, w_in, b_in, ln_g, ln_b, w_s, b_s, w_o)


LAT_TM = 512
CTX_TM = 256


def kernel(x, c, ctx, c_ctx, ada_w, ada_b, norm1_g, norm2_g, mlp_w1, mlp_w2,
           na_w_qkv, na_q_norm, na_k_norm, na_rpb, na_w_o,
           sg_w_in, sg_b_in, sg_ln_g, sg_ln_b, sg_w_s, sg_b_s, sg_w_o):
    last_ctx_layer = ((DEPTH - 1) // N_MIXERS) * N_MIXERS
    xl = x.reshape(BATCH * SEQ, D_MODEL)
    xc = ctx.reshape(BATCH * CTX_LEN, D_MODEL)

    cond = jnp.concatenate(
        [c, c_ctx[None, :], jnp.zeros((MOD_ROWS - BATCH - 1, D_MODEL), F32)], axis=0)
    mods = _adaln(cond, ada_w, ada_b).reshape(DEPTH * MOD_ROWS * 6, 1, D_MODEL)

    head_of = np.arange(D_MODEL) // NA_HEAD_DIM
    e_np = (head_of[:, None] == np.arange(LANES)[None, :]).astype(np.float32)
    e_mat = jnp.asarray(e_np / NA_HEAD_DIM, BF16)
    et_mat = jnp.asarray(np.concatenate([e_np.T, e_np.T], axis=0), BF16)
    q_gains = na_q_norm * (NA_HEAD_DIM ** -0.5 * LOG2_E)
    rpb2 = na_rpb * LOG2_E
    logit_bounds = (NA_HEAD_DIM * 1.01 * jnp.max(jnp.abs(q_gains), axis=1)
                    * jnp.max(jnp.abs(na_k_norm), axis=1) + jnp.max(jnp.abs(rpb2), axis=(1, 2, 3)))
    bias = _attention_bias_tables(
        (rpb2 - logit_bounds[:, None, None, None]).reshape((-1,) + na_rpb.shape[2:]))

    w1, w2 = mlp_w1.astype(BF16), mlp_w2.astype(BF16)
    w_qkv, w_ao = na_w_qkv.astype(BF16), na_w_o.astype(BF16)
    w_in, w_s, w_so = sg_w_in.astype(BF16), sg_w_s.astype(BF16), sg_w_o.astype(BF16)

    for i in range(DEPTH):
        ctx_full = i < last_ctx_layer
        n1 = norm1_g[i][None, :]
        n2 = norm2_g[i][None, :]
        if i % N_MIXERS == 0:
            a = i // N_MIXERS
            q_gain = jnp.tile(q_gains[a], NA_HEADS)[None, :]
            k_gain = jnp.tile(na_k_norm[a], NA_HEADS)[None, :]
            q, k, v = _qkv(xl, mods, i, n1, w_qkv, a, q_gain, k_gain, e_mat, et_mat,
                           latent=True, need_q=True, tm=LAT_TM)
            ctx_proj = _qkv(xc, mods, i, n1, w_qkv, a, q_gain, k_gain, e_mat, et_mat,
                            latent=False, need_q=ctx_full, tm=CTX_TM)
            kc, vc = ctx_proj[-2], ctx_proj[-1]
            att = _attn_lat(q, k, v, kc, vc, bias, a, logit_bounds[a])
            xl = _mlp(xl, mods, i, n2, w1, w2, latent=True, tm=LAT_TM, attn=att, w_o=w_ao, mixer=a)
            if ctx_full:
                att_c = _attn_ctx(ctx_proj[0], kc, vc)
                xc = _mlp(xc, mods, i, n2, w1, w2, latent=False, tm=CTX_TM, attn=att_c, w_o=w_ao,
                          mixer=a)
        else:
            s = i // N_MIXERS
            sg_args = (s, w_in, sg_b_in[s][None, :], sg_ln_g[s][None, :], sg_ln_b[s][None, :], w_s,
                       jnp.broadcast_to(sg_b_s[s][:, :, None], (SG_GROUPS, SG_CHUNK, SG_GROUP_CH)),
                       w_so)
            xl = _sg(xl, mods, i, n1, *sg_args, latent=True, tm=LAT_TM)
            xl = _mlp(xl, mods, i, n2, w1, w2, latent=True, tm=LAT_TM)
            if ctx_full:
                xc = _sg(xc, mods, i, n1, *sg_args, latent=False, tm=CTX_TM)
                xc = _mlp(xc, mods, i, n2, w1, w2, latent=False, tm=CTX_TM)
    return xl.reshape(BATCH, SEQ, D_MODEL)
```

```python
import functools

import numpy as np
import jax
import jax.numpy as jnp
from jax import lax
from jax.experimental import pallas as pl
from jax.experimental.pallas import tpu as pltpu

D_MODEL = 1024
BATCH = 8
SEQ = 4096
DEPTH = 4
GRID_W = 64
GRID_ROWS = SEQ // GRID_W
CTX_LEN = 256
N_MIXERS = 2
NA_HEADS = 16
NA_HEAD_DIM = D_MODEL // NA_HEADS
NA_KH = 8
NA_KW = 16
NA_KB = 2 * NA_KW
NA_NCB = GRID_W // NA_KW
SG_CHUNK = 128
SG_HALF = 3 * D_MODEL
SG_GROUP_CH = 128
SG_GROUPS = SG_HALF // SG_GROUP_CH
MLP_HIDDEN = 4 * D_MODEL
EPS = 1e-6
NEG_INF = -1e30
LOG2_E = float(np.log2(np.e))
EXP2_SAFE_BOUND = 60.0

F32 = jnp.float32
BF16 = jnp.bfloat16

LANES = 128
HEAD_PAIRS = NA_HEADS // 2
MOD_ROWS = 16
CTX_MOD_ROW = BATCH
Q_ROWS = 8
Q_PATCH = Q_ROWS * NA_KW
KEY_ROW_GROUP = 4
KEY_GROUPS = 4
KEY_ROWS = KEY_ROW_GROUP * KEY_GROUPS
LAT_KEYS = KEY_ROWS * NA_KB
ATTN_BATCHES = 4
VMEM_LIMIT = 56 * 1024 * 1024


def _dot(a, b):
    return jnp.dot(a, b, preferred_element_type=F32)


def _dot_nt(a, b):
    return lax.dot_general(a, b, (((1,), (1,)), ((), ())), preferred_element_type=F32)


def _rms_mod(x, g, shift, scale):
    ms = jnp.mean(x * x, axis=-1, keepdims=True)
    h = x * lax.rsqrt(ms + EPS) * g
    return h * (1.0 + scale) + shift


def _gelu(z):
    return 0.5 * z * (1.0 + lax.erf(z * np.float32(np.sqrt(0.5))))


def _split_bf16(a):
    hi = a.astype(BF16)
    lo = (a - hi.astype(F32)).astype(BF16)
    return hi, lo


def _params(n_axes):
    return pltpu.CompilerParams(dimension_semantics=("arbitrary",) * n_axes,
                                vmem_limit_bytes=VMEM_LIMIT)


def _resident(shape, stack_index=None):
    nd = len(shape)
    if stack_index is None:
        return pl.BlockSpec(shape, lambda *_: (0,) * nd, pipeline_mode=pl.Buffered(1))
    return pl.BlockSpec((None,) + tuple(shape), lambda *_: (stack_index,) + (0,) * nd,
                        pipeline_mode=pl.Buffered(1))


def _adaln_kernel(c_ref, w_ref, b_ref, o_ref):
    c = c_ref[...]
    s = c * (1.0 / (1.0 + jnp.exp(-c)))
    s_hi, s_lo = _split_bf16(s)
    w_hi, w_lo = _split_bf16(w_ref[...])
    o_ref[...] = _dot(s_hi, w_hi) + (_dot(s_lo, w_hi) + _dot(s_hi, w_lo)) + b_ref[...]


def _adaln(cond, ada_w, ada_b):
    tn = D_MODEL
    return pl.pallas_call(
        _adaln_kernel,
        grid=(DEPTH, 6 * D_MODEL // tn),
        in_specs=[
            pl.BlockSpec((MOD_ROWS, D_MODEL), lambda i, n: (0, 0)),
            pl.BlockSpec((None, D_MODEL, tn), lambda i, n: (i, 0, n)),
            pl.BlockSpec((None, 1, tn), lambda i, n: (i, 0, n)),
        ],
        out_specs=pl.BlockSpec((None, MOD_ROWS, tn), lambda i, n: (i, 0, n)),
        out_shape=jax.ShapeDtypeStruct((DEPTH, MOD_ROWS, 6 * D_MODEL), F32),
        compiler_params=_params(2),
        name="adaln",
    )(cond, ada_w, ada_b.reshape(DEPTH, 1, 6 * D_MODEL))


def _mod_spec(layer, k, tiles_per_batch):
    base = layer * MOD_ROWS * 6 + k
    if tiles_per_batch is None:
        return pl.BlockSpec((None, 1, D_MODEL), lambda t: (base + CTX_MOD_ROW * 6, 0, 0))
    return pl.BlockSpec((None, 1, D_MODEL), lambda t: (base + (t // tiles_per_batch) * 6, 0, 0))


def _vec_spec(n):
    return pl.BlockSpec((1, n), lambda t: (0, 0))


def _key_col_start(j):
    return int(np.clip(j * NA_KW - NA_KW // 2, 0, GRID_W - NA_KB))


def _qkv_kernel(x_ref, g_ref, sh_ref, sc_ref, w_ref, qg_ref, kg_ref, *out_refs, need_q, gather):
    h = _rms_mod(x_ref[...], g_ref[...], sh_ref[...], sc_ref[...]).astype(BF16)
    first = lax.broadcasted_iota(jnp.int32, (1, LANES), 1) < NA_HEAD_DIM

    def head_norm(y, gain):
        parts = []
        for c in range(0, D_MODEL, LANES):
            t = y[:, c:c + LANES]
            sq = t * t
            s_first = jnp.sum(jnp.where(first, sq, 0.0), axis=-1, keepdims=True)
            s_both = jnp.sum(sq, axis=-1, keepdims=True)
            ms = jnp.where(first, s_first, s_both - s_first) * (1.0 / NA_HEAD_DIM)
            parts.append(t * lax.rsqrt(ms + EPS))
        return jnp.concatenate(parts, axis=1) * gain

    refs = list(out_refs)
    if need_q:
        refs.pop(0)[...] = head_norm(_dot(h, w_ref[:, 0:D_MODEL]), qg_ref[...]).astype(BF16)
    k_ref, v_ref = refs
    k = head_norm(_dot(h, w_ref[:, D_MODEL:2 * D_MODEL]), kg_ref[...])
    v = _dot(h, w_ref[:, 2 * D_MODEL:3 * D_MODEL])
    if not gather:
        k_ref[...] = k.astype(BF16)
        v_ref[...] = v.astype(BF16)
        return
    rows = k.shape[0] // GRID_W
    k3 = k.reshape(rows, GRID_W, D_MODEL)
    v3 = v.reshape(rows, GRID_W, D_MODEL)
    for j in range(NA_NCB):
        c0 = _key_col_start(j)
        k_ref[:, j, :, :] = k3[:, c0:c0 + NA_KB, :].astype(BF16)
        v_ref[:, j, :, :] = v3[:, c0:c0 + NA_KB, :].astype(BF16)


def _qkv(x, mods, layer, norm_g, w_qkv, mixer, q_gain, k_gain, *, latent, need_q, tm):
    n = x.shape[0]
    tiles_per_batch = (SEQ // tm) if latent else None
    tok = pl.BlockSpec((tm, D_MODEL), lambda t: (t, 0))
    out_specs, out_shapes = [], []
    if need_q:
        out_specs.append(tok)
        out_shapes.append(jax.ShapeDtypeStruct((n, D_MODEL), BF16))
    if latent:
        rows = tm // GRID_W
        kv_spec = pl.BlockSpec((rows, NA_NCB, NA_KB, D_MODEL), lambda t: (t, 0, 0, 0))
        kv_shape = jax.ShapeDtypeStruct((n // GRID_W, NA_NCB, NA_KB, D_MODEL), BF16)
    else:
        kv_spec, kv_shape = tok, jax.ShapeDtypeStruct((n, D_MODEL), BF16)
    out_specs += [kv_spec, kv_spec]
    out_shapes += [kv_shape, kv_shape]
    return pl.pallas_call(
        functools.partial(_qkv_kernel, need_q=need_q, gather=latent),
        grid=(n // tm,),
        in_specs=[
            tok,
            _vec_spec(D_MODEL),
            _mod_spec(layer, 0, tiles_per_batch),
            _mod_spec(layer, 1, tiles_per_batch),
            _resident((D_MODEL, 3 * D_MODEL), mixer),
            _vec_spec(D_MODEL),
            _vec_spec(D_MODEL),
        ],
        out_specs=out_specs,
        out_shape=out_shapes,
        compiler_params=_params(1),
        name="qkv_lat" if latent else "qkv_ctx",
    )(x, norm_g, mods, mods, w_qkv, q_gain, k_gain)


def _pair_scores(q2, keys, bias_pair=None, shift=None):
    lane = lax.broadcasted_iota(jnp.int32, (1, LANES), 1)
    first = lane < NA_HEAD_DIM
    zero = jnp.zeros_like(q2)
    q4 = jnp.concatenate([jnp.where(first, q2, zero), jnp.where(first, zero, q2)], axis=0)
    s = _dot_nt(q4, keys)
    if bias_pair is None:
        return s
    nb = bias_pair[0].shape[1]
    rest = [s[:, i:i + LANES] - shift for i in range(nb, s.shape[1], LANES)]
    return jnp.concatenate([s[:, :nb] + jnp.concatenate(bias_pair, axis=0)] + rest, axis=1)


def _pair_softmax(s, row_max=True):
    tiles = [s[:, i:i + LANES] for i in range(0, s.shape[1], LANES)]
    if row_max:
        mx = jnp.max(functools.reduce(jnp.maximum, tiles), axis=-1, keepdims=True)
        tiles = [t - mx for t in tiles]
    probs = [jnp.exp2(t) for t in tiles]
    denom = jnp.sum(functools.reduce(lambda a, b: a + b, probs), axis=-1, keepdims=True)
    return jnp.concatenate([p.astype(BF16) for p in probs], axis=1), denom


def _pair_pv(p, denom, values):
    m_rows = p.shape[0] // 2
    o = _dot(p, values) / denom
    lane = lax.broadcasted_iota(jnp.int32, (1, LANES), 1)
    return jnp.where(lane < NA_HEAD_DIM, o[:m_rows], o[m_rows:])


def _attn_lat_kernel(q_ref, k0, k1, k2, k3, v0, v1, v2, v3, kc_ref, vc_ref, bias_ref, shift_ref,
                     o_ref, *, row_max):
    kv_rows = KEY_ROW_GROUP * NA_KB

    def lanes(hp):
        return slice(hp * LANES, (hp + 1) * LANES)

    def gather(refs, ctx_ref, t):
        bi, hp = divmod(t, HEAD_PAIRS)
        return jnp.concatenate(
            [r[bi, :, :, lanes(hp)].reshape(kv_rows, LANES) for r in refs] + [ctx_ref[bi, :, lanes(hp)]],
            axis=0)

    def scores(t):
        bi, hp = divmod(t, HEAD_PAIRS)
        q2 = q_ref[bi, :, :, lanes(hp)].reshape(Q_PATCH, LANES)
        return _pair_scores(q2, gather((k0, k1, k2, k3), kc_ref, t),
                            (bias_ref[2 * hp], bias_ref[2 * hp + 1]), shift_ref[...])

    n_stages = q_ref.shape[0] * HEAD_PAIRS
    s_next, prev = scores(0), None
    for t in range(n_stages + 1):
        s = s_next
        if t + 1 < n_stages:
            s_next = scores(t + 1)
        if prev is not None:
            bi, hp = divmod(t - 1, HEAD_PAIRS)
            o = _pair_pv(*prev, gather((v0, v1, v2, v3), vc_ref, t - 1))
            o_ref[bi, :, :, lanes(hp)] = o.reshape(Q_ROWS, NA_KW, LANES).astype(BF16)
        if t < n_stages:
            prev = _pair_softmax(s, row_max)


def _key_group_start(rb):
    return jnp.clip(2 * rb - 1, 0, GRID_ROWS // KEY_ROW_GROUP - KEY_GROUPS)


def _attn_lat(q, k, v, kc, vc, bias, attn_layer, logit_bound):
    n_rb = GRID_ROWS // Q_ROWS
    nb = ATTN_BATCHES
    q4 = q.reshape(BATCH, GRID_ROWS, GRID_W, D_MODEL)
    k5 = k.reshape(BATCH, GRID_ROWS, NA_NCB, NA_KB, D_MODEL)
    v5 = v.reshape(BATCH, GRID_ROWS, NA_NCB, NA_KB, D_MODEL)
    kc3 = kc.reshape(BATCH, CTX_LEN, D_MODEL)
    vc3 = vc.reshape(BATCH, CTX_LEN, D_MODEL)
    q_spec = pl.BlockSpec((nb, Q_ROWS, NA_KW, D_MODEL), lambda rb, j, b: (b, rb, j, 0))

    def kv_spec(i):
        return pl.BlockSpec(
            (nb, KEY_ROW_GROUP, None, NA_KB, D_MODEL),
            lambda rb, j, b: (b, _key_group_start(rb) + i, j, 0, 0))

    ctx_spec = pl.BlockSpec((nb, CTX_LEN, D_MODEL), lambda rb, j, b: (b, 0, 0))

    def bias_index(rb, j, b):
        variant = (rb > 0).astype(jnp.int32) + (rb == n_rb - 1).astype(jnp.int32)
        return (variant * NA_NCB + j, attn_layer, 0, 0)

    bias_spec = pl.BlockSpec((None, NA_HEADS, Q_PATCH, LAT_KEYS), bias_index)
    shift = jnp.full((1, LANES), logit_bound, F32)

    def run(row_max):
        return pl.pallas_call(
            functools.partial(_attn_lat_kernel, row_max=row_max),
            grid=(n_rb, NA_NCB, BATCH // nb),
            in_specs=([q_spec] + [kv_spec(i) for i in range(KEY_GROUPS)] * 2
                      + [ctx_spec, ctx_spec, bias_spec, pl.BlockSpec((1, LANES), lambda rb, j, b: (0, 0))]),
            out_specs=q_spec,
            out_shape=jax.ShapeDtypeStruct(q4.shape, BF16),
            compiler_params=_params(3),
            name="attn_lat_rowmax" if row_max else "attn_lat",
        )(q4, k5, k5, k5, k5, v5, v5, v5, v5, kc3, vc3, bias, shift)

    out = lax.cond(logit_bound <= EXP2_SAFE_BOUND, lambda: run(False), lambda: run(True))
    return out.reshape(BATCH * SEQ, D_MODEL)


def _attn_ctx_kernel(q_ref, k_ref, v_ref, o_ref):
    for hp in range(HEAD_PAIRS):
        sl = slice(hp * LANES, (hp + 1) * LANES)
        p, denom = _pair_softmax(_pair_scores(q_ref[:, sl], k_ref[:, sl], None))
        o_ref[:, sl] = _pair_pv(p, denom, v_ref[:, sl]).astype(BF16)


def _attn_ctx(qc, kc, vc):
    spec = pl.BlockSpec((CTX_LEN, D_MODEL), lambda b: (b, 0))
    return pl.pallas_call(
        _attn_ctx_kernel,
        grid=(BATCH,),
        in_specs=[spec, spec, spec],
        out_specs=spec,
        out_shape=jax.ShapeDtypeStruct(qc.shape, BF16),
        compiler_params=_params(1),
        name="attn_ctx",
    )(qc, kc, vc)


def _attention_bias_tables(rpb):
    n_rb = GRID_ROWS // Q_ROWS
    n_heads = rpb.shape[0]
    col_pad, row_pad = NA_KW, Q_ROWS
    rpb_p = jnp.pad(rpb, ((0, 0), (row_pad, row_pad), (col_pad, col_pad)))
    by_col = []
    for j in range(NA_NCB):
        col_off = _key_col_start(j) - j * NA_KW + NA_KW - 1 + col_pad
        by_col.append(jnp.stack([rpb_p[:, :, col_off - qi:col_off - qi + NA_KB]
                                 for qi in range(NA_KW)], axis=1).reshape(n_heads, NA_KW, -1))
    tables = []
    for rb in (0, 1, n_rb - 1):
        g0 = int(np.clip(2 * rb - 1, 0, GRID_ROWS // KEY_ROW_GROUP - KEY_GROUPS))
        row_off = g0 * KEY_ROW_GROUP - rb * Q_ROWS + NA_KH - 1 + row_pad
        for j in range(NA_NCB):
            by_row = jnp.stack(
                [by_col[j][:, :, (row_off - ql) * NA_KB:(row_off - ql) * NA_KB + LAT_KEYS]
                 for ql in range(Q_ROWS)], axis=1)
            by_row = by_row.reshape(n_heads, Q_PATCH, LAT_KEYS)
            qrow = (rb * Q_ROWS + np.arange(Q_ROWS))[:, None, None, None]
            qcol = (j * NA_KW + np.arange(NA_KW))[None, :, None, None]
            krow = (g0 * KEY_ROW_GROUP + np.arange(KEY_ROWS))[None, None, :, None]
            kcol = (_key_col_start(j) + np.arange(NA_KB))[None, None, None, :]
            rs = np.clip(qrow - NA_KH // 2, 0, GRID_ROWS - NA_KH)
            ws = np.clip(qcol - NA_KW // 2, 0, GRID_W - NA_KW)
            ok = (krow >= rs) & (krow < rs + NA_KH) & (kcol >= ws) & (kcol < ws + NA_KW)
            ok = np.broadcast_to(ok, (Q_ROWS, NA_KW, KEY_ROWS, NA_KB)).reshape(Q_PATCH, LAT_KEYS)
            tables.append(jnp.where(ok[None], by_row, NEG_INF))
    return jnp.stack(tables, axis=0)


def _mlp_kernel(*refs, with_proj):
    if with_proj:
        x_ref, a_ref, wo_ref, g1_ref, refs = refs[0], refs[1], refs[2], refs[3], refs[4:]
        x = x_ref[...] + g1_ref[...] * _dot(a_ref[...], wo_ref[...])
    else:
        x_ref, refs = refs[0], refs[1:]
        x = x_ref[...]
    n_ref, sh_ref, sc_ref, g2_ref, w1_ref, w2_ref, o_ref = refs
    h = _rms_mod(x, n_ref[...], sh_ref[...], sc_ref[...]).astype(BF16)
    t = jnp.maximum(_dot(h, w1_ref[...]), 0.0)
    t = (t * t).astype(BF16)
    o_ref[...] = x + g2_ref[...] * _dot(t, w2_ref[...])


def _mlp(x, mods, layer, norm_g, w1, w2, *, latent, tm, attn=None, w_o=None, mixer=None):
    n = x.shape[0]
    tiles_per_batch = (SEQ // tm) if latent else None
    tok = pl.BlockSpec((tm, D_MODEL), lambda t: (t, 0))
    with_proj = attn is not None
    in_specs, args = [tok], [x]
    if with_proj:
        in_specs += [tok, _resident((D_MODEL, D_MODEL), mixer), _mod_spec(layer, 2, tiles_per_batch)]
        args += [attn, w_o, mods]
    in_specs += [
        _vec_spec(D_MODEL),
        _mod_spec(layer, 3, tiles_per_batch),
        _mod_spec(layer, 4, tiles_per_batch),
        _mod_spec(layer, 5, tiles_per_batch),
        _resident((D_MODEL, MLP_HIDDEN), layer),
        _resident((MLP_HIDDEN, D_MODEL), layer),
    ]
    args += [norm_g, mods, mods, mods, w1, w2]
    return pl.pallas_call(
        functools.partial(_mlp_kernel, with_proj=with_proj),
        grid=(n // tm,),
        in_specs=in_specs,
        out_specs=tok,
        out_shape=jax.ShapeDtypeStruct((n, D_MODEL), F32),
        compiler_params=_params(1),
        name=("proj_mlp" if with_proj else "mlp") + ("_lat" if latent else "_ctx"),
    )(*args)


SG_COLS = 1024


def _sg_kernel(x_ref, n_ref, sh_ref, sc_ref, g1_ref, win_ref, bin_ref, lng_ref, lnb_ref,
               ws_ref, bs_ref, wo_ref, o_ref, v_scr, t_scr):
    x = x_ref[...]
    tm = x.shape[0]
    n_chunks = tm // SG_CHUNK
    h = _rms_mod(x, n_ref[...], sh_ref[...], sc_ref[...]).astype(BF16)
    n_steps = SG_HALF // SG_COLS
    groups_per_step = SG_COLS // SG_GROUP_CH

    def cols(c):
        return slice(c * SG_COLS, (c + 1) * SG_COLS)

    def lane_tile_sum(a):
        return functools.reduce(lambda p, q: p + q,
                                [a[:, i:i + LANES] for i in range(0, a.shape[1], LANES)])

    s1 = s2 = None
    for c in range(n_steps):
        vc = slice(SG_HALF + c * SG_COLS, SG_HALF + (c + 1) * SG_COLS)
        v_blk = _gelu(_dot(h, win_ref[:, vc]) + bin_ref[:, vc])
        v_scr[:, cols(c)] = v_blk
        p1, p2 = lane_tile_sum(v_blk), lane_tile_sum(v_blk * v_blk)
        s1, s2 = (p1, p2) if s1 is None else (s1 + p1, s2 + p2)
    mu = jnp.sum(s1, axis=-1, keepdims=True) * (1.0 / SG_HALF)
    var = jnp.sum(s2, axis=-1, keepdims=True) * (1.0 / SG_HALF) - mu * mu
    rstd = lax.rsqrt(var + EPS)

    for c in range(n_steps):
        u = _gelu(_dot(h, win_ref[:, cols(c)]) + bin_ref[:, cols(c)])
        vn = ((v_scr[:, cols(c)] - mu) * rstd * lng_ref[:, cols(c)] + lnb_ref[:, cols(c)]).astype(BF16)
        for gg in range(groups_per_step):
            g = c * groups_per_step + gg
            gl = slice(gg * SG_GROUP_CH, (gg + 1) * SG_GROUP_CH)
            rhs = jnp.concatenate(
                [vn[i * SG_CHUNK:(i + 1) * SG_CHUNK, gl] for i in range(n_chunks)], axis=1)
            s = _dot(ws_ref[g], rhs)
            for i in range(n_chunks):
                rows = slice(i * SG_CHUNK, (i + 1) * SG_CHUNK)
                s_i = s[:, i * SG_GROUP_CH:(i + 1) * SG_GROUP_CH] + bs_ref[g]
                t_scr[rows, g * SG_GROUP_CH:(g + 1) * SG_GROUP_CH] = (u[rows, gl] * s_i).astype(BF16)
    o_ref[...] = x + g1_ref[...] * _dot(t_scr[...], wo_ref[...])


def _sg(x, mods, layer, norm_g, mixer, w_in, b_in, ln_g, ln_b, w_s, b_s, w_o, *, latent, tm):
    n = x.shape[0]
    tiles_per_batch = (SEQ // tm) if latent else None
    tok = pl.BlockSpec((tm, D_MODEL), lambda t: (t, 0))
    return pl.pallas_call(
        _sg_kernel,
        grid=(n // tm,),
        in_specs=[
            tok,
            _vec_spec(D_MODEL),
            _mod_spec(layer, 0, tiles_per_batch),
            _mod_spec(layer, 1, tiles_per_batch),
            _mod_spec(layer, 2, tiles_per_batch),
            _resident((D_MODEL, 2 * SG_HALF), mixer),
            _vec_spec(2 * SG_HALF),
            _vec_spec(SG_HALF),
            _vec_spec(SG_HALF),
            _resident((SG_GROUPS, SG_CHUNK, SG_CHUNK), mixer),
            _resident((SG_GROUPS, SG_CHUNK, SG_GROUP_CH)),
            _resident((SG_HALF, D_MODEL), mixer),
        ],
        out_specs=tok,
        out_shape=jax.ShapeDtypeStruct((n, D_MODEL), F32),
        scratch_shapes=[pltpu.VMEM((tm, SG_HALF), F32), pltpu.VMEM((tm, SG_HALF), BF16)],
        compiler_params=_params(1),
        name="sg_lat" if latent else "sg_ctx",
    )(x, norm_g, mods, mods, mods, w_in, b_in, ln_g, ln_b, w_s, b_s, w_o)


LAT_TM = 512
CTX_TM = 256


def kernel(x, c, ctx, c_ctx, ada_w, ada_b, norm1_g, norm2_g, mlp_w1, mlp_w2,
           na_w_qkv, na_q_norm, na_k_norm, na_rpb, na_w_o,
           sg_w_in, sg_b_in, sg_ln_g, sg_ln_b, sg_w_s, sg_b_s, sg_w_o):
    last_ctx_layer = ((DEPTH - 1) // N_MIXERS) * N_MIXERS
    xl = x.reshape(BATCH * SEQ, D_MODEL)
    xc = ctx.reshape(BATCH * CTX_LEN, D_MODEL)

    cond = jnp.concatenate(
        [c, c_ctx[None, :], jnp.zeros((MOD_ROWS - BATCH - 1, D_MODEL), F32)], axis=0)
    mods = _adaln(cond, ada_w, ada_b).reshape(DEPTH * MOD_ROWS * 6, 1, D_MODEL)

    q_gains = na_q_norm * (NA_HEAD_DIM ** -0.5 * LOG2_E)
    rpb2 = na_rpb * LOG2_E
    logit_bounds = (NA_HEAD_DIM * 1.01 * jnp.max(jnp.abs(q_gains), axis=1)
                    * jnp.max(jnp.abs(na_k_norm), axis=1) + jnp.max(jnp.abs(rpb2), axis=(1, 2, 3)))
    bias = _attention_bias_tables(
        (rpb2 - logit_bounds[:, None, None, None]).reshape((-1,) + na_rpb.shape[2:]))

    w1, w2 = mlp_w1.astype(BF16), mlp_w2.astype(BF16)
    w_qkv, w_ao = na_w_qkv.astype(BF16), na_w_o.astype(BF16)
    w_in, w_s, w_so = sg_w_in.astype(BF16), sg_w_s.astype(BF16), sg_w_o.astype(BF16)

    for i in range(DEPTH):
        ctx_full = i < last_ctx_layer
        n1 = norm1_g[i][None, :]
        n2 = norm2_g[i][None, :]
        if i % N_MIXERS == 0:
            a = i // N_MIXERS
            q_gain = jnp.tile(q_gains[a], NA_HEADS)[None, :]
            k_gain = jnp.tile(na_k_norm[a], NA_HEADS)[None, :]
            q, k, v = _qkv(xl, mods, i, n1, w_qkv, a, q_gain, k_gain,
                           latent=True, need_q=True, tm=LAT_TM)
            ctx_proj = _qkv(xc, mods, i, n1, w_qkv, a, q_gain, k_gain,
                            latent=False, need_q=ctx_full, tm=CTX_TM)
            kc, vc = ctx_proj[-2], ctx_proj[-1]
            att = _attn_lat(q, k, v, kc, vc, bias, a, logit_bounds[a])
            xl = _mlp(xl, mods, i, n2, w1, w2, latent=True, tm=LAT_TM, attn=att, w_o=w_ao, mixer=a)
            if ctx_full:
                att_c = _attn_ctx(ctx_proj[0], kc, vc)
                xc = _mlp(xc, mods, i, n2, w1, w2, latent=False, tm=CTX_TM, attn=att_c, w_o=w_ao,
                          mixer=a)
        else:
            s = i // N_MIXERS
            sg_args = (s, w_in, sg_b_in[s][None, :], sg_ln_g[s][None, :], sg_ln_b[s][None, :], w_s,
                       jnp.broadcast_to(sg_b_s[s][:, :, None], (SG_GROUPS, SG_CHUNK, SG_GROUP_CH)),
                       w_so)
            xl = _sg(xl, mods, i, n1, *sg_args, latent=True, tm=LAT_TM)
            xl = _mlp(xl, mods, i, n2, w1, w2, latent=True, tm=LAT_TM)
            if ctx_full:
                xc = _sg(xc, mods, i, n1, *sg_args, latent=False, tm=CTX_TM)
                xc = _mlp(xc, mods, i, n2, w1, w2, latent=False, tm=CTX_TM)
    return xl.reshape(BATCH, SEQ, D_MODEL)
```

```python
import functools

import numpy as np
import jax
import jax.numpy as jnp
from jax import lax
from jax.experimental import pallas as pl
from jax.experimental.pallas import tpu as pltpu

D_MODEL = 1024
BATCH = 8
SEQ = 4096
DEPTH = 4
GRID_W = 64
GRID_ROWS = SEQ // GRID_W
CTX_LEN = 256
N_MIXERS = 2
NA_HEADS = 16
NA_HEAD_DIM = D_MODEL // NA_HEADS
NA_KH = 8
NA_KW = 16
NA_KB = 2 * NA_KW
NA_NCB = GRID_W // NA_KW
SG_CHUNK = 128
SG_HALF = 3 * D_MODEL
SG_GROUP_CH = 128
SG_GROUPS = SG_HALF // SG_GROUP_CH
MLP_HIDDEN = 4 * D_MODEL
EPS = 1e-6
NEG_INF = -1e30
LOG2_E = float(np.log2(np.e))
EXP2_SAFE_BOUND = 60.0

F32 = jnp.float32
BF16 = jnp.bfloat16

LANES = 128
HEAD_PAIRS = NA_HEADS // 2
MOD_ROWS = 16
CTX_MOD_ROW = BATCH
Q_ROWS = 8
Q_PATCH = Q_ROWS * NA_KW
KEY_ROW_GROUP = 4
KEY_GROUPS = 4
KEY_ROWS = KEY_ROW_GROUP * KEY_GROUPS
LAT_KEYS = KEY_ROWS * NA_KB
ATTN_BATCHES = 4
VMEM_LIMIT = 56 * 1024 * 1024


def _dot(a, b):
    return jnp.dot(a, b, preferred_element_type=F32)


def _dot_nt(a, b):
    return lax.dot_general(a, b, (((1,), (1,)), ((), ())), preferred_element_type=F32)


def _rms_mod(x, g, shift, scale):
    ms = jnp.mean(x * x, axis=-1, keepdims=True)
    h = x * lax.rsqrt(ms + EPS) * g
    return h * (1.0 + scale) + shift


def _gelu(z):
    return 0.5 * z * (1.0 + lax.erf(z * np.float32(np.sqrt(0.5))))


def _split_bf16(a):
    hi = a.astype(BF16)
    lo = (a - hi.astype(F32)).astype(BF16)
    return hi, lo


def _params(n_axes):
    return pltpu.CompilerParams(dimension_semantics=("arbitrary",) * n_axes,
                                vmem_limit_bytes=VMEM_LIMIT)


def _resident(shape, stack_index=None):
    nd = len(shape)
    if stack_index is None:
        return pl.BlockSpec(shape, lambda *_: (0,) * nd, pipeline_mode=pl.Buffered(1))
    return pl.BlockSpec((None,) + tuple(shape), lambda *_: (stack_index,) + (0,) * nd,
                        pipeline_mode=pl.Buffered(1))


def _adaln_kernel(c_ref, w_ref, b_ref, o_ref):
    c = c_ref[...]
    s = c * (1.0 / (1.0 + jnp.exp(-c)))
    s_hi, s_lo = _split_bf16(s)
    w_hi, w_lo = _split_bf16(w_ref[...])
    o_ref[...] = _dot(s_hi, w_hi) + (_dot(s_lo, w_hi) + _dot(s_hi, w_lo)) + b_ref[...]


def _adaln(cond, ada_w, ada_b):
    tn = D_MODEL
    return pl.pallas_call(
        _adaln_kernel,
        grid=(DEPTH, 6 * D_MODEL // tn),
        in_specs=[
            pl.BlockSpec((MOD_ROWS, D_MODEL), lambda i, n: (0, 0)),
            pl.BlockSpec((None, D_MODEL, tn), lambda i, n: (i, 0, n)),
            pl.BlockSpec((None, 1, tn), lambda i, n: (i, 0, n)),
        ],
        out_specs=pl.BlockSpec((None, MOD_ROWS, tn), lambda i, n: (i, 0, n)),
        out_shape=jax.ShapeDtypeStruct((DEPTH, MOD_ROWS, 6 * D_MODEL), F32),
        compiler_params=_params(2),
        name="adaln",
    )(cond, ada_w, ada_b.reshape(DEPTH, 1, 6 * D_MODEL))


def _mod_spec(layer, k, tiles_per_batch):
    base = layer * MOD_ROWS * 6 + k
    if tiles_per_batch is None:
        return pl.BlockSpec((None, 1, D_MODEL), lambda t: (base + CTX_MOD_ROW * 6, 0, 0))
    return pl.BlockSpec((None, 1, D_MODEL), lambda t: (base + (t // tiles_per_batch) * 6, 0, 0))


def _vec_spec(n, row):
    return pl.BlockSpec((None, 1, n), lambda t: (row, 0, 0))


def _key_col_start(j):
    return int(np.clip(j * NA_KW - NA_KW // 2, 0, GRID_W - NA_KB))


def _qkv_kernel(x_ref, g_ref, sh_ref, sc_ref, w_ref, qg_ref, kg_ref, *out_refs, need_q, gather):
    h = _rms_mod(x_ref[...], g_ref[...], sh_ref[...], sc_ref[...]).astype(BF16)
    first = lax.broadcasted_iota(jnp.int32, (1, LANES), 1) < NA_HEAD_DIM

    def head_norm(y, gain):
        parts = []
        for c in range(0, D_MODEL, LANES):
            t = y[:, c:c + LANES]
            sq = t * t
            s_first = jnp.sum(jnp.where(first, sq, 0.0), axis=-1, keepdims=True)
            s_both = jnp.sum(sq, axis=-1, keepdims=True)
            ms = jnp.where(first, s_first, s_both - s_first) * (1.0 / NA_HEAD_DIM)
            parts.append(t * lax.rsqrt(ms + EPS))
        return jnp.concatenate(parts, axis=1) * gain

    refs = list(out_refs)
    if need_q:
        refs.pop(0)[...] = head_norm(_dot(h, w_ref[:, 0:D_MODEL]), qg_ref[...]).astype(BF16)
    k_ref, v_ref = refs
    k = head_norm(_dot(h, w_ref[:, D_MODEL:2 * D_MODEL]), kg_ref[...])
    v = _dot(h, w_ref[:, 2 * D_MODEL:3 * D_MODEL])
    if not gather:
        k_ref[...] = k.astype(BF16)
        v_ref[...] = v.astype(BF16)
        return
    rows = k.shape[0] // GRID_W
    k3 = k.reshape(rows, GRID_W, D_MODEL)
    v3 = v.reshape(rows, GRID_W, D_MODEL)
    for j in range(NA_NCB):
        c0 = _key_col_start(j)
        k_ref[:, j, :, :] = k3[:, c0:c0 + NA_KB, :].astype(BF16)
        v_ref[:, j, :, :] = v3[:, c0:c0 + NA_KB, :].astype(BF16)


def _qkv(x, mods, layer, norm_g, w_qkv, mixer, q_gain, k_gain, *, latent, need_q, tm):
    n = x.shape[0]
    tiles_per_batch = (SEQ // tm) if latent else None
    tok = pl.BlockSpec((tm, D_MODEL), lambda t: (t, 0))
    out_specs, out_shapes = [], []
    if need_q:
        out_specs.append(tok)
        out_shapes.append(jax.ShapeDtypeStruct((n, D_MODEL), BF16))
    if latent:
        rows = tm // GRID_W
        kv_spec = pl.BlockSpec((rows, NA_NCB, NA_KB, D_MODEL), lambda t: (t, 0, 0, 0))
        kv_shape = jax.ShapeDtypeStruct((n // GRID_W, NA_NCB, NA_KB, D_MODEL), BF16)
    else:
        kv_spec, kv_shape = tok, jax.ShapeDtypeStruct((n, D_MODEL), BF16)
    out_specs += [kv_spec, kv_spec]
    out_shapes += [kv_shape, kv_shape]
    return pl.pallas_call(
        functools.partial(_qkv_kernel, need_q=need_q, gather=latent),
        grid=(n // tm,),
        in_specs=[
            tok,
            _vec_spec(D_MODEL, layer),
            _mod_spec(layer, 0, tiles_per_batch),
            _mod_spec(layer, 1, tiles_per_batch),
            _resident((D_MODEL, 3 * D_MODEL), mixer),
            _vec_spec(D_MODEL, mixer),
            _vec_spec(D_MODEL, mixer),
        ],
        out_specs=out_specs,
        out_shape=out_shapes,
        compiler_params=_params(1),
        name="qkv_lat" if latent else "qkv_ctx",
    )(x, norm_g, mods, mods, w_qkv, q_gain, k_gain)


def _pair_scores(q2, keys, bias_pair=None, shift=None):
    lane = lax.broadcasted_iota(jnp.int32, (1, LANES), 1)
    first = lane < NA_HEAD_DIM
    zero = jnp.zeros_like(q2)
    q4 = jnp.concatenate([jnp.where(first, q2, zero), jnp.where(first, zero, q2)], axis=0)
    s = _dot_nt(q4, keys)
    if bias_pair is None:
        return s
    nb = bias_pair[0].shape[1]
    rest = [s[:, i:i + LANES] - shift for i in range(nb, s.shape[1], LANES)]
    return jnp.concatenate([s[:, :nb] + jnp.concatenate(bias_pair, axis=0)] + rest, axis=1)


def _pair_softmax(s, row_max=True):
    tiles = [s[:, i:i + LANES] for i in range(0, s.shape[1], LANES)]
    if row_max:
        mx = jnp.max(functools.reduce(jnp.maximum, tiles), axis=-1, keepdims=True)
        tiles = [t - mx for t in tiles]
    probs = [jnp.exp2(t) for t in tiles]
    denom = jnp.sum(functools.reduce(lambda a, b: a + b, probs), axis=-1, keepdims=True)
    return jnp.concatenate([p.astype(BF16) for p in probs], axis=1), denom


def _pair_pv(p, denom, values):
    m_rows = p.shape[0] // 2
    o = _dot(p, values) / denom
    lane = lax.broadcasted_iota(jnp.int32, (1, LANES), 1)
    return jnp.where(lane < NA_HEAD_DIM, o[:m_rows], o[m_rows:])


def _attn_lat_kernel(q_ref, k0, k1, k2, k3, v0, v1, v2, v3, kc_ref, vc_ref, bias_ref, shift_ref,
                     o_ref, *, row_max):
    kv_rows = KEY_ROW_GROUP * NA_KB

    def lanes(hp):
        return slice(hp * LANES, (hp + 1) * LANES)

    def gather(refs, ctx_ref, t):
        bi, hp = divmod(t, HEAD_PAIRS)
        return jnp.concatenate(
            [r[bi, :, :, lanes(hp)].reshape(kv_rows, LANES) for r in refs] + [ctx_ref[bi, :, lanes(hp)]],
            axis=0)

    def scores(t):
        bi, hp = divmod(t, HEAD_PAIRS)
        q2 = q_ref[bi, :, :, lanes(hp)].reshape(Q_PATCH, LANES)
        return _pair_scores(q2, gather((k0, k1, k2, k3), kc_ref, t),
                            (bias_ref[2 * hp], bias_ref[2 * hp + 1]), shift_ref[...])

    n_stages = q_ref.shape[0] * HEAD_PAIRS
    s_next, prev = scores(0), None
    for t in range(n_stages + 1):
        s = s_next
        if t + 1 < n_stages:
            s_next = scores(t + 1)
        if prev is not None:
            bi, hp = divmod(t - 1, HEAD_PAIRS)
            o = _pair_pv(*prev, gather((v0, v1, v2, v3), vc_ref, t - 1))
            o_ref[bi, :, :, lanes(hp)] = o.reshape(Q_ROWS, NA_KW, LANES).astype(BF16)
        if t < n_stages:
            prev = _pair_softmax(s, row_max)


def _key_group_start(rb):
    return jnp.clip(2 * rb - 1, 0, GRID_ROWS // KEY_ROW_GROUP - KEY_GROUPS)


def _attn_lat(q, k, v, kc, vc, bias, attn_layer, logit_bound):
    n_rb = GRID_ROWS // Q_ROWS
    nb = ATTN_BATCHES
    q4 = q.reshape(BATCH, GRID_ROWS, GRID_W, D_MODEL)
    k5 = k.reshape(BATCH, GRID_ROWS, NA_NCB, NA_KB, D_MODEL)
    v5 = v.reshape(BATCH, GRID_ROWS, NA_NCB, NA_KB, D_MODEL)
    kc3 = kc.reshape(BATCH, CTX_LEN, D_MODEL)
    vc3 = vc.reshape(BATCH, CTX_LEN, D_MODEL)
    q_spec = pl.BlockSpec((nb, Q_ROWS, NA_KW, D_MODEL), lambda rb, j, b: (b, rb, j, 0))

    def kv_spec(i):
        return pl.BlockSpec(
            (nb, KEY_ROW_GROUP, None, NA_KB, D_MODEL),
            lambda rb, j, b: (b, _key_group_start(rb) + i, j, 0, 0))

    ctx_spec = pl.BlockSpec((nb, CTX_LEN, D_MODEL), lambda rb, j, b: (b, 0, 0))

    def bias_index(rb, j, b):
        variant = (rb > 0).astype(jnp.int32) + (rb == n_rb - 1).astype(jnp.int32)
        return (variant * NA_NCB + j, attn_layer, 0, 0)

    bias_spec = pl.BlockSpec((None, NA_HEADS, Q_PATCH, LAT_KEYS), bias_index)
    shift = jnp.full((1, LANES), logit_bound, F32)

    def run(row_max):
        return pl.pallas_call(
            functools.partial(_attn_lat_kernel, row_max=row_max),
            grid=(n_rb, NA_NCB, BATCH // nb),
            in_specs=([q_spec] + [kv_spec(i) for i in range(KEY_GROUPS)] * 2
                      + [ctx_spec, ctx_spec, bias_spec, pl.BlockSpec((1, LANES), lambda rb, j, b: (0, 0))]),
            out_specs=q_spec,
            out_shape=jax.ShapeDtypeStruct(q4.shape, BF16),
            compiler_params=_params(3),
            name="attn_lat_rowmax" if row_max else "attn_lat",
        )(q4, k5, k5, k5, k5, v5, v5, v5, v5, kc3, vc3, bias, shift)

    out = lax.cond(logit_bound <= EXP2_SAFE_BOUND, lambda: run(False), lambda: run(True))
    return out.reshape(BATCH * SEQ, D_MODEL)


def _attn_ctx_kernel(q_ref, k_ref, v_ref, o_ref):
    for hp in range(HEAD_PAIRS):
        sl = slice(hp * LANES, (hp + 1) * LANES)
        p, denom = _pair_softmax(_pair_scores(q_ref[:, sl], k_ref[:, sl], None))
        o_ref[:, sl] = _pair_pv(p, denom, v_ref[:, sl]).astype(BF16)


def _attn_ctx(qc, kc, vc):
    spec = pl.BlockSpec((CTX_LEN, D_MODEL), lambda b: (b, 0))
    return pl.pallas_call(
        _attn_ctx_kernel,
        grid=(BATCH,),
        in_specs=[spec, spec, spec],
        out_specs=spec,
        out_shape=jax.ShapeDtypeStruct(qc.shape, BF16),
        compiler_params=_params(1),
        name="attn_ctx",
    )(qc, kc, vc)


def _attention_bias_tables(rpb):
    n_rb = GRID_ROWS // Q_ROWS
    n_heads = rpb.shape[0]
    col_pad, row_pad = NA_KW, Q_ROWS
    rpb_p = jnp.pad(rpb, ((0, 0), (row_pad, row_pad), (col_pad, col_pad)))
    per_tile = LANES // NA_KB
    max_shift = NA_KH - 1 + row_pad
    width = (max_shift // per_tile) * LANES + LAT_KEYS
    by_col = []
    for j in range(NA_NCB):
        col_off = _key_col_start(j) - j * NA_KW + NA_KW - 1 + col_pad
        flat = jnp.stack([rpb_p[:, :, col_off - qi:col_off - qi + NA_KB]
                          for qi in range(NA_KW)], axis=1).reshape(n_heads, NA_KW, -1)
        flat = jnp.pad(flat, ((0, 0), (0, 0), (0, width + LANES - flat.shape[2])))
        by_col.append([flat[:, :, sh * NA_KB:sh * NA_KB + width] for sh in range(per_tile)])
    tables = []
    for rb in (0, 1, n_rb - 1):
        g0 = int(np.clip(2 * rb - 1, 0, GRID_ROWS // KEY_ROW_GROUP - KEY_GROUPS))
        row_off = g0 * KEY_ROW_GROUP - rb * Q_ROWS + NA_KH - 1 + row_pad
        for j in range(NA_NCB):
            shifts = [divmod(row_off - ql, per_tile) for ql in range(Q_ROWS)]
            by_row = jnp.stack(
                [by_col[j][sh][:, :, tile * LANES:tile * LANES + LAT_KEYS] for tile, sh in shifts],
                axis=1)
            by_row = by_row.reshape(n_heads, Q_PATCH, LAT_KEYS)
            qrow = (rb * Q_ROWS + np.arange(Q_ROWS))[:, None, None, None]
            qcol = (j * NA_KW + np.arange(NA_KW))[None, :, None, None]
            krow = (g0 * KEY_ROW_GROUP + np.arange(KEY_ROWS))[None, None, :, None]
            kcol = (_key_col_start(j) + np.arange(NA_KB))[None, None, None, :]
            rs = np.clip(qrow - NA_KH // 2, 0, GRID_ROWS - NA_KH)
            ws = np.clip(qcol - NA_KW // 2, 0, GRID_W - NA_KW)
            ok = (krow >= rs) & (krow < rs + NA_KH) & (kcol >= ws) & (kcol < ws + NA_KW)
            ok = np.broadcast_to(ok, (Q_ROWS, NA_KW, KEY_ROWS, NA_KB)).reshape(Q_PATCH, LAT_KEYS)
            tables.append(jnp.where(ok[None], by_row, NEG_INF))
    return jnp.stack(tables, axis=0)


def _mlp_kernel(*refs, with_proj):
    if with_proj:
        x_ref, a_ref, wo_ref, g1_ref, refs = refs[0], refs[1], refs[2], refs[3], refs[4:]
        x = x_ref[...] + g1_ref[...] * _dot(a_ref[...], wo_ref[...])
    else:
        x_ref, refs = refs[0], refs[1:]
        x = x_ref[...]
    n_ref, sh_ref, sc_ref, g2_ref, w1_ref, w2_ref, o_ref = refs
    h = _rms_mod(x, n_ref[...], sh_ref[...], sc_ref[...]).astype(BF16)
    t = jnp.maximum(_dot(h, w1_ref[...]), 0.0)
    t = (t * t).astype(BF16)
    o_ref[...] = x + g2_ref[...] * _dot(t, w2_ref[...])


def _mlp(x, mods, layer, norm_g, w1, w2, *, latent, tm, attn=None, w_o=None, mixer=None):
    n = x.shape[0]
    tiles_per_batch = (SEQ // tm) if latent else None
    tok = pl.BlockSpec((tm, D_MODEL), lambda t: (t, 0))
    with_proj = attn is not None
    in_specs, args = [tok], [x]
    if with_proj:
        in_specs += [tok, _resident((D_MODEL, D_MODEL), mixer), _mod_spec(layer, 2, tiles_per_batch)]
        args += [attn, w_o, mods]
    in_specs += [
        _vec_spec(D_MODEL, layer),
        _mod_spec(layer, 3, tiles_per_batch),
        _mod_spec(layer, 4, tiles_per_batch),
        _mod_spec(layer, 5, tiles_per_batch),
        _resident((D_MODEL, MLP_HIDDEN), layer),
        _resident((MLP_HIDDEN, D_MODEL), layer),
    ]
    args += [norm_g, mods, mods, mods, w1, w2]
    return pl.pallas_call(
        functools.partial(_mlp_kernel, with_proj=with_proj),
        grid=(n // tm,),
        in_specs=in_specs,
        out_specs=tok,
        out_shape=jax.ShapeDtypeStruct((n, D_MODEL), F32),
        compiler_params=_params(1),
        name=("proj_mlp" if with_proj else "mlp") + ("_lat" if latent else "_ctx"),
    )(*args)


SG_COLS = 1024


def _sg_kernel(x_ref, n_ref, sh_ref, sc_ref, g1_ref, win_ref, bin_ref, lng_ref, lnb_ref,
               ws_ref, bs_ref, wo_ref, o_ref, v_scr, t_scr):
    x = x_ref[...]
    tm = x.shape[0]
    n_chunks = tm // SG_CHUNK
    h = _rms_mod(x, n_ref[...], sh_ref[...], sc_ref[...]).astype(BF16)
    n_steps = SG_HALF // SG_COLS
    groups_per_step = SG_COLS // SG_GROUP_CH

    def cols(c):
        return slice(c * SG_COLS, (c + 1) * SG_COLS)

    def lane_tile_sum(a):
        return functools.reduce(lambda p, q: p + q,
                                [a[:, i:i + LANES] for i in range(0, a.shape[1], LANES)])

    s1 = s2 = None
    for c in range(n_steps):
        vc = slice(SG_HALF + c * SG_COLS, SG_HALF + (c + 1) * SG_COLS)
        v_blk = _gelu(_dot(h, win_ref[:, vc]) + bin_ref[:, vc])
        v_scr[:, cols(c)] = v_blk
        p1, p2 = lane_tile_sum(v_blk), lane_tile_sum(v_blk * v_blk)
        s1, s2 = (p1, p2) if s1 is None else (s1 + p1, s2 + p2)
    mu = jnp.sum(s1, axis=-1, keepdims=True) * (1.0 / SG_HALF)
    var = jnp.sum(s2, axis=-1, keepdims=True) * (1.0 / SG_HALF) - mu * mu
    rstd = lax.rsqrt(var + EPS)

    for c in range(n_steps):
        u = _gelu(_dot(h, win_ref[:, cols(c)]) + bin_ref[:, cols(c)])
        vn = ((v_scr[:, cols(c)] - mu) * rstd * lng_ref[:, cols(c)] + lnb_ref[:, cols(c)]).astype(BF16)
        for gg in range(groups_per_step):
            g = c * groups_per_step + gg
            gl = slice(gg * SG_GROUP_CH, (gg + 1) * SG_GROUP_CH)
            rhs = jnp.concatenate(
                [vn[i * SG_CHUNK:(i + 1) * SG_CHUNK, gl] for i in range(n_chunks)], axis=1)
            s = _dot(ws_ref[g], rhs)
            for i in range(n_chunks):
                rows = slice(i * SG_CHUNK, (i + 1) * SG_CHUNK)
                s_i = s[:, i * SG_GROUP_CH:(i + 1) * SG_GROUP_CH] + bs_ref[g]
                t_scr[rows, g * SG_GROUP_CH:(g + 1) * SG_GROUP_CH] = (u[rows, gl] * s_i).astype(BF16)
    o_ref[...] = x + g1_ref[...] * _dot(t_scr[...], wo_ref[...])


def _sg(x, mods, layer, norm_g, mixer, w_in, b_in, ln_g, ln_b, w_s, b_s, w_o, *, latent, tm):
    n = x.shape[0]
    tiles_per_batch = (SEQ // tm) if latent else None
    tok = pl.BlockSpec((tm, D_MODEL), lambda t: (t, 0))
    return pl.pallas_call(
        _sg_kernel,
        grid=(n // tm,),
        in_specs=[
            tok,
            _vec_spec(D_MODEL, layer),
            _mod_spec(layer, 0, tiles_per_batch),
            _mod_spec(layer, 1, tiles_per_batch),
            _mod_spec(layer, 2, tiles_per_batch),
            _resident((D_MODEL, 2 * SG_HALF), mixer),
            _vec_spec(2 * SG_HALF, mixer),
            _vec_spec(SG_HALF, mixer),
            _vec_spec(SG_HALF, mixer),
            _resident((SG_GROUPS, SG_CHUNK, SG_CHUNK), mixer),
            _resident((SG_GROUPS, SG_CHUNK, SG_GROUP_CH), mixer),
            _resident((SG_HALF, D_MODEL), mixer),
        ],
        out_specs=tok,
        out_shape=jax.ShapeDtypeStruct((n, D_MODEL), F32),
        scratch_shapes=[pltpu.VMEM((tm, SG_HALF), F32), pltpu.VMEM((tm, SG_HALF), BF16)],
        compiler_params=_params(1),
        name="sg_lat" if latent else "sg_ctx",
    )(x, norm_g, mods, mods, mods, w_in, b_in, ln_g, ln_b, w_s, b_s, w_o)


LAT_TM = 512
CTX_TM = 256


def kernel(x, c, ctx, c_ctx, ada_w, ada_b, norm1_g, norm2_g, mlp_w1, mlp_w2,
           na_w_qkv, na_q_norm, na_k_norm, na_rpb, na_w_o,
           sg_w_in, sg_b_in, sg_ln_g, sg_ln_b, sg_w_s, sg_b_s, sg_w_o):
    last_ctx_layer = ((DEPTH - 1) // N_MIXERS) * N_MIXERS
    xl = x.reshape(BATCH * SEQ, D_MODEL)
    xc = ctx.reshape(BATCH * CTX_LEN, D_MODEL)

    cond = jnp.concatenate(
        [c, c_ctx[None, :], jnp.zeros((MOD_ROWS - BATCH - 1, D_MODEL), F32)], axis=0)
    mods = _adaln(cond, ada_w, ada_b).reshape(DEPTH * MOD_ROWS * 6, 1, D_MODEL)

    q_gains = na_q_norm * (NA_HEAD_DIM ** -0.5 * LOG2_E)
    rpb2 = na_rpb * LOG2_E
    logit_bounds = (NA_HEAD_DIM * 1.01 * jnp.max(jnp.abs(q_gains), axis=1)
                    * jnp.max(jnp.abs(na_k_norm), axis=1) + jnp.max(jnp.abs(rpb2), axis=(1, 2, 3)))
    bias = _attention_bias_tables(
        (rpb2 - logit_bounds[:, None, None, None]).reshape((-1,) + na_rpb.shape[2:]))

    w1, w2 = mlp_w1.astype(BF16), mlp_w2.astype(BF16)
    w_qkv, w_ao = na_w_qkv.astype(BF16), na_w_o.astype(BF16)
    w_in, w_s, w_so = sg_w_in.astype(BF16), sg_w_s.astype(BF16), sg_w_o.astype(BF16)

    n1, n2 = norm1_g[:, None, :], norm2_g[:, None, :]
    q_gain = jnp.tile(q_gains, (1, NA_HEADS))[:, None, :]
    k_gain = jnp.tile(na_k_norm, (1, NA_HEADS))[:, None, :]
    sg_vecs = (sg_b_in[:, None, :], sg_ln_g[:, None, :], sg_ln_b[:, None, :])
    sg_bias = jnp.broadcast_to(sg_b_s[:, :, :, None], sg_b_s.shape + (SG_GROUP_CH,))

    for i in range(DEPTH):
        ctx_full = i < last_ctx_layer
        if i % N_MIXERS == 0:
            a = i // N_MIXERS
            q, k, v = _qkv(xl, mods, i, n1, w_qkv, a, q_gain, k_gain,
                           latent=True, need_q=True, tm=LAT_TM)
            ctx_proj = _qkv(xc, mods, i, n1, w_qkv, a, q_gain, k_gain,
                            latent=False, need_q=ctx_full, tm=CTX_TM)
            kc, vc = ctx_proj[-2], ctx_proj[-1]
            att = _attn_lat(q, k, v, kc, vc, bias, a, logit_bounds[a])
            xl = _mlp(xl, mods, i, n2, w1, w2, latent=True, tm=LAT_TM, attn=att, w_o=w_ao, mixer=a)
            if ctx_full:
                att_c = _attn_ctx(ctx_proj[0], kc, vc)
                xc = _mlp(xc, mods, i, n2, w1, w2, latent=False, tm=CTX_TM, attn=att_c, w_o=w_ao,
                          mixer=a)
        else:
            s = i // N_MIXERS
            sg_args = (s, w_in, *sg_vecs, w_s, sg_bias, w_so)
            xl = _sg(xl, mods, i, n1, *sg_args, latent=True, tm=LAT_TM)
            xl = _mlp(xl, mods, i, n2, w1, w2, latent=True, tm=LAT_TM)
            if ctx_full:
                xc = _sg(xc, mods, i, n1, *sg_args, latent=False, tm=CTX_TM)
                xc = _mlp(xc, mods, i, n2, w1, w2, latent=False, tm=CTX_TM)
    return xl.reshape(BATCH, SEQ, D_MODEL)
```

```python
import functools

import numpy as np
import jax
import jax.numpy as jnp
from jax import lax
from jax.experimental import pallas as pl
from jax.experimental.pallas import tpu as pltpu

D_MODEL = 1024
BATCH = 8
SEQ = 4096
DEPTH = 4
GRID_W = 64
GRID_ROWS = SEQ // GRID_W
CTX_LEN = 256
N_MIXERS = 2
NA_HEADS = 16
NA_HEAD_DIM = D_MODEL // NA_HEADS
NA_KH = 8
NA_KW = 16
NA_KB = 2 * NA_KW
NA_NCB = GRID_W // NA_KW
SG_CHUNK = 128
SG_HALF = 3 * D_MODEL
SG_GROUP_CH = 128
SG_GROUPS = SG_HALF // SG_GROUP_CH
MLP_HIDDEN = 4 * D_MODEL
EPS = 1e-6
NEG_INF = -1e30
LOG2_E = float(np.log2(np.e))
EXP2_SAFE_BOUND = 60.0

F32 = jnp.float32
BF16 = jnp.bfloat16

LANES = 128
HEAD_PAIRS = NA_HEADS // 2
MOD_ROWS = 16
CTX_MOD_ROW = BATCH
Q_ROWS = 8
Q_PATCH = Q_ROWS * NA_KW
KEY_ROW_GROUP = 4
KEY_GROUPS = 4
KEY_ROWS = KEY_ROW_GROUP * KEY_GROUPS
LAT_KEYS = KEY_ROWS * NA_KB
ATTN_BATCHES = 4
VMEM_LIMIT = 56 * 1024 * 1024


def _dot(a, b):
    return jnp.dot(a, b, preferred_element_type=F32)


def _dot_nt(a, b):
    return lax.dot_general(a, b, (((1,), (1,)), ((), ())), preferred_element_type=F32)


def _rms_mod(x, g, shift, scale):
    ms = jnp.mean(x * x, axis=-1, keepdims=True)
    h = x * lax.rsqrt(ms + EPS) * g
    return h * (1.0 + scale) + shift


def _gelu(z):
    return 0.5 * z * (1.0 + lax.erf(z * np.float32(np.sqrt(0.5))))


def _split_bf16(a):
    hi = a.astype(BF16)
    lo = (a - hi.astype(F32)).astype(BF16)
    return hi, lo


def _params(n_axes):
    return pltpu.CompilerParams(dimension_semantics=("arbitrary",) * n_axes,
                                vmem_limit_bytes=VMEM_LIMIT)


def _resident(shape, stack_index=None):
    nd = len(shape)
    if stack_index is None:
        return pl.BlockSpec(shape, lambda *_: (0,) * nd, pipeline_mode=pl.Buffered(1))
    return pl.BlockSpec((None,) + tuple(shape), lambda *_: (stack_index,) + (0,) * nd,
                        pipeline_mode=pl.Buffered(1))


def _adaln_kernel(c_ref, w_ref, b_ref, o_ref):
    c = c_ref[...]
    s = c * (1.0 / (1.0 + jnp.exp(-c)))
    s_hi, s_lo = _split_bf16(s)
    w_hi, w_lo = _split_bf16(w_ref[...])
    o_ref[...] = _dot(s_hi, w_hi) + (_dot(s_lo, w_hi) + _dot(s_hi, w_lo)) + b_ref[...]


def _adaln(cond, ada_w, ada_b):
    tn = D_MODEL
    return pl.pallas_call(
        _adaln_kernel,
        grid=(DEPTH, 6 * D_MODEL // tn),
        in_specs=[
            pl.BlockSpec((MOD_ROWS, D_MODEL), lambda i, n: (0, 0)),
            pl.BlockSpec((None, D_MODEL, tn), lambda i, n: (i, 0, n)),
            pl.BlockSpec((None, 1, tn), lambda i, n: (i, 0, n)),
        ],
        out_specs=pl.BlockSpec((None, MOD_ROWS, tn), lambda i, n: (i, 0, n)),
        out_shape=jax.ShapeDtypeStruct((DEPTH, MOD_ROWS, 6 * D_MODEL), F32),
        compiler_params=_params(2),
        name="adaln",
    )(cond, ada_w, ada_b.reshape(DEPTH, 1, 6 * D_MODEL))


def _mod_spec(layer, k, tiles_per_batch):
    base = layer * MOD_ROWS * 6 + k
    if tiles_per_batch is None:
        return pl.BlockSpec((None, 1, D_MODEL), lambda t: (base + CTX_MOD_ROW * 6, 0, 0))
    return pl.BlockSpec((None, 1, D_MODEL), lambda t: (base + (t // tiles_per_batch) * 6, 0, 0))


def _vec_spec(n, row):
    return pl.BlockSpec((None, 1, n), lambda t: (row, 0, 0))


def _key_col_start(j):
    return int(np.clip(j * NA_KW - NA_KW // 2, 0, GRID_W - NA_KB))


def _qkv_kernel(x_ref, g_ref, sh_ref, sc_ref, w_ref, qg_ref, kg_ref, *out_refs, need_q, gather):
    h = _rms_mod(x_ref[...], g_ref[...], sh_ref[...], sc_ref[...]).astype(BF16)
    first = lax.broadcasted_iota(jnp.int32, (1, LANES), 1) < NA_HEAD_DIM

    def head_norm(y, gain):
        parts = []
        for c in range(0, D_MODEL, LANES):
            t = y[:, c:c + LANES]
            sq = t * t
            s_first = jnp.sum(jnp.where(first, sq, 0.0), axis=-1, keepdims=True)
            s_both = jnp.sum(sq, axis=-1, keepdims=True)
            ms = jnp.where(first, s_first, s_both - s_first) * (1.0 / NA_HEAD_DIM)
            parts.append(t * lax.rsqrt(ms + EPS))
        return jnp.concatenate(parts, axis=1) * gain

    refs = list(out_refs)
    if need_q:
        refs.pop(0)[...] = head_norm(_dot(h, w_ref[:, 0:D_MODEL]), qg_ref[...]).astype(BF16)
    k_ref, v_ref = refs
    k = head_norm(_dot(h, w_ref[:, D_MODEL:2 * D_MODEL]), kg_ref[...])
    v = _dot(h, w_ref[:, 2 * D_MODEL:3 * D_MODEL])
    if not gather:
        k_ref[...] = k.astype(BF16)
        v_ref[...] = v.astype(BF16)
        return
    rows = k.shape[0] // GRID_W
    k3 = k.reshape(rows, GRID_W, D_MODEL)
    v3 = v.reshape(rows, GRID_W, D_MODEL)
    for j in range(NA_NCB):
        c0 = _key_col_start(j)
        k_ref[:, j, :, :] = k3[:, c0:c0 + NA_KB, :].astype(BF16)
        v_ref[:, j, :, :] = v3[:, c0:c0 + NA_KB, :].astype(BF16)


def _qkv(x, mods, layer, norm_g, w_qkv, mixer, q_gain, k_gain, *, latent, need_q, tm):
    n = x.shape[0]
    tiles_per_batch = (SEQ // tm) if latent else None
    tok = pl.BlockSpec((tm, D_MODEL), lambda t: (t, 0))
    out_specs, out_shapes = [], []
    if need_q:
        out_specs.append(tok)
        out_shapes.append(jax.ShapeDtypeStruct((n, D_MODEL), BF16))
    if latent:
        rows = tm // GRID_W
        kv_spec = pl.BlockSpec((rows, NA_NCB, NA_KB, D_MODEL), lambda t: (t, 0, 0, 0))
        kv_shape = jax.ShapeDtypeStruct((n // GRID_W, NA_NCB, NA_KB, D_MODEL), BF16)
    else:
        kv_spec, kv_shape = tok, jax.ShapeDtypeStruct((n, D_MODEL), BF16)
    out_specs += [kv_spec, kv_spec]
    out_shapes += [kv_shape, kv_shape]
    return pl.pallas_call(
        functools.partial(_qkv_kernel, need_q=need_q, gather=latent),
        grid=(n // tm,),
        in_specs=[
            tok,
            _vec_spec(D_MODEL, layer),
            _mod_spec(layer, 0, tiles_per_batch),
            _mod_spec(layer, 1, tiles_per_batch),
            _resident((D_MODEL, 3 * D_MODEL), mixer),
            _vec_spec(D_MODEL, mixer),
            _vec_spec(D_MODEL, mixer),
        ],
        out_specs=out_specs,
        out_shape=out_shapes,
        compiler_params=_params(1),
        name="qkv_lat" if latent else "qkv_ctx",
    )(x, norm_g, mods, mods, w_qkv, q_gain, k_gain)


def _pair_scores(q2, keys, bias_pair=None, shift=None):
    lane = lax.broadcasted_iota(jnp.int32, (1, LANES), 1)
    first = lane < NA_HEAD_DIM
    zero = jnp.zeros_like(q2)
    q4 = jnp.concatenate([jnp.where(first, q2, zero), jnp.where(first, zero, q2)], axis=0)
    s = _dot_nt(q4, keys)
    if bias_pair is None:
        return s
    nb = bias_pair[0].shape[1]
    rest = [s[:, i:i + LANES] - shift for i in range(nb, s.shape[1], LANES)]
    return jnp.concatenate([s[:, :nb] + jnp.concatenate(bias_pair, axis=0)] + rest, axis=1)


def _pair_softmax(s, row_max=True):
    tiles = [s[:, i:i + LANES] for i in range(0, s.shape[1], LANES)]
    if row_max:
        mx = jnp.max(functools.reduce(jnp.maximum, tiles), axis=-1, keepdims=True)
        tiles = [t - mx for t in tiles]
    probs = [jnp.exp2(t) for t in tiles]
    denom = jnp.sum(functools.reduce(lambda a, b: a + b, probs), axis=-1, keepdims=True)
    return jnp.concatenate([p.astype(BF16) for p in probs], axis=1), denom


def _pair_pv(p, denom, values):
    m_rows = p.shape[0] // 2
    o = _dot(p, values) / denom
    lane = lax.broadcasted_iota(jnp.int32, (1, LANES), 1)
    return jnp.where(lane < NA_HEAD_DIM, o[:m_rows], o[m_rows:])


def _attn_lat_kernel(q_ref, k0, k1, k2, k3, v0, v1, v2, v3, kc_ref, vc_ref, bias_ref, shift_ref,
                     o_ref, *, row_max):
    kv_rows = KEY_ROW_GROUP * NA_KB

    def lanes(hp):
        return slice(hp * LANES, (hp + 1) * LANES)

    def gather(refs, ctx_ref, t):
        bi, hp = divmod(t, HEAD_PAIRS)
        return jnp.concatenate(
            [r[bi, :, :, lanes(hp)].reshape(kv_rows, LANES) for r in refs] + [ctx_ref[bi, :, lanes(hp)]],
            axis=0)

    def scores(t):
        bi, hp = divmod(t, HEAD_PAIRS)
        q2 = q_ref[bi, :, :, lanes(hp)].reshape(Q_PATCH, LANES)
        return _pair_scores(q2, gather((k0, k1, k2, k3), kc_ref, t),
                            (bias_ref[2 * hp], bias_ref[2 * hp + 1]), shift_ref[...])

    n_stages = q_ref.shape[0] * HEAD_PAIRS
    s_next, prev = scores(0), None
    for t in range(n_stages + 1):
        s = s_next
        if t + 1 < n_stages:
            s_next = scores(t + 1)
        if prev is not None:
            bi, hp = divmod(t - 1, HEAD_PAIRS)
            o = _pair_pv(*prev, gather((v0, v1, v2, v3), vc_ref, t - 1))
            o_ref[bi, :, :, lanes(hp)] = o.reshape(Q_ROWS, NA_KW, LANES).astype(BF16)
        if t < n_stages:
            prev = _pair_softmax(s, row_max)


def _key_group_start(rb):
    return jnp.clip(2 * rb - 1, 0, GRID_ROWS // KEY_ROW_GROUP - KEY_GROUPS)


def _attn_lat(q, k, v, kc, vc, bias, attn_layer, logit_bound):
    n_rb = GRID_ROWS // Q_ROWS
    nb = ATTN_BATCHES
    q4 = q.reshape(BATCH, GRID_ROWS, GRID_W, D_MODEL)
    k5 = k.reshape(BATCH, GRID_ROWS, NA_NCB, NA_KB, D_MODEL)
    v5 = v.reshape(BATCH, GRID_ROWS, NA_NCB, NA_KB, D_MODEL)
    kc3 = kc.reshape(BATCH, CTX_LEN, D_MODEL)
    vc3 = vc.reshape(BATCH, CTX_LEN, D_MODEL)
    q_spec = pl.BlockSpec((nb, Q_ROWS, NA_KW, D_MODEL), lambda rb, j, b: (b, rb, j, 0))

    def kv_spec(i):
        return pl.BlockSpec(
            (nb, KEY_ROW_GROUP, None, NA_KB, D_MODEL),
            lambda rb, j, b: (b, _key_group_start(rb) + i, j, 0, 0))

    ctx_spec = pl.BlockSpec((nb, CTX_LEN, D_MODEL), lambda rb, j, b: (b, 0, 0))

    def bias_index(rb, j, b):
        variant = (rb > 0).astype(jnp.int32) + (rb == n_rb - 1).astype(jnp.int32)
        return (variant * NA_NCB + j, attn_layer, 0, 0)

    bias_spec = pl.BlockSpec((None, NA_HEADS, Q_PATCH, LAT_KEYS), bias_index)
    shift = jnp.full((1, LANES), logit_bound, F32)

    def run(row_max):
        return pl.pallas_call(
            functools.partial(_attn_lat_kernel, row_max=row_max),
            grid=(n_rb, NA_NCB, BATCH // nb),
            in_specs=([q_spec] + [kv_spec(i) for i in range(KEY_GROUPS)] * 2
                      + [ctx_spec, ctx_spec, bias_spec, pl.BlockSpec((1, LANES), lambda rb, j, b: (0, 0))]),
            out_specs=q_spec,
            out_shape=jax.ShapeDtypeStruct(q4.shape, BF16),
            compiler_params=_params(3),
            name="attn_lat_rowmax" if row_max else "attn_lat",
        )(q4, k5, k5, k5, k5, v5, v5, v5, v5, kc3, vc3, bias, shift)

    out = lax.cond(logit_bound <= EXP2_SAFE_BOUND, lambda: run(False), lambda: run(True))
    return out.reshape(BATCH * SEQ, D_MODEL)


def _attn_ctx_kernel(q_ref, k_ref, v_ref, o_ref):
    for hp in range(HEAD_PAIRS):
        sl = slice(hp * LANES, (hp + 1) * LANES)
        p, denom = _pair_softmax(_pair_scores(q_ref[:, sl], k_ref[:, sl], None))
        o_ref[:, sl] = _pair_pv(p, denom, v_ref[:, sl]).astype(BF16)


def _attn_ctx(qc, kc, vc):
    spec = pl.BlockSpec((CTX_LEN, D_MODEL), lambda b: (b, 0))
    return pl.pallas_call(
        _attn_ctx_kernel,
        grid=(BATCH,),
        in_specs=[spec, spec, spec],
        out_specs=spec,
        out_shape=jax.ShapeDtypeStruct(qc.shape, BF16),
        compiler_params=_params(1),
        name="attn_ctx",
    )(qc, kc, vc)


def _attention_bias_tables(rpb):
    n_rb = GRID_ROWS // Q_ROWS
    n_heads = rpb.shape[0]
    col_pad, row_pad = NA_KW, Q_ROWS
    rpb_p = jnp.pad(rpb, ((0, 0), (row_pad, row_pad), (col_pad, col_pad)))
    by_col = []
    for j in range(NA_NCB):
        col_off = _key_col_start(j) - j * NA_KW + NA_KW - 1 + col_pad
        flat = jnp.stack([rpb_p[:, :, col_off - qi:col_off - qi + NA_KB]
                          for qi in range(NA_KW)], axis=1).reshape(n_heads, NA_KW, -1)
        by_col.append(jnp.pad(flat, ((0, 0), (0, 0), (0, BIAS_SRC_LANES - flat.shape[2]))))
    by_col = jnp.stack(by_col, axis=0)
    valid, row_offs = [], []
    for rb in (0, 1, n_rb - 1):
        g0 = int(np.clip(2 * rb - 1, 0, GRID_ROWS // KEY_ROW_GROUP - KEY_GROUPS))
        row_offs.append(g0 * KEY_ROW_GROUP - rb * Q_ROWS + NA_KH - 1 + row_pad)
        for j in range(NA_NCB):
            qrow = (rb * Q_ROWS + np.arange(Q_ROWS))[:, None, None, None]
            qcol = (j * NA_KW + np.arange(NA_KW))[None, :, None, None]
            krow = (g0 * KEY_ROW_GROUP + np.arange(KEY_ROWS))[None, None, :, None]
            kcol = (_key_col_start(j) + np.arange(NA_KB))[None, None, None, :]
            rs = np.clip(qrow - NA_KH // 2, 0, GRID_ROWS - NA_KH)
            ws = np.clip(qcol - NA_KW // 2, 0, GRID_W - NA_KW)
            ok = (krow >= rs) & (krow < rs + NA_KH) & (kcol >= ws) & (kcol < ws + NA_KW)
            valid.append(np.broadcast_to(ok, (Q_ROWS, NA_KW, KEY_ROWS, NA_KB)).reshape(Q_PATCH, LAT_KEYS))
    valid = jnp.asarray(np.stack(valid).reshape(3, NA_NCB, Q_PATCH, LAT_KEYS), F32)

    heads_per_step = 8
    out = pl.pallas_call(
        functools.partial(_bias_table_kernel, row_offs=tuple(row_offs)),
        grid=(NA_NCB, n_heads // heads_per_step),
        in_specs=[
            pl.BlockSpec((None, heads_per_step, NA_KW, BIAS_SRC_LANES), lambda j, hb: (j, hb, 0, 0)),
            pl.BlockSpec((3, None, Q_PATCH, LAT_KEYS), lambda j, hb: (0, j, 0, 0)),
        ],
        out_specs=pl.BlockSpec((3, None, heads_per_step, Q_PATCH, LAT_KEYS),
                               lambda j, hb: (0, j, hb, 0, 0)),
        out_shape=jax.ShapeDtypeStruct((3, NA_NCB, n_heads, Q_PATCH, LAT_KEYS), F32),
        compiler_params=_params(2),
        name="bias_table",
    )(by_col, valid)
    return out.reshape(3 * NA_NCB, n_heads, Q_PATCH, LAT_KEYS)


BIAS_SRC_LANES = 1024


def _bias_table_kernel(src_ref, valid_ref, o_ref, *, row_offs):
    for v, row_off in enumerate(row_offs):
        for ql in range(Q_ROWS):
            start = (row_off - ql) * NA_KB
            rows = slice(ql * NA_KW, (ql + 1) * NA_KW)
            ok = valid_ref[v, rows, :] > 0.0
            for h in range(src_ref.shape[0]):
                src = src_ref[h]
                if start:
                    src = pltpu.roll(src, BIAS_SRC_LANES - start, axis=1)
                o_ref[v, h, rows, :] = jnp.where(ok, src[:, :LAT_KEYS], NEG_INF)


def _mlp_kernel(*refs, with_proj):
    if with_proj:
        x_ref, a_ref, wo_ref, g1_ref, refs = refs[0], refs[1], refs[2], refs[3], refs[4:]
        x = x_ref[...] + g1_ref[...] * _dot(a_ref[...], wo_ref[...])
    else:
        x_ref, refs = refs[0], refs[1:]
        x = x_ref[...]
    n_ref, sh_ref, sc_ref, g2_ref, w1_ref, w2_ref, o_ref = refs
    h = _rms_mod(x, n_ref[...], sh_ref[...], sc_ref[...]).astype(BF16)
    t = jnp.maximum(_dot(h, w1_ref[...]), 0.0)
    t = (t * t).astype(BF16)
    o_ref[...] = x + g2_ref[...] * _dot(t, w2_ref[...])


def _mlp(x, mods, layer, norm_g, w1, w2, *, latent, tm, attn=None, w_o=None, mixer=None):
    n = x.shape[0]
    tiles_per_batch = (SEQ // tm) if latent else None
    tok = pl.BlockSpec((tm, D_MODEL), lambda t: (t, 0))
    with_proj = attn is not None
    in_specs, args = [tok], [x]
    if with_proj:
        in_specs += [tok, _resident((D_MODEL, D_MODEL), mixer), _mod_spec(layer, 2, tiles_per_batch)]
        args += [attn, w_o, mods]
    in_specs += [
        _vec_spec(D_MODEL, layer),
        _mod_spec(layer, 3, tiles_per_batch),
        _mod_spec(layer, 4, tiles_per_batch),
        _mod_spec(layer, 5, tiles_per_batch),
        _resident((D_MODEL, MLP_HIDDEN), layer),
        _resident((MLP_HIDDEN, D_MODEL), layer),
    ]
    args += [norm_g, mods, mods, mods, w1, w2]
    return pl.pallas_call(
        functools.partial(_mlp_kernel, with_proj=with_proj),
        grid=(n // tm,),
        in_specs=in_specs,
        out_specs=tok,
        out_shape=jax.ShapeDtypeStruct((n, D_MODEL), F32),
        compiler_params=_params(1),
        name=("proj_mlp" if with_proj else "mlp") + ("_lat" if latent else "_ctx"),
    )(*args)


SG_COLS = 1024


def _sg_kernel(x_ref, n_ref, sh_ref, sc_ref, g1_ref, win_ref, bin_ref, lng_ref, lnb_ref,
               ws_ref, bs_ref, wo_ref, o_ref, v_scr, t_scr):
    x = x_ref[...]
    tm = x.shape[0]
    n_chunks = tm // SG_CHUNK
    h = _rms_mod(x, n_ref[...], sh_ref[...], sc_ref[...]).astype(BF16)
    n_steps = SG_HALF // SG_COLS
    groups_per_step = SG_COLS // SG_GROUP_CH

    def cols(c):
        return slice(c * SG_COLS, (c + 1) * SG_COLS)

    def lane_tile_sum(a):
        return functools.reduce(lambda p, q: p + q,
                                [a[:, i:i + LANES] for i in range(0, a.shape[1], LANES)])

    s1 = s2 = None
    for c in range(n_steps):
        vc = slice(SG_HALF + c * SG_COLS, SG_HALF + (c + 1) * SG_COLS)
        v_blk = _gelu(_dot(h, win_ref[:, vc]) + bin_ref[:, vc])
        v_scr[:, cols(c)] = v_blk
        p1, p2 = lane_tile_sum(v_blk), lane_tile_sum(v_blk * v_blk)
        s1, s2 = (p1, p2) if s1 is None else (s1 + p1, s2 + p2)
    mu = jnp.sum(s1, axis=-1, keepdims=True) * (1.0 / SG_HALF)
    var = jnp.sum(s2, axis=-1, keepdims=True) * (1.0 / SG_HALF) - mu * mu
    rstd = lax.rsqrt(var + EPS)

    for c in range(n_steps):
        u = _gelu(_dot(h, win_ref[:, cols(c)]) + bin_ref[:, cols(c)])
        vn = ((v_scr[:, cols(c)] - mu) * rstd * lng_ref[:, cols(c)] + lnb_ref[:, cols(c)]).astype(BF16)
        for gg in range(groups_per_step):
            g = c * groups_per_step + gg
            gl = slice(gg * SG_GROUP_CH, (gg + 1) * SG_GROUP_CH)
            rhs = jnp.concatenate(
                [vn[i * SG_CHUNK:(i + 1) * SG_CHUNK, gl] for i in range(n_chunks)], axis=1)
            s = _dot(ws_ref[g], rhs)
            for i in range(n_chunks):
                rows = slice(i * SG_CHUNK, (i + 1) * SG_CHUNK)
                s_i = s[:, i * SG_GROUP_CH:(i + 1) * SG_GROUP_CH] + bs_ref[g]
                t_scr[rows, g * SG_GROUP_CH:(g + 1) * SG_GROUP_CH] = (u[rows, gl] * s_i).astype(BF16)
    o_ref[...] = x + g1_ref[...] * _dot(t_scr[...], wo_ref[...])


def _sg(x, mods, layer, norm_g, mixer, w_in, b_in, ln_g, ln_b, w_s, b_s, w_o, *, latent, tm):
    n = x.shape[0]
    tiles_per_batch = (SEQ // tm) if latent else None
    tok = pl.BlockSpec((tm, D_MODEL), lambda t: (t, 0))
    return pl.pallas_call(
        _sg_kernel,
        grid=(n // tm,),
        in_specs=[
            tok,
            _vec_spec(D_MODEL, layer),
            _mod_spec(layer, 0, tiles_per_batch),
            _mod_spec(layer, 1, tiles_per_batch),
            _mod_spec(layer, 2, tiles_per_batch),
            _resident((D_MODEL, 2 * SG_HALF), mixer),
            _vec_spec(2 * SG_HALF, mixer),
            _vec_spec(SG_HALF, mixer),
            _vec_spec(SG_HALF, mixer),
            _resident((SG_GROUPS, SG_CHUNK, SG_CHUNK), mixer),
            _resident((SG_GROUPS, SG_CHUNK, SG_GROUP_CH), mixer),
            _resident((SG_HALF, D_MODEL), mixer),
        ],
        out_specs=tok,
        out_shape=jax.ShapeDtypeStruct((n, D_MODEL), F32),
        scratch_shapes=[pltpu.VMEM((tm, SG_HALF), F32), pltpu.VMEM((tm, SG_HALF), BF16)],
        compiler_params=_params(1),
        name="sg_lat" if latent else "sg_ctx",
    )(x, norm_g, mods, mods, mods, w_in, b_in, ln_g, ln_b, w_s, b_s, w_o)


LAT_TM = 512
CTX_TM = 256


def kernel(x, c, ctx, c_ctx, ada_w, ada_b, norm1_g, norm2_g, mlp_w1, mlp_w2,
           na_w_qkv, na_q_norm, na_k_norm, na_rpb, na_w_o,
           sg_w_in, sg_b_in, sg_ln_g, sg_ln_b, sg_w_s, sg_b_s, sg_w_o):
    last_ctx_layer = ((DEPTH - 1) // N_MIXERS) * N_MIXERS
    xl = x.reshape(BATCH * SEQ, D_MODEL)
    xc = ctx.reshape(BATCH * CTX_LEN, D_MODEL)

    cond = jnp.concatenate(
        [c, c_ctx[None, :], jnp.zeros((MOD_ROWS - BATCH - 1, D_MODEL), F32)], axis=0)
    mods = _adaln(cond, ada_w, ada_b).reshape(DEPTH * MOD_ROWS * 6, 1, D_MODEL)

    q_gains = na_q_norm * (NA_HEAD_DIM ** -0.5 * LOG2_E)
    rpb2 = na_rpb * LOG2_E
    logit_bounds = (NA_HEAD_DIM * 1.01 * jnp.max(jnp.abs(q_gains), axis=1)
                    * jnp.max(jnp.abs(na_k_norm), axis=1) + jnp.max(jnp.abs(rpb2), axis=(1, 2, 3)))
    bias = _attention_bias_tables(
        (rpb2 - logit_bounds[:, None, None, None]).reshape((-1,) + na_rpb.shape[2:]))

    w1, w2 = mlp_w1.astype(BF16), mlp_w2.astype(BF16)
    w_qkv, w_ao = na_w_qkv.astype(BF16), na_w_o.astype(BF16)
    w_in, w_s, w_so = sg_w_in.astype(BF16), sg_w_s.astype(BF16), sg_w_o.astype(BF16)

    n1, n2 = norm1_g[:, None, :], norm2_g[:, None, :]
    q_gain = jnp.tile(q_gains, (1, NA_HEADS))[:, None, :]
    k_gain = jnp.tile(na_k_norm, (1, NA_HEADS))[:, None, :]
    sg_vecs = (sg_b_in[:, None, :], sg_ln_g[:, None, :], sg_ln_b[:, None, :])
    sg_bias = jnp.broadcast_to(sg_b_s[:, :, :, None], sg_b_s.shape + (SG_GROUP_CH,))

    for i in range(DEPTH):
        ctx_full = i < last_ctx_layer
        if i % N_MIXERS == 0:
            a = i // N_MIXERS
            q, k, v = _qkv(xl, mods, i, n1, w_qkv, a, q_gain, k_gain,
                           latent=True, need_q=True, tm=LAT_TM)
            ctx_proj = _qkv(xc, mods, i, n1, w_qkv, a, q_gain, k_gain,
                            latent=False, need_q=ctx_full, tm=CTX_TM)
            kc, vc = ctx_proj[-2], ctx_proj[-1]
            att = _attn_lat(q, k, v, kc, vc, bias, a, logit_bounds[a])
            xl = _mlp(xl, mods, i, n2, w1, w2, latent=True, tm=LAT_TM, attn=att, w_o=w_ao, mixer=a)
            if ctx_full:
                att_c = _attn_ctx(ctx_proj[0], kc, vc)
                xc = _mlp(xc, mods, i, n2, w1, w2, latent=False, tm=CTX_TM, attn=att_c, w_o=w_ao,
                          mixer=a)
        else:
            s = i // N_MIXERS
            sg_args = (s, w_in, *sg_vecs, w_s, sg_bias, w_so)
            xl = _sg(xl, mods, i, n1, *sg_args, latent=True, tm=LAT_TM)
            xl = _mlp(xl, mods, i, n2, w1, w2, latent=True, tm=LAT_TM)
            if ctx_full:
                xc = _sg(xc, mods, i, n1, *sg_args, latent=False, tm=CTX_TM)
                xc = _mlp(xc, mods, i, n2, w1, w2, latent=False, tm=CTX_TM)
    return xl.reshape(BATCH, SEQ, D_MODEL)
```

```python
import functools

import numpy as np
import jax
import jax.numpy as jnp
from jax import lax
from jax.experimental import pallas as pl
from jax.experimental.pallas import tpu as pltpu

D_MODEL = 1024
BATCH = 8
SEQ = 4096
DEPTH = 4
GRID_W = 64
GRID_ROWS = SEQ // GRID_W
CTX_LEN = 256
N_MIXERS = 2
NA_HEADS = 16
NA_HEAD_DIM = D_MODEL // NA_HEADS
NA_KH = 8
NA_KW = 16
NA_KB = 2 * NA_KW
NA_NCB = GRID_W // NA_KW
SG_CHUNK = 128
SG_HALF = 3 * D_MODEL
SG_GROUP_CH = 128
SG_GROUPS = SG_HALF // SG_GROUP_CH
MLP_HIDDEN = 4 * D_MODEL
EPS = 1e-6
NEG_INF = -1e30
LOG2_E = float(np.log2(np.e))
EXP2_SAFE_BOUND = 60.0

F32 = jnp.float32
BF16 = jnp.bfloat16

LANES = 128
HEAD_PAIRS = NA_HEADS // 2
MOD_ROWS = 16
CTX_MOD_ROW = BATCH
Q_ROWS = 8
Q_PATCH = Q_ROWS * NA_KW
KEY_ROW_GROUP = 4
KEY_GROUPS = 4
KEY_ROWS = KEY_ROW_GROUP * KEY_GROUPS
LAT_KEYS = KEY_ROWS * NA_KB
ATTN_BATCHES = 4
VMEM_LIMIT = 56 * 1024 * 1024


def _dot(a, b):
    return jnp.dot(a, b, preferred_element_type=F32)


def _dot_nt(a, b):
    return lax.dot_general(a, b, (((1,), (1,)), ((), ())), preferred_element_type=F32)


def _rms_mod(x, g, shift, scale):
    ms = jnp.mean(x * x, axis=-1, keepdims=True)
    h = x * lax.rsqrt(ms + EPS) * g
    return h * (1.0 + scale) + shift


def _gelu(z):
    half = np.asarray(0.5, z.dtype)
    return half * z * (np.asarray(1.0, z.dtype) + lax.erf(z * np.asarray(np.sqrt(0.5), z.dtype)))


def _split_bf16(a):
    hi = a.astype(BF16)
    lo = (a - hi.astype(F32)).astype(BF16)
    return hi, lo


def _params(n_axes):
    return pltpu.CompilerParams(dimension_semantics=("arbitrary",) * n_axes,
                                vmem_limit_bytes=VMEM_LIMIT)


def _resident(shape, stack_index=None):
    nd = len(shape)
    if stack_index is None:
        return pl.BlockSpec(shape, lambda *_: (0,) * nd, pipeline_mode=pl.Buffered(1))
    return pl.BlockSpec((None,) + tuple(shape), lambda *_: (stack_index,) + (0,) * nd,
                        pipeline_mode=pl.Buffered(1))


def _adaln_kernel(c_ref, w_ref, b_ref, o_ref):
    c = c_ref[...]
    s = c * (1.0 / (1.0 + jnp.exp(-c)))
    s_hi, s_lo = _split_bf16(s)
    w_hi, w_lo = _split_bf16(w_ref[...])
    o_ref[...] = _dot(s_hi, w_hi) + (_dot(s_lo, w_hi) + _dot(s_hi, w_lo)) + b_ref[...]


def _adaln(cond, ada_w, ada_b):
    tn = D_MODEL
    return pl.pallas_call(
        _adaln_kernel,
        grid=(DEPTH, 6 * D_MODEL // tn),
        in_specs=[
            pl.BlockSpec((MOD_ROWS, D_MODEL), lambda i, n: (0, 0)),
            pl.BlockSpec((None, D_MODEL, tn), lambda i, n: (i, 0, n)),
            pl.BlockSpec((None, 1, tn), lambda i, n: (i, 0, n)),
        ],
        out_specs=pl.BlockSpec((None, MOD_ROWS, tn), lambda i, n: (i, 0, n)),
        out_shape=jax.ShapeDtypeStruct((DEPTH, MOD_ROWS, 6 * D_MODEL), F32),
        compiler_params=_params(2),
        name="adaln",
    )(cond, ada_w, ada_b.reshape(DEPTH, 1, 6 * D_MODEL))


def _mod_spec(layer, k, tiles_per_batch):
    base = layer * MOD_ROWS * 6 + k
    if tiles_per_batch is None:
        return pl.BlockSpec((None, 1, D_MODEL), lambda t: (base + CTX_MOD_ROW * 6, 0, 0))
    return pl.BlockSpec((None, 1, D_MODEL), lambda t: (base + (t // tiles_per_batch) * 6, 0, 0))


def _vec_spec(n, row):
    return pl.BlockSpec((None, 1, n), lambda t: (row, 0, 0))


def _key_col_start(j):
    return int(np.clip(j * NA_KW - NA_KW // 2, 0, GRID_W - NA_KB))


def _qkv_kernel(x_ref, g_ref, sh_ref, sc_ref, w_ref, qg_ref, kg_ref, *out_refs, need_q, gather):
    h = _rms_mod(x_ref[...], g_ref[...], sh_ref[...], sc_ref[...]).astype(BF16)
    first = lax.broadcasted_iota(jnp.int32, (1, LANES), 1) < NA_HEAD_DIM

    def head_norm(y, gain):
        parts = []
        for c in range(0, D_MODEL, LANES):
            t = y[:, c:c + LANES]
            sq = t * t
            s_first = jnp.sum(jnp.where(first, sq, 0.0), axis=-1, keepdims=True)
            s_both = jnp.sum(sq, axis=-1, keepdims=True)
            ms = jnp.where(first, s_first, s_both - s_first) * (1.0 / NA_HEAD_DIM)
            parts.append(t * lax.rsqrt(ms + EPS))
        return jnp.concatenate(parts, axis=1) * gain

    refs = list(out_refs)
    if need_q:
        refs.pop(0)[...] = head_norm(_dot(h, w_ref[:, 0:D_MODEL]), qg_ref[...]).astype(BF16)
    k_ref, v_ref = refs
    k = head_norm(_dot(h, w_ref[:, D_MODEL:2 * D_MODEL]), kg_ref[...])
    v = _dot(h, w_ref[:, 2 * D_MODEL:3 * D_MODEL])
    if not gather:
        k_ref[...] = k.astype(BF16)
        v_ref[...] = v.astype(BF16)
        return
    rows = k.shape[0] // GRID_W
    k3 = k.reshape(rows, GRID_W, D_MODEL)
    v3 = v.reshape(rows, GRID_W, D_MODEL)
    for j in range(NA_NCB):
        c0 = _key_col_start(j)
        k_ref[:, j, :, :] = k3[:, c0:c0 + NA_KB, :].astype(BF16)
        v_ref[:, j, :, :] = v3[:, c0:c0 + NA_KB, :].astype(BF16)


def _qkv(x, mods, layer, norm_g, w_qkv, mixer, q_gain, k_gain, *, latent, need_q, tm):
    n = x.shape[0]
    tiles_per_batch = (SEQ // tm) if latent else None
    tok = pl.BlockSpec((tm, D_MODEL), lambda t: (t, 0))
    out_specs, out_shapes = [], []
    if need_q:
        out_specs.append(tok)
        out_shapes.append(jax.ShapeDtypeStruct((n, D_MODEL), BF16))
    if latent:
        rows = tm // GRID_W
        kv_spec = pl.BlockSpec((rows, NA_NCB, NA_KB, D_MODEL), lambda t: (t, 0, 0, 0))
        kv_shape = jax.ShapeDtypeStruct((n // GRID_W, NA_NCB, NA_KB, D_MODEL), BF16)
    else:
        kv_spec, kv_shape = tok, jax.ShapeDtypeStruct((n, D_MODEL), BF16)
    out_specs += [kv_spec, kv_spec]
    out_shapes += [kv_shape, kv_shape]
    return pl.pallas_call(
        functools.partial(_qkv_kernel, need_q=need_q, gather=latent),
        grid=(n // tm,),
        in_specs=[
            tok,
            _vec_spec(D_MODEL, layer),
            _mod_spec(layer, 0, tiles_per_batch),
            _mod_spec(layer, 1, tiles_per_batch),
            _resident((D_MODEL, 3 * D_MODEL), mixer),
            _vec_spec(D_MODEL, mixer),
            _vec_spec(D_MODEL, mixer),
        ],
        out_specs=out_specs,
        out_shape=out_shapes,
        compiler_params=_params(1),
        name="qkv_lat" if latent else "qkv_ctx",
    )(x, norm_g, mods, mods, w_qkv, q_gain, k_gain)


def _pair_scores(q2, keys, bias_pair=None, shift=None):
    lane = lax.broadcasted_iota(jnp.int32, (1, LANES), 1)
    first = lane < NA_HEAD_DIM
    zero = jnp.zeros_like(q2)
    q4 = jnp.concatenate([jnp.where(first, q2, zero), jnp.where(first, zero, q2)], axis=0)
    s = _dot_nt(q4, keys)
    if bias_pair is None:
        return s
    nb = bias_pair[0].shape[1]
    rest = [s[:, i:i + LANES] - shift for i in range(nb, s.shape[1], LANES)]
    return jnp.concatenate([s[:, :nb] + jnp.concatenate(bias_pair, axis=0)] + rest, axis=1)


def _pair_softmax(s, row_max=True):
    tiles = [s[:, i:i + LANES] for i in range(0, s.shape[1], LANES)]
    if row_max:
        mx = jnp.max(functools.reduce(jnp.maximum, tiles), axis=-1, keepdims=True)
        tiles = [t - mx for t in tiles]
    probs = [jnp.exp2(t) for t in tiles]
    denom = jnp.sum(functools.reduce(lambda a, b: a + b, probs), axis=-1, keepdims=True)
    return jnp.concatenate([p.astype(BF16) for p in probs], axis=1), denom


def _pair_pv(p, denom, values):
    m_rows = p.shape[0] // 2
    o = _dot(p, values) / denom
    lane = lax.broadcasted_iota(jnp.int32, (1, LANES), 1)
    return jnp.where(lane < NA_HEAD_DIM, o[:m_rows], o[m_rows:])


def _attn_lat_kernel(q_ref, k0, k1, k2, k3, v0, v1, v2, v3, kc_ref, vc_ref, bias_ref, shift_ref,
                     o_ref, *, row_max):
    kv_rows = KEY_ROW_GROUP * NA_KB

    def lanes(hp):
        return slice(hp * LANES, (hp + 1) * LANES)

    def gather(refs, ctx_ref, t):
        bi, hp = divmod(t, HEAD_PAIRS)
        return jnp.concatenate(
            [r[bi, :, :, lanes(hp)].reshape(kv_rows, LANES) for r in refs] + [ctx_ref[bi, :, lanes(hp)]],
            axis=0)

    def scores(t):
        bi, hp = divmod(t, HEAD_PAIRS)
        q2 = q_ref[bi, :, :, lanes(hp)].reshape(Q_PATCH, LANES)
        return _pair_scores(q2, gather((k0, k1, k2, k3), kc_ref, t),
                            (bias_ref[2 * hp], bias_ref[2 * hp + 1]), shift_ref[...])

    n_stages = q_ref.shape[0] * HEAD_PAIRS
    s_next, prev = scores(0), None
    for t in range(n_stages + 1):
        s = s_next
        if t + 1 < n_stages:
            s_next = scores(t + 1)
        if prev is not None:
            bi, hp = divmod(t - 1, HEAD_PAIRS)
            o = _pair_pv(*prev, gather((v0, v1, v2, v3), vc_ref, t - 1))
            o_ref[bi, :, :, lanes(hp)] = o.reshape(Q_ROWS, NA_KW, LANES).astype(BF16)
        if t < n_stages:
            prev = _pair_softmax(s, row_max)


def _key_group_start(rb):
    return jnp.clip(2 * rb - 1, 0, GRID_ROWS // KEY_ROW_GROUP - KEY_GROUPS)


def _attn_lat(q, k, v, kc, vc, bias, attn_layer, logit_bound):
    n_rb = GRID_ROWS // Q_ROWS
    nb = ATTN_BATCHES
    q4 = q.reshape(BATCH, GRID_ROWS, GRID_W, D_MODEL)
    k5 = k.reshape(BATCH, GRID_ROWS, NA_NCB, NA_KB, D_MODEL)
    v5 = v.reshape(BATCH, GRID_ROWS, NA_NCB, NA_KB, D_MODEL)
    kc3 = kc.reshape(BATCH, CTX_LEN, D_MODEL)
    vc3 = vc.reshape(BATCH, CTX_LEN, D_MODEL)
    q_spec = pl.BlockSpec((nb, Q_ROWS, NA_KW, D_MODEL), lambda rb, j, b: (b, rb, j, 0))

    def kv_spec(i):
        return pl.BlockSpec(
            (nb, KEY_ROW_GROUP, None, NA_KB, D_MODEL),
            lambda rb, j, b: (b, _key_group_start(rb) + i, j, 0, 0))

    ctx_spec = pl.BlockSpec((nb, CTX_LEN, D_MODEL), lambda rb, j, b: (b, 0, 0))

    def bias_index(rb, j, b):
        variant = (rb > 0).astype(jnp.int32) + (rb == n_rb - 1).astype(jnp.int32)
        return (variant * NA_NCB + j, attn_layer, 0, 0)

    bias_spec = pl.BlockSpec((None, NA_HEADS, Q_PATCH, LAT_KEYS), bias_index)
    shift = jnp.full((1, LANES), logit_bound, F32)

    def run(row_max):
        return pl.pallas_call(
            functools.partial(_attn_lat_kernel, row_max=row_max),
            grid=(n_rb, NA_NCB, BATCH // nb),
            in_specs=([q_spec] + [kv_spec(i) for i in range(KEY_GROUPS)] * 2
                      + [ctx_spec, ctx_spec, bias_spec, pl.BlockSpec((1, LANES), lambda rb, j, b: (0, 0))]),
            out_specs=q_spec,
            out_shape=jax.ShapeDtypeStruct(q4.shape, BF16),
            compiler_params=_params(3),
            name="attn_lat_rowmax" if row_max else "attn_lat",
        )(q4, k5, k5, k5, k5, v5, v5, v5, v5, kc3, vc3, bias, shift)

    out = lax.cond(logit_bound <= EXP2_SAFE_BOUND, lambda: run(False), lambda: run(True))
    return out.reshape(BATCH * SEQ, D_MODEL)


def _attn_ctx_kernel(q_ref, k_ref, v_ref, o_ref):
    for hp in range(HEAD_PAIRS):
        sl = slice(hp * LANES, (hp + 1) * LANES)
        p, denom = _pair_softmax(_pair_scores(q_ref[:, sl], k_ref[:, sl], None))
        o_ref[:, sl] = _pair_pv(p, denom, v_ref[:, sl]).astype(BF16)


def _attn_ctx(qc, kc, vc):
    spec = pl.BlockSpec((CTX_LEN, D_MODEL), lambda b: (b, 0))
    return pl.pallas_call(
        _attn_ctx_kernel,
        grid=(BATCH,),
        in_specs=[spec, spec, spec],
        out_specs=spec,
        out_shape=jax.ShapeDtypeStruct(qc.shape, BF16),
        compiler_params=_params(1),
        name="attn_ctx",
    )(qc, kc, vc)


def _attention_bias_tables(rpb):
    n_rb = GRID_ROWS // Q_ROWS
    n_heads = rpb.shape[0]
    col_pad, row_pad = NA_KW, Q_ROWS
    rpb_p = jnp.pad(rpb, ((0, 0), (row_pad, row_pad), (col_pad, col_pad)))
    by_col = []
    for j in range(NA_NCB):
        col_off = _key_col_start(j) - j * NA_KW + NA_KW - 1 + col_pad
        flat = jnp.stack([rpb_p[:, :, col_off - qi:col_off - qi + NA_KB]
                          for qi in range(NA_KW)], axis=1).reshape(n_heads, NA_KW, -1)
        by_col.append(jnp.pad(flat, ((0, 0), (0, 0), (0, BIAS_SRC_LANES - flat.shape[2]))))
    by_col = jnp.stack(by_col, axis=0)
    valid, row_offs = [], []
    for rb in (0, 1, n_rb - 1):
        g0 = int(np.clip(2 * rb - 1, 0, GRID_ROWS // KEY_ROW_GROUP - KEY_GROUPS))
        row_offs.append(g0 * KEY_ROW_GROUP - rb * Q_ROWS + NA_KH - 1 + row_pad)
        for j in range(NA_NCB):
            qrow = (rb * Q_ROWS + np.arange(Q_ROWS))[:, None, None, None]
            qcol = (j * NA_KW + np.arange(NA_KW))[None, :, None, None]
            krow = (g0 * KEY_ROW_GROUP + np.arange(KEY_ROWS))[None, None, :, None]
            kcol = (_key_col_start(j) + np.arange(NA_KB))[None, None, None, :]
            rs = np.clip(qrow - NA_KH // 2, 0, GRID_ROWS - NA_KH)
            ws = np.clip(qcol - NA_KW // 2, 0, GRID_W - NA_KW)
            ok = (krow >= rs) & (krow < rs + NA_KH) & (kcol >= ws) & (kcol < ws + NA_KW)
            valid.append(np.broadcast_to(ok, (Q_ROWS, NA_KW, KEY_ROWS, NA_KB)).reshape(Q_PATCH, LAT_KEYS))
    valid = jnp.asarray(np.stack(valid).reshape(3, NA_NCB, Q_PATCH, LAT_KEYS), F32)

    heads_per_step = 8
    out = pl.pallas_call(
        functools.partial(_bias_table_kernel, row_offs=tuple(row_offs)),
        grid=(NA_NCB, n_heads // heads_per_step),
        in_specs=[
            pl.BlockSpec((None, heads_per_step, NA_KW, BIAS_SRC_LANES), lambda j, hb: (j, hb, 0, 0)),
            pl.BlockSpec((3, None, Q_PATCH, LAT_KEYS), lambda j, hb: (0, j, 0, 0)),
        ],
        out_specs=pl.BlockSpec((3, None, heads_per_step, Q_PATCH, LAT_KEYS),
                               lambda j, hb: (0, j, hb, 0, 0)),
        out_shape=jax.ShapeDtypeStruct((3, NA_NCB, n_heads, Q_PATCH, LAT_KEYS), F32),
        compiler_params=_params(2),
        name="bias_table",
    )(by_col, valid)
    return out.reshape(3 * NA_NCB, n_heads, Q_PATCH, LAT_KEYS)


BIAS_SRC_LANES = 1024


def _bias_table_kernel(src_ref, valid_ref, o_ref, *, row_offs):
    for v, row_off in enumerate(row_offs):
        for ql in range(Q_ROWS):
            start = (row_off - ql) * NA_KB
            rows = slice(ql * NA_KW, (ql + 1) * NA_KW)
            ok = valid_ref[v, rows, :] > 0.0
            for h in range(src_ref.shape[0]):
                src = src_ref[h]
                if start:
                    src = pltpu.roll(src, BIAS_SRC_LANES - start, axis=1)
                o_ref[v, h, rows, :] = jnp.where(ok, src[:, :LAT_KEYS], NEG_INF)


def _mlp_kernel(*refs, with_proj):
    if with_proj:
        x_ref, a_ref, wo_ref, g1_ref, refs = refs[0], refs[1], refs[2], refs[3], refs[4:]
        x = x_ref[...] + g1_ref[...] * _dot(a_ref[...], wo_ref[...])
    else:
        x_ref, refs = refs[0], refs[1:]
        x = x_ref[...]
    n_ref, sh_ref, sc_ref, g2_ref, w1_ref, w2_ref, o_ref = refs
    h = _rms_mod(x, n_ref[...], sh_ref[...], sc_ref[...]).astype(BF16)
    t = jnp.maximum(_dot(h, w1_ref[...]), 0.0)
    t = (t * t).astype(BF16)
    o_ref[...] = x + g2_ref[...] * _dot(t, w2_ref[...])


def _mlp(x, mods, layer, norm_g, w1, w2, *, latent, tm, attn=None, w_o=None, mixer=None):
    n = x.shape[0]
    tiles_per_batch = (SEQ // tm) if latent else None
    tok = pl.BlockSpec((tm, D_MODEL), lambda t: (t, 0))
    with_proj = attn is not None
    in_specs, args = [tok], [x]
    if with_proj:
        in_specs += [tok, _resident((D_MODEL, D_MODEL), mixer), _mod_spec(layer, 2, tiles_per_batch)]
        args += [attn, w_o, mods]
    in_specs += [
        _vec_spec(D_MODEL, layer),
        _mod_spec(layer, 3, tiles_per_batch),
        _mod_spec(layer, 4, tiles_per_batch),
        _mod_spec(layer, 5, tiles_per_batch),
        _resident((D_MODEL, MLP_HIDDEN), layer),
        _resident((MLP_HIDDEN, D_MODEL), layer),
    ]
    args += [norm_g, mods, mods, mods, w1, w2]
    return pl.pallas_call(
        functools.partial(_mlp_kernel, with_proj=with_proj),
        grid=(n // tm,),
        in_specs=in_specs,
        out_specs=tok,
        out_shape=jax.ShapeDtypeStruct((n, D_MODEL), F32),
        compiler_params=_params(1),
        name=("proj_mlp" if with_proj else "mlp") + ("_lat" if latent else "_ctx"),
    )(*args)


SG_COLS = 1024


def _sg_kernel(x_ref, n_ref, sh_ref, sc_ref, g1_ref, win_ref, bin_ref, lng_ref, lnb_ref,
               ws_ref, bs_ref, wo_ref, o_ref, v_scr, t_scr):
    x = x_ref[...]
    tm = x.shape[0]
    n_chunks = tm // SG_CHUNK
    h = _rms_mod(x, n_ref[...], sh_ref[...], sc_ref[...]).astype(BF16)
    n_steps = SG_HALF // SG_COLS
    groups_per_step = SG_COLS // SG_GROUP_CH

    def cols(c):
        return slice(c * SG_COLS, (c + 1) * SG_COLS)

    def lane_tile_sum(a):
        return functools.reduce(lambda p, q: p + q,
                                [a[:, i:i + LANES] for i in range(0, a.shape[1], LANES)])

    s1 = s2 = None
    for c in range(n_steps):
        vc = slice(SG_HALF + c * SG_COLS, SG_HALF + (c + 1) * SG_COLS)
        v_blk = _gelu(_dot(h, win_ref[:, vc]) + bin_ref[:, vc])
        v_scr[:, cols(c)] = v_blk
        p1, p2 = lane_tile_sum(v_blk), lane_tile_sum(v_blk * v_blk)
        s1, s2 = (p1, p2) if s1 is None else (s1 + p1, s2 + p2)
    mu = jnp.sum(s1, axis=-1, keepdims=True) * (1.0 / SG_HALF)
    var = jnp.sum(s2, axis=-1, keepdims=True) * (1.0 / SG_HALF) - mu * mu
    rstd = lax.rsqrt(var + EPS)

    for c in range(n_steps):
        u = _gelu((_dot(h, win_ref[:, cols(c)]) + bin_ref[:, cols(c)]).astype(BF16))
        vn = ((v_scr[:, cols(c)] - mu) * rstd * lng_ref[:, cols(c)] + lnb_ref[:, cols(c)]).astype(BF16)
        for gg in range(groups_per_step):
            g = c * groups_per_step + gg
            gl = slice(gg * SG_GROUP_CH, (gg + 1) * SG_GROUP_CH)
            rhs = jnp.concatenate(
                [vn[i * SG_CHUNK:(i + 1) * SG_CHUNK, gl] for i in range(n_chunks)], axis=1)
            s = _dot(ws_ref[g], rhs)
            for i in range(n_chunks):
                rows = slice(i * SG_CHUNK, (i + 1) * SG_CHUNK)
                s_i = s[:, i * SG_GROUP_CH:(i + 1) * SG_GROUP_CH] + bs_ref[g]
                t_scr[rows, g * SG_GROUP_CH:(g + 1) * SG_GROUP_CH] = u[rows, gl] * s_i.astype(BF16)
    o_ref[...] = x + g1_ref[...] * _dot(t_scr[...], wo_ref[...])


def _sg(x, mods, layer, norm_g, mixer, w_in, b_in, ln_g, ln_b, w_s, b_s, w_o, *, latent, tm):
    n = x.shape[0]
    tiles_per_batch = (SEQ // tm) if latent else None
    tok = pl.BlockSpec((tm, D_MODEL), lambda t: (t, 0))
    return pl.pallas_call(
        _sg_kernel,
        grid=(n // tm,),
        in_specs=[
            tok,
            _vec_spec(D_MODEL, layer),
            _mod_spec(layer, 0, tiles_per_batch),
            _mod_spec(layer, 1, tiles_per_batch),
            _mod_spec(layer, 2, tiles_per_batch),
            _resident((D_MODEL, 2 * SG_HALF), mixer),
            _vec_spec(2 * SG_HALF, mixer),
            _vec_spec(SG_HALF, mixer),
            _vec_spec(SG_HALF, mixer),
            _resident((SG_GROUPS, SG_CHUNK, SG_CHUNK), mixer),
            _resident((SG_GROUPS, SG_CHUNK, SG_GROUP_CH), mixer),
            _resident((SG_HALF, D_MODEL), mixer),
        ],
        out_specs=tok,
        out_shape=jax.ShapeDtypeStruct((n, D_MODEL), F32),
        scratch_shapes=[pltpu.VMEM((tm, SG_HALF), F32), pltpu.VMEM((tm, SG_HALF), BF16)],
        compiler_params=_params(1),
        name="sg_lat" if latent else "sg_ctx",
    )(x, norm_g, mods, mods, mods, w_in, b_in, ln_g, ln_b, w_s, b_s, w_o)


LAT_TM = 512
CTX_TM = 256


def kernel(x, c, ctx, c_ctx, ada_w, ada_b, norm1_g, norm2_g, mlp_w1, mlp_w2,
           na_w_qkv, na_q_norm, na_k_norm, na_rpb, na_w_o,
           sg_w_in, sg_b_in, sg_ln_g, sg_ln_b, sg_w_s, sg_b_s, sg_w_o):
    last_ctx_layer = ((DEPTH - 1) // N_MIXERS) * N_MIXERS
    xl = x.reshape(BATCH * SEQ, D_MODEL)
    xc = ctx.reshape(BATCH * CTX_LEN, D_MODEL)

    cond = jnp.concatenate(
        [c, c_ctx[None, :], jnp.zeros((MOD_ROWS - BATCH - 1, D_MODEL), F32)], axis=0)
    mods = _adaln(cond, ada_w, ada_b).reshape(DEPTH * MOD_ROWS * 6, 1, D_MODEL)

    q_gains = na_q_norm * (NA_HEAD_DIM ** -0.5 * LOG2_E)
    rpb2 = na_rpb * LOG2_E
    logit_bounds = (NA_HEAD_DIM * 1.01 * jnp.max(jnp.abs(q_gains), axis=1)
                    * jnp.max(jnp.abs(na_k_norm), axis=1) + jnp.max(jnp.abs(rpb2), axis=(1, 2, 3)))
    bias = _attention_bias_tables(
        (rpb2 - logit_bounds[:, None, None, None]).reshape((-1,) + na_rpb.shape[2:]))

    w1, w2 = mlp_w1.astype(BF16), mlp_w2.astype(BF16)
    w_qkv, w_ao = na_w_qkv.astype(BF16), na_w_o.astype(BF16)
    w_in, w_s, w_so = sg_w_in.astype(BF16), sg_w_s.astype(BF16), sg_w_o.astype(BF16)

    n1, n2 = norm1_g[:, None, :], norm2_g[:, None, :]
    q_gain = jnp.tile(q_gains, (1, NA_HEADS))[:, None, :]
    k_gain = jnp.tile(na_k_norm, (1, NA_HEADS))[:, None, :]
    sg_vecs = (sg_b_in[:, None, :], sg_ln_g[:, None, :], sg_ln_b[:, None, :])
    sg_bias = jnp.broadcast_to(sg_b_s[:, :, :, None], sg_b_s.shape + (SG_GROUP_CH,))

    for i in range(DEPTH):
        ctx_full = i < last_ctx_layer
        if i % N_MIXERS == 0:
            a = i // N_MIXERS
            q, k, v = _qkv(xl, mods, i, n1, w_qkv, a, q_gain, k_gain,
                           latent=True, need_q=True, tm=LAT_TM)
            ctx_proj = _qkv(xc, mods, i, n1, w_qkv, a, q_gain, k_gain,
                            latent=False, need_q=ctx_full, tm=CTX_TM)
            kc, vc = ctx_proj[-2], ctx_proj[-1]
            att = _attn_lat(q, k, v, kc, vc, bias, a, logit_bounds[a])
            xl = _mlp(xl, mods, i, n2, w1, w2, latent=True, tm=LAT_TM, attn=att, w_o=w_ao, mixer=a)
            if ctx_full:
                att_c = _attn_ctx(ctx_proj[0], kc, vc)
                xc = _mlp(xc, mods, i, n2, w1, w2, latent=False, tm=CTX_TM, attn=att_c, w_o=w_ao,
                          mixer=a)
        else:
            s = i // N_MIXERS
            sg_args = (s, w_in, *sg_vecs, w_s, sg_bias, w_so)
            xl = _sg(xl, mods, i, n1, *sg_args, latent=True, tm=LAT_TM)
            xl = _mlp(xl, mods, i, n2, w1, w2, latent=True, tm=LAT_TM)
            if ctx_full:
                xc = _sg(xc, mods, i, n1, *sg_args, latent=False, tm=CTX_TM)
                xc = _mlp(xc, mods, i, n2, w1, w2, latent=False, tm=CTX_TM)
    return xl.reshape(BATCH, SEQ, D_MODEL)
```

```python
import functools

import numpy as np
import jax
import jax.numpy as jnp
from jax import lax
from jax.experimental import pallas as pl
from jax.experimental.pallas import tpu as pltpu

D_MODEL = 1024
BATCH = 8
SEQ = 4096
DEPTH = 4
GRID_W = 64
GRID_ROWS = SEQ // GRID_W
CTX_LEN = 256
N_MIXERS = 2
NA_HEADS = 16
NA_HEAD_DIM = D_MODEL // NA_HEADS
NA_KH = 8
NA_KW = 16
NA_KB = 2 * NA_KW
NA_NCB = GRID_W // NA_KW
SG_CHUNK = 128
SG_HALF = 3 * D_MODEL
SG_GROUP_CH = 128
SG_GROUPS = SG_HALF // SG_GROUP_CH
MLP_HIDDEN = 4 * D_MODEL
EPS = 1e-6
NEG_INF = -1e30
LOG2_E = float(np.log2(np.e))
EXP2_SAFE_BOUND = 60.0

F32 = jnp.float32
BF16 = jnp.bfloat16

LANES = 128
HEAD_PAIRS = NA_HEADS // 2
MOD_ROWS = 16
CTX_MOD_ROW = BATCH
Q_ROWS = 8
Q_PATCH = Q_ROWS * NA_KW
KEY_ROW_GROUP = 4
KEY_GROUPS = 4
KEY_ROWS = KEY_ROW_GROUP * KEY_GROUPS
LAT_KEYS = KEY_ROWS * NA_KB
ATTN_BATCHES = 4
VMEM_LIMIT = 56 * 1024 * 1024


def _dot(a, b):
    return jnp.dot(a, b, preferred_element_type=F32)


def _dot_nt(a, b):
    return lax.dot_general(a, b, (((1,), (1,)), ((), ())), preferred_element_type=F32)


def _rms_mod(x, g, shift, scale):
    ms = jnp.mean(x * x, axis=-1, keepdims=True)
    h = x * lax.rsqrt(ms + EPS) * g
    return h * (1.0 + scale) + shift


def _gelu(z):
    half = np.asarray(0.5, z.dtype)
    return half * z * (np.asarray(1.0, z.dtype) + lax.erf(z * np.asarray(np.sqrt(0.5), z.dtype)))


def _split_bf16(a):
    hi = a.astype(BF16)
    lo = (a - hi.astype(F32)).astype(BF16)
    return hi, lo


def _params(n_axes):
    return pltpu.CompilerParams(dimension_semantics=("arbitrary",) * n_axes,
                                vmem_limit_bytes=VMEM_LIMIT)


def _resident(shape, stack_index=None):
    nd = len(shape)
    if stack_index is None:
        return pl.BlockSpec(shape, lambda *_: (0,) * nd, pipeline_mode=pl.Buffered(1))
    return pl.BlockSpec((None,) + tuple(shape), lambda *_: (stack_index,) + (0,) * nd,
                        pipeline_mode=pl.Buffered(1))


def _adaln_kernel(c_ref, w_ref, b_ref, o_ref):
    c = c_ref[...]
    s = c * (1.0 / (1.0 + jnp.exp(-c)))
    s_hi, s_lo = _split_bf16(s)
    w_hi, w_lo = _split_bf16(w_ref[...])
    o_ref[...] = _dot(s_hi, w_hi) + (_dot(s_lo, w_hi) + _dot(s_hi, w_lo)) + b_ref[...]


def _adaln(cond, ada_w, ada_b):
    tn = D_MODEL
    return pl.pallas_call(
        _adaln_kernel,
        grid=(DEPTH, 6 * D_MODEL // tn),
        in_specs=[
            pl.BlockSpec((MOD_ROWS, D_MODEL), lambda i, n: (0, 0)),
            pl.BlockSpec((None, D_MODEL, tn), lambda i, n: (i, 0, n)),
            pl.BlockSpec((None, 1, tn), lambda i, n: (i, 0, n)),
        ],
        out_specs=pl.BlockSpec((None, MOD_ROWS, tn), lambda i, n: (i, 0, n)),
        out_shape=jax.ShapeDtypeStruct((DEPTH, MOD_ROWS, 6 * D_MODEL), F32),
        compiler_params=_params(2),
        name="adaln",
    )(cond, ada_w, ada_b.reshape(DEPTH, 1, 6 * D_MODEL))


def _mod_spec(layer, k, tiles_per_batch):
    base = layer * MOD_ROWS * 6 + k
    if tiles_per_batch is None:
        return pl.BlockSpec((None, 1, D_MODEL), lambda t: (base + CTX_MOD_ROW * 6, 0, 0))
    return pl.BlockSpec((None, 1, D_MODEL), lambda t: (base + (t // tiles_per_batch) * 6, 0, 0))


def _vec_spec(n, row):
    return pl.BlockSpec((None, 1, n), lambda t: (row, 0, 0))


def _key_col_start(j):
    return int(np.clip(j * NA_KW - NA_KW // 2, 0, GRID_W - NA_KB))


def _qkv_kernel(x_ref, g_ref, sh_ref, sc_ref, w_ref, qg_ref, kg_ref, *out_refs, need_q, gather):
    h = _rms_mod(x_ref[...], g_ref[...], sh_ref[...], sc_ref[...]).astype(BF16)
    first = lax.broadcasted_iota(jnp.int32, (1, LANES), 1) < NA_HEAD_DIM

    def head_norm(y, gain):
        parts = []
        for c in range(0, D_MODEL, LANES):
            t = y[:, c:c + LANES]
            sq = t * t
            s_first = jnp.sum(jnp.where(first, sq, 0.0), axis=-1, keepdims=True)
            s_both = jnp.sum(sq, axis=-1, keepdims=True)
            ms = jnp.where(first, s_first, s_both - s_first) * (1.0 / NA_HEAD_DIM)
            parts.append(t * lax.rsqrt(ms + EPS))
        return jnp.concatenate(parts, axis=1) * gain

    refs = list(out_refs)
    if need_q:
        refs.pop(0)[...] = head_norm(_dot(h, w_ref[:, 0:D_MODEL]), qg_ref[...]).astype(BF16)
    k_ref, v_ref = refs
    k = head_norm(_dot(h, w_ref[:, D_MODEL:2 * D_MODEL]), kg_ref[...])
    v = _dot(h, w_ref[:, 2 * D_MODEL:3 * D_MODEL])
    if not gather:
        k_ref[...] = k.astype(BF16)
        v_ref[...] = v.astype(BF16)
        return
    rows = k.shape[0] // GRID_W
    k3 = k.reshape(rows, GRID_W, D_MODEL)
    v3 = v.reshape(rows, GRID_W, D_MODEL)
    for j in range(NA_NCB):
        c0 = _key_col_start(j)
        k_ref[:, j, :, :] = k3[:, c0:c0 + NA_KB, :].astype(BF16)
        v_ref[:, j, :, :] = v3[:, c0:c0 + NA_KB, :].astype(BF16)


def _qkv(x, mods, layer, norm_g, w_qkv, mixer, q_gain, k_gain, *, latent, need_q, tm):
    n = x.shape[0]
    tiles_per_batch = (SEQ // tm) if latent else None
    tok = pl.BlockSpec((tm, D_MODEL), lambda t: (t, 0))
    out_specs, out_shapes = [], []
    if need_q:
        out_specs.append(tok)
        out_shapes.append(jax.ShapeDtypeStruct((n, D_MODEL), BF16))
    if latent:
        rows = tm // GRID_W
        kv_spec = pl.BlockSpec((rows, NA_NCB, NA_KB, D_MODEL), lambda t: (t, 0, 0, 0))
        kv_shape = jax.ShapeDtypeStruct((n // GRID_W, NA_NCB, NA_KB, D_MODEL), BF16)
    else:
        kv_spec, kv_shape = tok, jax.ShapeDtypeStruct((n, D_MODEL), BF16)
    out_specs += [kv_spec, kv_spec]
    out_shapes += [kv_shape, kv_shape]
    return pl.pallas_call(
        functools.partial(_qkv_kernel, need_q=need_q, gather=latent),
        grid=(n // tm,),
        in_specs=[
            tok,
            _vec_spec(D_MODEL, layer),
            _mod_spec(layer, 0, tiles_per_batch),
            _mod_spec(layer, 1, tiles_per_batch),
            _resident((D_MODEL, 3 * D_MODEL), mixer),
            _vec_spec(D_MODEL, mixer),
            _vec_spec(D_MODEL, mixer),
        ],
        out_specs=out_specs,
        out_shape=out_shapes,
        compiler_params=_params(1),
        name="qkv_lat" if latent else "qkv_ctx",
    )(x, norm_g, mods, mods, w_qkv, q_gain, k_gain)


def _pair_scores(q2, keys, bias_pair=None, shift=None):
    lane = lax.broadcasted_iota(jnp.int32, (1, LANES), 1)
    first = lane < NA_HEAD_DIM
    zero = jnp.zeros_like(q2)
    q4 = jnp.concatenate([jnp.where(first, q2, zero), jnp.where(first, zero, q2)], axis=0)
    s = _dot_nt(q4, keys)
    if bias_pair is None:
        return s
    nb = bias_pair[0].shape[1]
    rest = [s[:, i:i + LANES] - shift for i in range(nb, s.shape[1], LANES)]
    return jnp.concatenate([s[:, :nb] + jnp.concatenate(bias_pair, axis=0)] + rest, axis=1)


def _pair_softmax(s, row_max=True):
    tiles = [s[:, i:i + LANES] for i in range(0, s.shape[1], LANES)]
    if row_max:
        mx = jnp.max(functools.reduce(jnp.maximum, tiles), axis=-1, keepdims=True)
        tiles = [t - mx for t in tiles]
    probs = [jnp.exp2(t) for t in tiles]
    denom = jnp.sum(functools.reduce(lambda a, b: a + b, probs), axis=-1, keepdims=True)
    return jnp.concatenate([p.astype(BF16) for p in probs], axis=1), denom


def _pair_pv(p, denom, values):
    m_rows = p.shape[0] // 2
    o = _dot(p, values) / denom
    lane = lax.broadcasted_iota(jnp.int32, (1, LANES), 1)
    return jnp.where(lane < NA_HEAD_DIM, o[:m_rows], o[m_rows:])


def _attn_lat_kernel(q_ref, k0, k1, k2, k3, v0, v1, v2, v3, kc_ref, vc_ref, bias_ref, shift_ref,
                     o_ref, *, row_max):
    kv_rows = KEY_ROW_GROUP * NA_KB

    def lanes(hp):
        return slice(hp * LANES, (hp + 1) * LANES)

    def gather(refs, ctx_ref, t):
        bi, hp = divmod(t, HEAD_PAIRS)
        return jnp.concatenate(
            [r[bi, :, :, lanes(hp)].reshape(kv_rows, LANES) for r in refs] + [ctx_ref[bi, :, lanes(hp)]],
            axis=0)

    def scores(t):
        bi, hp = divmod(t, HEAD_PAIRS)
        q2 = q_ref[bi, :, :, lanes(hp)].reshape(Q_PATCH, LANES)
        return _pair_scores(q2, gather((k0, k1, k2, k3), kc_ref, t),
                            (bias_ref[2 * hp], bias_ref[2 * hp + 1]), shift_ref[...])

    n_stages = q_ref.shape[0] * HEAD_PAIRS
    s_next, prev = scores(0), None
    for t in range(n_stages + 1):
        s = s_next
        if t + 1 < n_stages:
            s_next = scores(t + 1)
        if prev is not None:
            bi, hp = divmod(t - 1, HEAD_PAIRS)
            o = _pair_pv(*prev, gather((v0, v1, v2, v3), vc_ref, t - 1))
            o_ref[bi, :, :, lanes(hp)] = o.reshape(Q_ROWS, NA_KW, LANES).astype(BF16)
        if t < n_stages:
            prev = _pair_softmax(s, row_max)


def _key_group_start(rb):
    return jnp.clip(2 * rb - 1, 0, GRID_ROWS // KEY_ROW_GROUP - KEY_GROUPS)


def _attn_lat(q, k, v, kc, vc, bias, attn_layer, logit_bound):
    n_rb = GRID_ROWS // Q_ROWS
    nb = ATTN_BATCHES
    q4 = q.reshape(BATCH, GRID_ROWS, GRID_W, D_MODEL)
    k5 = k.reshape(BATCH, GRID_ROWS, NA_NCB, NA_KB, D_MODEL)
    v5 = v.reshape(BATCH, GRID_ROWS, NA_NCB, NA_KB, D_MODEL)
    kc3 = kc.reshape(BATCH, CTX_LEN, D_MODEL)
    vc3 = vc.reshape(BATCH, CTX_LEN, D_MODEL)
    q_spec = pl.BlockSpec((nb, Q_ROWS, NA_KW, D_MODEL), lambda rb, j, b: (b, rb, j, 0))

    def kv_spec(i):
        return pl.BlockSpec(
            (nb, KEY_ROW_GROUP, None, NA_KB, D_MODEL),
            lambda rb, j, b: (b, _key_group_start(rb) + i, j, 0, 0))

    ctx_spec = pl.BlockSpec((nb, CTX_LEN, D_MODEL), lambda rb, j, b: (b, 0, 0))

    def bias_index(rb, j, b):
        variant = (rb > 0).astype(jnp.int32) + (rb == n_rb - 1).astype(jnp.int32)
        return (variant * NA_NCB + j, attn_layer, 0, 0)

    bias_spec = pl.BlockSpec((None, NA_HEADS, Q_PATCH, LAT_KEYS), bias_index)
    shift = jnp.full((1, LANES), logit_bound, F32)

    def run(row_max):
        return pl.pallas_call(
            functools.partial(_attn_lat_kernel, row_max=row_max),
            grid=(n_rb, NA_NCB, BATCH // nb),
            in_specs=([q_spec] + [kv_spec(i) for i in range(KEY_GROUPS)] * 2
                      + [ctx_spec, ctx_spec, bias_spec, pl.BlockSpec((1, LANES), lambda rb, j, b: (0, 0))]),
            out_specs=q_spec,
            out_shape=jax.ShapeDtypeStruct(q4.shape, BF16),
            compiler_params=_params(3),
            name="attn_lat_rowmax" if row_max else "attn_lat",
        )(q4, k5, k5, k5, k5, v5, v5, v5, v5, kc3, vc3, bias, shift)

    out = lax.cond(logit_bound <= EXP2_SAFE_BOUND, lambda: run(False), lambda: run(True))
    return out.reshape(BATCH * SEQ, D_MODEL)


def _attn_ctx_kernel(q_ref, k_ref, v_ref, o_ref):
    for hp in range(HEAD_PAIRS):
        sl = slice(hp * LANES, (hp + 1) * LANES)
        p, denom = _pair_softmax(_pair_scores(q_ref[:, sl], k_ref[:, sl], None))
        o_ref[:, sl] = _pair_pv(p, denom, v_ref[:, sl]).astype(BF16)


def _attn_ctx(qc, kc, vc):
    spec = pl.BlockSpec((CTX_LEN, D_MODEL), lambda b: (b, 0))
    return pl.pallas_call(
        _attn_ctx_kernel,
        grid=(BATCH,),
        in_specs=[spec, spec, spec],
        out_specs=spec,
        out_shape=jax.ShapeDtypeStruct(qc.shape, BF16),
        compiler_params=_params(1),
        name="attn_ctx",
    )(qc, kc, vc)


def _attention_bias_tables(rpb):
    n_rb = GRID_ROWS // Q_ROWS
    n_heads = rpb.shape[0]
    col_pad, row_pad = NA_KW, Q_ROWS
    rpb_p = jnp.pad(rpb, ((0, 0), (row_pad, row_pad), (col_pad, col_pad)))
    by_col = []
    for j in range(NA_NCB):
        col_off = _key_col_start(j) - j * NA_KW + NA_KW - 1 + col_pad
        flat = jnp.stack([rpb_p[:, :, col_off - qi:col_off - qi + NA_KB]
                          for qi in range(NA_KW)], axis=1).reshape(n_heads, NA_KW, -1)
        by_col.append(jnp.pad(flat, ((0, 0), (0, 0), (0, BIAS_SRC_LANES - flat.shape[2]))))
    by_col = jnp.stack(by_col, axis=0)
    valid, row_offs = [], []
    for rb in (0, 1, n_rb - 1):
        g0 = int(np.clip(2 * rb - 1, 0, GRID_ROWS // KEY_ROW_GROUP - KEY_GROUPS))
        row_offs.append(g0 * KEY_ROW_GROUP - rb * Q_ROWS + NA_KH - 1 + row_pad)
        for j in range(NA_NCB):
            qrow = (rb * Q_ROWS + np.arange(Q_ROWS))[:, None, None, None]
            qcol = (j * NA_KW + np.arange(NA_KW))[None, :, None, None]
            krow = (g0 * KEY_ROW_GROUP + np.arange(KEY_ROWS))[None, None, :, None]
            kcol = (_key_col_start(j) + np.arange(NA_KB))[None, None, None, :]
            rs = np.clip(qrow - NA_KH // 2, 0, GRID_ROWS - NA_KH)
            ws = np.clip(qcol - NA_KW // 2, 0, GRID_W - NA_KW)
            ok = (krow >= rs) & (krow < rs + NA_KH) & (kcol >= ws) & (kcol < ws + NA_KW)
            valid.append(np.broadcast_to(ok, (Q_ROWS, NA_KW, KEY_ROWS, NA_KB)).reshape(Q_PATCH, LAT_KEYS))
    valid = jnp.asarray(np.stack(valid).reshape(3, NA_NCB, Q_PATCH, LAT_KEYS), F32)

    heads_per_step = 8
    out = pl.pallas_call(
        functools.partial(_bias_table_kernel, row_offs=tuple(row_offs)),
        grid=(NA_NCB, n_heads // heads_per_step),
        in_specs=[
            pl.BlockSpec((None, heads_per_step, NA_KW, BIAS_SRC_LANES), lambda j, hb: (j, hb, 0, 0)),
            pl.BlockSpec((3, None, Q_PATCH, LAT_KEYS), lambda j, hb: (0, j, 0, 0)),
        ],
        out_specs=pl.BlockSpec((3, None, heads_per_step, Q_PATCH, LAT_KEYS),
                               lambda j, hb: (0, j, hb, 0, 0)),
        out_shape=jax.ShapeDtypeStruct((3, NA_NCB, n_heads, Q_PATCH, LAT_KEYS), F32),
        compiler_params=_params(2),
        name="bias_table",
    )(by_col, valid)
    return out.reshape(3 * NA_NCB, n_heads, Q_PATCH, LAT_KEYS)


BIAS_SRC_LANES = 1024


def _bias_table_kernel(src_ref, valid_ref, o_ref, *, row_offs):
    for v, row_off in enumerate(row_offs):
        for ql in range(Q_ROWS):
            start = (row_off - ql) * NA_KB
            rows = slice(ql * NA_KW, (ql + 1) * NA_KW)
            ok = valid_ref[v, rows, :] > 0.0
            for h in range(src_ref.shape[0]):
                src = src_ref[h]
                if start:
                    src = pltpu.roll(src, BIAS_SRC_LANES - start, axis=1)
                o_ref[v, h, rows, :] = jnp.where(ok, src[:, :LAT_KEYS], NEG_INF)


def _mlp_kernel(*refs, with_proj):
    if with_proj:
        x_ref, a_ref, wo_ref, g1_ref, refs = refs[0], refs[1], refs[2], refs[3], refs[4:]
        x = x_ref[...] + g1_ref[...] * _dot(a_ref[...], wo_ref[...])
    else:
        x_ref, refs = refs[0], refs[1:]
        x = x_ref[...]
    n_ref, sh_ref, sc_ref, g2_ref, w1_ref, w2_ref, o_ref = refs
    h = _rms_mod(x, n_ref[...], sh_ref[...], sc_ref[...]).astype(BF16)
    t = jnp.maximum(_dot(h, w1_ref[...]), 0.0)
    t = (t * t).astype(BF16)
    o_ref[...] = x + g2_ref[...] * _dot(t, w2_ref[...])


def _mlp(x, mods, layer, norm_g, w1, w2, *, latent, tm, attn=None, w_o=None, mixer=None):
    n = x.shape[0]
    tiles_per_batch = (SEQ // tm) if latent else None
    tok = pl.BlockSpec((tm, D_MODEL), lambda t: (t, 0))
    with_proj = attn is not None
    in_specs, args = [tok], [x]
    if with_proj:
        in_specs += [tok, _resident((D_MODEL, D_MODEL), mixer), _mod_spec(layer, 2, tiles_per_batch)]
        args += [attn, w_o, mods]
    in_specs += [
        _vec_spec(D_MODEL, layer),
        _mod_spec(layer, 3, tiles_per_batch),
        _mod_spec(layer, 4, tiles_per_batch),
        _mod_spec(layer, 5, tiles_per_batch),
        _resident((D_MODEL, MLP_HIDDEN), layer),
        _resident((MLP_HIDDEN, D_MODEL), layer),
    ]
    args += [norm_g, mods, mods, mods, w1, w2]
    return pl.pallas_call(
        functools.partial(_mlp_kernel, with_proj=with_proj),
        grid=(n // tm,),
        in_specs=in_specs,
        out_specs=tok,
        out_shape=jax.ShapeDtypeStruct((n, D_MODEL), F32),
        compiler_params=_params(1),
        name=("proj_mlp" if with_proj else "mlp") + ("_lat" if latent else "_ctx"),
    )(*args)


SG_COLS = 1024


def _sg_kernel(x_ref, n_ref, sh_ref, sc_ref, g1_ref, win_ref, bin_ref, lng_ref, lnb_ref,
               ws_ref, bs_ref, wo_ref, o_ref, v_scr, t_scr):
    x = x_ref[...]
    tm = x.shape[0]
    n_chunks = tm // SG_CHUNK
    h = _rms_mod(x, n_ref[...], sh_ref[...], sc_ref[...]).astype(BF16)
    n_steps = SG_HALF // SG_COLS
    groups_per_step = SG_COLS // SG_GROUP_CH

    def cols(c):
        return slice(c * SG_COLS, (c + 1) * SG_COLS)

    def lane_tile_sum(a):
        return functools.reduce(lambda p, q: p + q,
                                [a[:, i:i + LANES] for i in range(0, a.shape[1], LANES)])

    s1 = s2 = None
    for c in range(n_steps):
        vc = slice(SG_HALF + c * SG_COLS, SG_HALF + (c + 1) * SG_COLS)
        v_blk = _gelu((_dot(h, win_ref[:, vc]) + bin_ref[:, vc]).astype(BF16))
        v_scr[:, cols(c)] = v_blk
        v_blk = v_blk.astype(F32)
        p1, p2 = lane_tile_sum(v_blk), lane_tile_sum(v_blk * v_blk)
        s1, s2 = (p1, p2) if s1 is None else (s1 + p1, s2 + p2)
    mu = jnp.sum(s1, axis=-1, keepdims=True) * (1.0 / SG_HALF)
    var = jnp.sum(s2, axis=-1, keepdims=True) * (1.0 / SG_HALF) - mu * mu
    rstd = lax.rsqrt(var + EPS)

    for c in range(n_steps):
        u = _gelu((_dot(h, win_ref[:, cols(c)]) + bin_ref[:, cols(c)]).astype(BF16))
        vn = ((v_scr[:, cols(c)].astype(F32) - mu) * rstd * lng_ref[:, cols(c)]
              + lnb_ref[:, cols(c)]).astype(BF16)
        for gg in range(groups_per_step):
            g = c * groups_per_step + gg
            gl = slice(gg * SG_GROUP_CH, (gg + 1) * SG_GROUP_CH)
            rhs = jnp.concatenate(
                [vn[i * SG_CHUNK:(i + 1) * SG_CHUNK, gl] for i in range(n_chunks)], axis=1)
            s = _dot(ws_ref[g], rhs)
            for i in range(n_chunks):
                rows = slice(i * SG_CHUNK, (i + 1) * SG_CHUNK)
                s_i = s[:, i * SG_GROUP_CH:(i + 1) * SG_GROUP_CH] + bs_ref[g]
                t_scr[rows, g * SG_GROUP_CH:(g + 1) * SG_GROUP_CH] = u[rows, gl] * s_i.astype(BF16)
    o_ref[...] = x + g1_ref[...] * _dot(t_scr[...], wo_ref[...])


def _sg(x, mods, layer, norm_g, mixer, w_in, b_in, ln_g, ln_b, w_s, b_s, w_o, *, latent, tm):
    n = x.shape[0]
    tiles_per_batch = (SEQ // tm) if latent else None
    tok = pl.BlockSpec((tm, D_MODEL), lambda t: (t, 0))
    return pl.pallas_call(
        _sg_kernel,
        grid=(n // tm,),
        in_specs=[
            tok,
            _vec_spec(D_MODEL, layer),
            _mod_spec(layer, 0, tiles_per_batch),
            _mod_spec(layer, 1, tiles_per_batch),
            _mod_spec(layer, 2, tiles_per_batch),
            _resident((D_MODEL, 2 * SG_HALF), mixer),
            _vec_spec(2 * SG_HALF, mixer),
            _vec_spec(SG_HALF, mixer),
            _vec_spec(SG_HALF, mixer),
            _resident((SG_GROUPS, SG_CHUNK, SG_CHUNK), mixer),
            _resident((SG_GROUPS, SG_CHUNK, SG_GROUP_CH), mixer),
            _resident((SG_HALF, D_MODEL), mixer),
        ],
        out_specs=tok,
        out_shape=jax.ShapeDtypeStruct((n, D_MODEL), F32),
        scratch_shapes=[pltpu.VMEM((tm, SG_HALF), BF16), pltpu.VMEM((tm, SG_HALF), BF16)],
        compiler_params=_params(1),
        name="sg_lat" if latent else "sg_ctx",
    )(x, norm_g, mods, mods, mods, w_in, b_in, ln_g, ln_b, w_s, b_s, w_o)


LAT_TM = 512
CTX_TM = 256


def kernel(x, c, ctx, c_ctx, ada_w, ada_b, norm1_g, norm2_g, mlp_w1, mlp_w2,
           na_w_qkv, na_q_norm, na_k_norm, na_rpb, na_w_o,
           sg_w_in, sg_b_in, sg_ln_g, sg_ln_b, sg_w_s, sg_b_s, sg_w_o):
    last_ctx_layer = ((DEPTH - 1) // N_MIXERS) * N_MIXERS
    xl = x.reshape(BATCH * SEQ, D_MODEL)
    xc = ctx.reshape(BATCH * CTX_LEN, D_MODEL)

    cond = jnp.concatenate(
        [c, c_ctx[None, :], jnp.zeros((MOD_ROWS - BATCH - 1, D_MODEL), F32)], axis=0)
    mods = _adaln(cond, ada_w, ada_b).reshape(DEPTH * MOD_ROWS * 6, 1, D_MODEL)

    q_gains = na_q_norm * (NA_HEAD_DIM ** -0.5 * LOG2_E)
    rpb2 = na_rpb * LOG2_E
    logit_bounds = (NA_HEAD_DIM * 1.01 * jnp.max(jnp.abs(q_gains), axis=1)
                    * jnp.max(jnp.abs(na_k_norm), axis=1) + jnp.max(jnp.abs(rpb2), axis=(1, 2, 3)))
    bias = _attention_bias_tables(
        (rpb2 - logit_bounds[:, None, None, None]).reshape((-1,) + na_rpb.shape[2:]))

    w1, w2 = mlp_w1.astype(BF16), mlp_w2.astype(BF16)
    w_qkv, w_ao = na_w_qkv.astype(BF16), na_w_o.astype(BF16)
    w_in, w_s, w_so = sg_w_in.astype(BF16), sg_w_s.astype(BF16), sg_w_o.astype(BF16)

    n1, n2 = norm1_g[:, None, :], norm2_g[:, None, :]
    q_gain = jnp.tile(q_gains, (1, NA_HEADS))[:, None, :]
    k_gain = jnp.tile(na_k_norm, (1, NA_HEADS))[:, None, :]
    sg_vecs = (sg_b_in[:, None, :], sg_ln_g[:, None, :], sg_ln_b[:, None, :])
    sg_bias = jnp.broadcast_to(sg_b_s[:, :, :, None], sg_b_s.shape + (SG_GROUP_CH,))

    for i in range(DEPTH):
        ctx_full = i < last_ctx_layer
        if i % N_MIXERS == 0:
            a = i // N_MIXERS
            q, k, v = _qkv(xl, mods, i, n1, w_qkv, a, q_gain, k_gain,
                           latent=True, need_q=True, tm=LAT_TM)
            ctx_proj = _qkv(xc, mods, i, n1, w_qkv, a, q_gain, k_gain,
                            latent=False, need_q=ctx_full, tm=CTX_TM)
            kc, vc = ctx_proj[-2], ctx_proj[-1]
            att = _attn_lat(q, k, v, kc, vc, bias, a, logit_bounds[a])
            xl = _mlp(xl, mods, i, n2, w1, w2, latent=True, tm=LAT_TM, attn=att, w_o=w_ao, mixer=a)
            if ctx_full:
                att_c = _attn_ctx(ctx_proj[0], kc, vc)
                xc = _mlp(xc, mods, i, n2, w1, w2, latent=False, tm=CTX_TM, attn=att_c, w_o=w_ao,
                          mixer=a)
        else:
            s = i // N_MIXERS
            sg_args = (s, w_in, *sg_vecs, w_s, sg_bias, w_so)
            xl = _sg(xl, mods, i, n1, *sg_args, latent=True, tm=LAT_TM)
            xl = _mlp(xl, mods, i, n2, w1, w2, latent=True, tm=LAT_TM)
            if ctx_full:
                xc = _sg(xc, mods, i, n1, *sg_args, latent=False, tm=CTX_TM)
                xc = _mlp(xc, mods, i, n2, w1, w2, latent=False, tm=CTX_TM)
    return xl.reshape(BATCH, SEQ, D_MODEL)
```

```python
import functools

import numpy as np
import jax
import jax.numpy as jnp
from jax import lax
from jax.experimental import pallas as pl
from jax.experimental.pallas import tpu as pltpu

D_MODEL = 1024
BATCH = 8
SEQ = 4096
DEPTH = 4
GRID_W = 64
GRID_ROWS = SEQ // GRID_W
CTX_LEN = 256
N_MIXERS = 2
NA_HEADS = 16
NA_HEAD_DIM = D_MODEL // NA_HEADS
NA_KH = 8
NA_KW = 16
NA_KB = 2 * NA_KW
NA_NCB = GRID_W // NA_KW
SG_CHUNK = 128
SG_HALF = 3 * D_MODEL
SG_GROUP_CH = 128
SG_GROUPS = SG_HALF // SG_GROUP_CH
MLP_HIDDEN = 4 * D_MODEL
EPS = 1e-6
NEG_INF = -1e30
LOG2_E = float(np.log2(np.e))
EXP2_SAFE_BOUND = 60.0

F32 = jnp.float32
BF16 = jnp.bfloat16

LANES = 128
HEAD_PAIRS = NA_HEADS // 2
MOD_ROWS = 16
CTX_MOD_ROW = BATCH
Q_ROWS = 8
Q_PATCH = Q_ROWS * NA_KW
KEY_ROW_GROUP = 4
KEY_GROUPS = 4
KEY_ROWS = KEY_ROW_GROUP * KEY_GROUPS
LAT_KEYS = KEY_ROWS * NA_KB
ATTN_BATCHES = 4
VMEM_LIMIT = 56 * 1024 * 1024


def _dot(a, b):
    return jnp.dot(a, b, preferred_element_type=F32)


def _dot_nt(a, b):
    return lax.dot_general(a, b, (((1,), (1,)), ((), ())), preferred_element_type=F32)


def _rms_mod(x, g, shift, scale):
    ms = jnp.mean(x * x, axis=-1, keepdims=True)
    h = x * lax.rsqrt(ms + EPS) * g
    return h * (1.0 + scale) + shift


def _gelu(z):
    half = np.asarray(0.5, z.dtype)
    return half * z * (np.asarray(1.0, z.dtype) + lax.erf(z * np.asarray(np.sqrt(0.5), z.dtype)))


def _split_bf16(a):
    hi = a.astype(BF16)
    lo = (a - hi.astype(F32)).astype(BF16)
    return hi, lo


def _params(n_axes):
    return pltpu.CompilerParams(dimension_semantics=("arbitrary",) * n_axes,
                                vmem_limit_bytes=VMEM_LIMIT)


def _resident(shape, stack_index=None):
    nd = len(shape)
    if stack_index is None:
        return pl.BlockSpec(shape, lambda *_: (0,) * nd, pipeline_mode=pl.Buffered(1))
    return pl.BlockSpec((None,) + tuple(shape), lambda *_: (stack_index,) + (0,) * nd,
                        pipeline_mode=pl.Buffered(1))


def _adaln_kernel(c_ref, w_ref, b_ref, o_ref):
    c = c_ref[...]
    s = c * (1.0 / (1.0 + jnp.exp(-c)))
    s_hi, s_lo = _split_bf16(s)
    w_hi, w_lo = _split_bf16(w_ref[...])
    o_ref[...] = _dot(s_hi, w_hi) + (_dot(s_lo, w_hi) + _dot(s_hi, w_lo)) + b_ref[...]


def _adaln(cond, ada_w, ada_b):
    tn = D_MODEL
    return pl.pallas_call(
        _adaln_kernel,
        grid=(DEPTH, 6 * D_MODEL // tn),
        in_specs=[
            pl.BlockSpec((MOD_ROWS, D_MODEL), lambda i, n: (0, 0)),
            pl.BlockSpec((None, D_MODEL, tn), lambda i, n: (i, 0, n)),
            pl.BlockSpec((None, 1, tn), lambda i, n: (i, 0, n)),
        ],
        out_specs=pl.BlockSpec((None, MOD_ROWS, tn), lambda i, n: (i, 0, n)),
        out_shape=jax.ShapeDtypeStruct((DEPTH, MOD_ROWS, 6 * D_MODEL), F32),
        compiler_params=_params(2),
        name="adaln",
    )(cond, ada_w, ada_b.reshape(DEPTH, 1, 6 * D_MODEL))


def _mod_spec(layer, k, tiles_per_batch):
    base = layer * MOD_ROWS * 6 + k
    if tiles_per_batch is None:
        return pl.BlockSpec((None, 1, D_MODEL), lambda t: (base + CTX_MOD_ROW * 6, 0, 0))
    return pl.BlockSpec((None, 1, D_MODEL), lambda t: (base + (t // tiles_per_batch) * 6, 0, 0))


def _vec_spec(n, row):
    return pl.BlockSpec((None, 1, n), lambda t: (row, 0, 0))


def _key_col_start(j):
    return int(np.clip(j * NA_KW - NA_KW // 2, 0, GRID_W - NA_KB))


def _qkv_kernel(x_ref, g_ref, sh_ref, sc_ref, w_ref, qg_ref, kg_ref, *out_refs, need_q, gather):
    h = _rms_mod(x_ref[...], g_ref[...], sh_ref[...], sc_ref[...]).astype(BF16)
    first = lax.broadcasted_iota(jnp.int32, (1, LANES), 1) < NA_HEAD_DIM

    def head_norm(y, gain):
        parts = []
        for c in range(0, D_MODEL, LANES):
            t = y[:, c:c + LANES]
            sq = t * t
            s_first = jnp.sum(jnp.where(first, sq, 0.0), axis=-1, keepdims=True)
            s_both = jnp.sum(sq, axis=-1, keepdims=True)
            ms = jnp.where(first, s_first, s_both - s_first) * (1.0 / NA_HEAD_DIM)
            parts.append(t * lax.rsqrt(ms + EPS))
        return jnp.concatenate(parts, axis=1) * gain

    refs = list(out_refs)
    if need_q:
        refs.pop(0)[...] = head_norm(_dot(h, w_ref[:, 0:D_MODEL]), qg_ref[...]).astype(BF16)
    k_ref, v_ref = refs
    k = head_norm(_dot(h, w_ref[:, D_MODEL:2 * D_MODEL]), kg_ref[...])
    v = _dot(h, w_ref[:, 2 * D_MODEL:3 * D_MODEL])
    if not gather:
        k_ref[...] = k.astype(BF16)
        v_ref[...] = v.astype(BF16)
        return
    rows = k.shape[0] // GRID_W
    k3 = k.reshape(rows, GRID_W, D_MODEL)
    v3 = v.reshape(rows, GRID_W, D_MODEL)
    for j in range(NA_NCB):
        c0 = _key_col_start(j)
        k_ref[:, j, :, :] = k3[:, c0:c0 + NA_KB, :].astype(BF16)
        v_ref[:, j, :, :] = v3[:, c0:c0 + NA_KB, :].astype(BF16)


def _qkv(x, mods, layer, norm_g, w_qkv, mixer, q_gain, k_gain, *, latent, need_q, tm):
    n = x.shape[0]
    tiles_per_batch = (SEQ // tm) if latent else None
    tok = pl.BlockSpec((tm, D_MODEL), lambda t: (t, 0))
    out_specs, out_shapes = [], []
    if need_q:
        out_specs.append(tok)
        out_shapes.append(jax.ShapeDtypeStruct((n, D_MODEL), BF16))
    if latent:
        rows = tm // GRID_W
        kv_spec = pl.BlockSpec((rows, NA_NCB, NA_KB, D_MODEL), lambda t: (t, 0, 0, 0))
        kv_shape = jax.ShapeDtypeStruct((n // GRID_W, NA_NCB, NA_KB, D_MODEL), BF16)
    else:
        kv_spec, kv_shape = tok, jax.ShapeDtypeStruct((n, D_MODEL), BF16)
    out_specs += [kv_spec, kv_spec]
    out_shapes += [kv_shape, kv_shape]
    return pl.pallas_call(
        functools.partial(_qkv_kernel, need_q=need_q, gather=latent),
        grid=(n // tm,),
        in_specs=[
            tok,
            _vec_spec(D_MODEL, layer),
            _mod_spec(layer, 0, tiles_per_batch),
            _mod_spec(layer, 1, tiles_per_batch),
            _resident((D_MODEL, 3 * D_MODEL), mixer),
            _vec_spec(D_MODEL, mixer),
            _vec_spec(D_MODEL, mixer),
        ],
        out_specs=out_specs,
        out_shape=out_shapes,
        compiler_params=_params(1),
        name="qkv_lat" if latent else "qkv_ctx",
    )(x, norm_g, mods, mods, w_qkv, q_gain, k_gain)


def _pair_scores(q2, keys, bias_pair=None, shift=None):
    lane = lax.broadcasted_iota(jnp.int32, (1, LANES), 1)
    first = lane < NA_HEAD_DIM
    zero = jnp.zeros_like(q2)
    q4 = jnp.concatenate([jnp.where(first, q2, zero), jnp.where(first, zero, q2)], axis=0)
    s = _dot_nt(q4, keys)
    if bias_pair is None:
        return s
    nb = bias_pair[0].shape[1]
    rest = [s[:, i:i + LANES] - shift for i in range(nb, s.shape[1], LANES)]
    return jnp.concatenate([s[:, :nb] + jnp.concatenate(bias_pair, axis=0)] + rest, axis=1)


def _pair_softmax(s, row_max=True):
    tiles = [s[:, i:i + LANES] for i in range(0, s.shape[1], LANES)]
    if row_max:
        mx = jnp.max(functools.reduce(jnp.maximum, tiles), axis=-1, keepdims=True)
        tiles = [t - mx for t in tiles]
    probs = [jnp.exp2(t) for t in tiles]
    denom = jnp.sum(functools.reduce(lambda a, b: a + b, probs), axis=-1, keepdims=True)
    return jnp.concatenate([p.astype(BF16) for p in probs], axis=1), denom


def _pair_pv(p, denom, values):
    m_rows = p.shape[0] // 2
    o = _dot(p, values) / denom
    lane = lax.broadcasted_iota(jnp.int32, (1, LANES), 1)
    return jnp.where(lane < NA_HEAD_DIM, o[:m_rows], o[m_rows:])


def _attn_lat_kernel(q_ref, k0, k1, k2, k3, v0, v1, v2, v3, kc_ref, vc_ref, bias_ref, shift_ref,
                     o_ref, *, row_max):
    kv_rows = KEY_ROW_GROUP * NA_KB

    def lanes(hp):
        return slice(hp * LANES, (hp + 1) * LANES)

    def gather(refs, ctx_ref, t):
        bi, hp = divmod(t, HEAD_PAIRS)
        return jnp.concatenate(
            [r[bi, :, :, lanes(hp)].reshape(kv_rows, LANES) for r in refs] + [ctx_ref[bi, :, lanes(hp)]],
            axis=0)

    def scores(t):
        bi, hp = divmod(t, HEAD_PAIRS)
        q2 = q_ref[bi, :, :, lanes(hp)].reshape(Q_PATCH, LANES)
        return _pair_scores(q2, gather((k0, k1, k2, k3), kc_ref, t),
                            (bias_ref[2 * hp], bias_ref[2 * hp + 1]), shift_ref[...])

    n_stages = q_ref.shape[0] * HEAD_PAIRS
    s_next, prev = scores(0), None
    for t in range(n_stages + 1):
        s = s_next
        if t + 1 < n_stages:
            s_next = scores(t + 1)
        if prev is not None:
            bi, hp = divmod(t - 1, HEAD_PAIRS)
            o = _pair_pv(*prev, gather((v0, v1, v2, v3), vc_ref, t - 1))
            o_ref[bi, :, :, lanes(hp)] = o.reshape(Q_ROWS, NA_KW, LANES).astype(BF16)
        if t < n_stages:
            prev = _pair_softmax(s, row_max)


def _key_group_start(rb):
    return jnp.clip(2 * rb - 1, 0, GRID_ROWS // KEY_ROW_GROUP - KEY_GROUPS)


def _attn_lat(q, k, v, kc, vc, bias, attn_layer, logit_bound):
    n_rb = GRID_ROWS // Q_ROWS
    nb = ATTN_BATCHES
    q4 = q.reshape(BATCH, GRID_ROWS, GRID_W, D_MODEL)
    k5 = k.reshape(BATCH, GRID_ROWS, NA_NCB, NA_KB, D_MODEL)
    v5 = v.reshape(BATCH, GRID_ROWS, NA_NCB, NA_KB, D_MODEL)
    kc3 = kc.reshape(BATCH, CTX_LEN, D_MODEL)
    vc3 = vc.reshape(BATCH, CTX_LEN, D_MODEL)
    q_spec = pl.BlockSpec((nb, Q_ROWS, NA_KW, D_MODEL), lambda rb, j, b: (b, rb, j, 0))

    def kv_spec(i):
        return pl.BlockSpec(
            (nb, KEY_ROW_GROUP, None, NA_KB, D_MODEL),
            lambda rb, j, b: (b, _key_group_start(rb) + i, j, 0, 0))

    ctx_spec = pl.BlockSpec((nb, CTX_LEN, D_MODEL), lambda rb, j, b: (b, 0, 0))

    def bias_index(rb, j, b):
        variant = (rb > 0).astype(jnp.int32) + (rb == n_rb - 1).astype(jnp.int32)
        return (variant * NA_NCB + j, attn_layer, 0, 0)

    bias_spec = pl.BlockSpec((None, NA_HEADS, Q_PATCH, LAT_KEYS), bias_index)
    shift = jnp.full((1, LANES), logit_bound, F32)

    def run(row_max):
        return pl.pallas_call(
            functools.partial(_attn_lat_kernel, row_max=row_max),
            grid=(n_rb, NA_NCB, BATCH // nb),
            in_specs=([q_spec] + [kv_spec(i) for i in range(KEY_GROUPS)] * 2
                      + [ctx_spec, ctx_spec, bias_spec, pl.BlockSpec((1, LANES), lambda rb, j, b: (0, 0))]),
            out_specs=q_spec,
            out_shape=jax.ShapeDtypeStruct(q4.shape, BF16),
            compiler_params=_params(3),
            name="attn_lat_rowmax" if row_max else "attn_lat",
        )(q4, k5, k5, k5, k5, v5, v5, v5, v5, kc3, vc3, bias, shift)

    out = lax.cond(logit_bound <= EXP2_SAFE_BOUND, lambda: run(False), lambda: run(True))
    return out.reshape(BATCH * SEQ, D_MODEL)


def _attn_ctx_kernel(q_ref, k_ref, v_ref, o_ref):
    for hp in range(HEAD_PAIRS):
        sl = slice(hp * LANES, (hp + 1) * LANES)
        p, denom = _pair_softmax(_pair_scores(q_ref[:, sl], k_ref[:, sl], None))
        o_ref[:, sl] = _pair_pv(p, denom, v_ref[:, sl]).astype(BF16)


def _attn_ctx(qc, kc, vc):
    spec = pl.BlockSpec((CTX_LEN, D_MODEL), lambda b: (b, 0))
    return pl.pallas_call(
        _attn_ctx_kernel,
        grid=(BATCH,),
        in_specs=[spec, spec, spec],
        out_specs=spec,
        out_shape=jax.ShapeDtypeStruct(qc.shape, BF16),
        compiler_params=_params(1),
        name="attn_ctx",
    )(qc, kc, vc)


def _attention_bias_tables(rpb):
    n_rb = GRID_ROWS // Q_ROWS
    n_heads = rpb.shape[0]
    col_pad, row_pad = NA_KW, Q_ROWS
    rpb_p = jnp.pad(rpb, ((0, 0), (row_pad, row_pad), (col_pad, col_pad)))
    by_col = []
    for j in range(NA_NCB):
        col_off = _key_col_start(j) - j * NA_KW + NA_KW - 1 + col_pad
        flat = jnp.stack([rpb_p[:, :, col_off - qi:col_off - qi + NA_KB]
                          for qi in range(NA_KW)], axis=1).reshape(n_heads, NA_KW, -1)
        by_col.append(jnp.pad(flat, ((0, 0), (0, 0), (0, BIAS_SRC_LANES - flat.shape[2]))))
    by_col = jnp.stack(by_col, axis=0)
    valid, row_offs = [], []
    for rb in (0, 1, n_rb - 1):
        g0 = int(np.clip(2 * rb - 1, 0, GRID_ROWS // KEY_ROW_GROUP - KEY_GROUPS))
        row_offs.append(g0 * KEY_ROW_GROUP - rb * Q_ROWS + NA_KH - 1 + row_pad)
        for j in range(NA_NCB):
            qrow = (rb * Q_ROWS + np.arange(Q_ROWS))[:, None, None, None]
            qcol = (j * NA_KW + np.arange(NA_KW))[None, :, None, None]
            krow = (g0 * KEY_ROW_GROUP + np.arange(KEY_ROWS))[None, None, :, None]
            kcol = (_key_col_start(j) + np.arange(NA_KB))[None, None, None, :]
            rs = np.clip(qrow - NA_KH // 2, 0, GRID_ROWS - NA_KH)
            ws = np.clip(qcol - NA_KW // 2, 0, GRID_W - NA_KW)
            ok = (krow >= rs) & (krow < rs + NA_KH) & (kcol >= ws) & (kcol < ws + NA_KW)
            valid.append(np.broadcast_to(ok, (Q_ROWS, NA_KW, KEY_ROWS, NA_KB)).reshape(Q_PATCH, LAT_KEYS))
    valid = jnp.asarray(np.stack(valid).reshape(3, NA_NCB, Q_PATCH, LAT_KEYS), F32)

    heads_per_step = 8
    out = pl.pallas_call(
        functools.partial(_bias_table_kernel, row_offs=tuple(row_offs)),
        grid=(NA_NCB, n_heads // heads_per_step),
        in_specs=[
            pl.BlockSpec((None, heads_per_step, NA_KW, BIAS_SRC_LANES), lambda j, hb: (j, hb, 0, 0)),
            pl.BlockSpec((3, None, Q_PATCH, LAT_KEYS), lambda j, hb: (0, j, 0, 0)),
        ],
        out_specs=pl.BlockSpec((3, None, heads_per_step, Q_PATCH, LAT_KEYS),
                               lambda j, hb: (0, j, hb, 0, 0)),
        out_shape=jax.ShapeDtypeStruct((3, NA_NCB, n_heads, Q_PATCH, LAT_KEYS), F32),
        compiler_params=_params(2),
        name="bias_table",
    )(by_col, valid)
    return out.reshape(3 * NA_NCB, n_heads, Q_PATCH, LAT_KEYS)


BIAS_SRC_LANES = 1024


def _bias_table_kernel(src_ref, valid_ref, o_ref, *, row_offs):
    for v, row_off in enumerate(row_offs):
        for ql in range(Q_ROWS):
            start = (row_off - ql) * NA_KB
            rows = slice(ql * NA_KW, (ql + 1) * NA_KW)
            ok = valid_ref[v, rows, :] > 0.0
            for h in range(src_ref.shape[0]):
                src = src_ref[h]
                if start:
                    src = pltpu.roll(src, BIAS_SRC_LANES - start, axis=1)
                o_ref[v, h, rows, :] = jnp.where(ok, src[:, :LAT_KEYS], NEG_INF)


def _mlp_kernel(*refs, with_proj):
    if with_proj:
        x_ref, a_ref, wo_ref, g1_ref, refs = refs[0], refs[1], refs[2], refs[3], refs[4:]
        x = x_ref[...] + g1_ref[...] * _dot(a_ref[...], wo_ref[...])
    else:
        x_ref, refs = refs[0], refs[1:]
        x = x_ref[...]
    n_ref, sh_ref, sc_ref, g2_ref, w1_ref, w2_ref, o_ref = refs
    h = _rms_mod(x, n_ref[...], sh_ref[...], sc_ref[...]).astype(BF16)
    t = jnp.maximum(_dot(h, w1_ref[...]), 0.0)
    t = (t * t).astype(BF16)
    o_ref[...] = x + g2_ref[...] * _dot(t, w2_ref[...])


def _mlp(x, mods, layer, norm_g, w1, w2, *, latent, tm, attn=None, w_o=None, mixer=None):
    n = x.shape[0]
    tiles_per_batch = (SEQ // tm) if latent else None
    tok = pl.BlockSpec((tm, D_MODEL), lambda t: (t, 0))
    with_proj = attn is not None
    in_specs, args = [tok], [x]
    if with_proj:
        in_specs += [tok, _resident((D_MODEL, D_MODEL), mixer), _mod_spec(layer, 2, tiles_per_batch)]
        args += [attn, w_o, mods]
    in_specs += [
        _vec_spec(D_MODEL, layer),
        _mod_spec(layer, 3, tiles_per_batch),
        _mod_spec(layer, 4, tiles_per_batch),
        _mod_spec(layer, 5, tiles_per_batch),
        _resident((D_MODEL, MLP_HIDDEN), layer),
        _resident((MLP_HIDDEN, D_MODEL), layer),
    ]
    args += [norm_g, mods, mods, mods, w1, w2]
    return pl.pallas_call(
        functools.partial(_mlp_kernel, with_proj=with_proj),
        grid=(n // tm,),
        in_specs=in_specs,
        out_specs=tok,
        out_shape=jax.ShapeDtypeStruct((n, D_MODEL), F32),
        compiler_params=_params(1),
        name=("proj_mlp" if with_proj else "mlp") + ("_lat" if latent else "_ctx"),
    )(*args)


SG_COLS = 1024


def _sg_kernel(x_ref, n_ref, sh_ref, sc_ref, g1_ref, win_ref, bin_ref, lng_ref, lnb_ref,
               ws_ref, bs_ref, wo_ref, o_ref, v_scr, t_scr):
    x = x_ref[...]
    tm = x.shape[0]
    n_chunks = tm // SG_CHUNK
    h = _rms_mod(x, n_ref[...], sh_ref[...], sc_ref[...]).astype(BF16)
    n_steps = SG_HALF // SG_COLS
    groups_per_step = SG_COLS // SG_GROUP_CH

    def cols(c):
        return slice(c * SG_COLS, (c + 1) * SG_COLS)

    def lane_tile_sum(a):
        return functools.reduce(lambda p, q: p + q,
                                [a[:, i:i + LANES] for i in range(0, a.shape[1], LANES)])

    s1 = s2 = None
    for c in range(n_steps):
        vc = slice(SG_HALF + c * SG_COLS, SG_HALF + (c + 1) * SG_COLS)
        v_blk = _gelu(_dot(h, win_ref[:, vc]) + bin_ref[:, vc])
        v_scr[:, cols(c)] = v_blk
        p1, p2 = lane_tile_sum(v_blk), lane_tile_sum(v_blk * v_blk)
        s1, s2 = (p1, p2) if s1 is None else (s1 + p1, s2 + p2)
    mu = jnp.sum(s1, axis=-1, keepdims=True) * (1.0 / SG_HALF)
    var = jnp.sum(s2, axis=-1, keepdims=True) * (1.0 / SG_HALF) - mu * mu
    rstd = lax.rsqrt(var + EPS)

    for c in range(n_steps):
        u = _gelu((_dot(h, win_ref[:, cols(c)]) + bin_ref[:, cols(c)]).astype(BF16))
        vn = ((v_scr[:, cols(c)] - mu) * rstd * lng_ref[:, cols(c)] + lnb_ref[:, cols(c)]).astype(BF16)
        for gg in range(groups_per_step):
            g = c * groups_per_step + gg
            gl = slice(gg * SG_GROUP_CH, (gg + 1) * SG_GROUP_CH)
            rhs = jnp.concatenate(
                [vn[i * SG_CHUNK:(i + 1) * SG_CHUNK, gl] for i in range(n_chunks)], axis=1)
            s = _dot(ws_ref[g], rhs)
            for i in range(n_chunks):
                rows = slice(i * SG_CHUNK, (i + 1) * SG_CHUNK)
                s_i = s[:, i * SG_GROUP_CH:(i + 1) * SG_GROUP_CH] + bs_ref[g]
                t_scr[rows, g * SG_GROUP_CH:(g + 1) * SG_GROUP_CH] = u[rows, gl] * s_i.astype(BF16)
    o_ref[...] = x + g1_ref[...] * _dot(t_scr[...], wo_ref[...])


def _sg(x, mods, layer, norm_g, mixer, w_in, b_in, ln_g, ln_b, w_s, b_s, w_o, *, latent, tm):
    n = x.shape[0]
    tiles_per_batch = (SEQ // tm) if latent else None
    tok = pl.BlockSpec((tm, D_MODEL), lambda t: (t, 0))
    return pl.pallas_call(
        _sg_kernel,
        grid=(n // tm,),
        in_specs=[
            tok,
            _vec_spec(D_MODEL, layer),
            _mod_spec(layer, 0, tiles_per_batch),
            _mod_spec(layer, 1, tiles_per_batch),
            _mod_spec(layer, 2, tiles_per_batch),
            _resident((D_MODEL, 2 * SG_HALF), mixer),
            _vec_spec(2 * SG_HALF, mixer),
            _vec_spec(SG_HALF, mixer),
            _vec_spec(SG_HALF, mixer),
            _resident((SG_GROUPS, SG_CHUNK, SG_CHUNK), mixer),
            _resident((SG_GROUPS, SG_CHUNK, SG_GROUP_CH), mixer),
            _resident((SG_HALF, D_MODEL), mixer),
        ],
        out_specs=tok,
        out_shape=jax.ShapeDtypeStruct((n, D_MODEL), F32),
        scratch_shapes=[pltpu.VMEM((tm, SG_HALF), F32), pltpu.VMEM((tm, SG_HALF), BF16)],
        compiler_params=_params(1),
        name="sg_lat" if latent else "sg_ctx",
    )(x, norm_g, mods, mods, mods, w_in, b_in, ln_g, ln_b, w_s, b_s, w_o)


LAT_TM = 1024
SG_TM = 512
CTX_TM = 256


def kernel(x, c, ctx, c_ctx, ada_w, ada_b, norm1_g, norm2_g, mlp_w1, mlp_w2,
           na_w_qkv, na_q_norm, na_k_norm, na_rpb, na_w_o,
           sg_w_in, sg_b_in, sg_ln_g, sg_ln_b, sg_w_s, sg_b_s, sg_w_o):
    last_ctx_layer = ((DEPTH - 1) // N_MIXERS) * N_MIXERS
    xl = x.reshape(BATCH * SEQ, D_MODEL)
    xc = ctx.reshape(BATCH * CTX_LEN, D_MODEL)

    cond = jnp.concatenate(
        [c, c_ctx[None, :], jnp.zeros((MOD_ROWS - BATCH - 1, D_MODEL), F32)], axis=0)
    mods = _adaln(cond, ada_w, ada_b).reshape(DEPTH * MOD_ROWS * 6, 1, D_MODEL)

    q_gains = na_q_norm * (NA_HEAD_DIM ** -0.5 * LOG2_E)
    rpb2 = na_rpb * LOG2_E
    logit_bounds = (NA_HEAD_DIM * 1.01 * jnp.max(jnp.abs(q_gains), axis=1)
                    * jnp.max(jnp.abs(na_k_norm), axis=1) + jnp.max(jnp.abs(rpb2), axis=(1, 2, 3)))
    bias = _attention_bias_tables(
        (rpb2 - logit_bounds[:, None, None, None]).reshape((-1,) + na_rpb.shape[2:]))

    w1, w2 = mlp_w1.astype(BF16), mlp_w2.astype(BF16)
    w_qkv, w_ao = na_w_qkv.astype(BF16), na_w_o.astype(BF16)
    w_in, w_s, w_so = sg_w_in.astype(BF16), sg_w_s.astype(BF16), sg_w_o.astype(BF16)

    n1, n2 = norm1_g[:, None, :], norm2_g[:, None, :]
    q_gain = jnp.tile(q_gains, (1, NA_HEADS))[:, None, :]
    k_gain = jnp.tile(na_k_norm, (1, NA_HEADS))[:, None, :]
    sg_vecs = (sg_b_in[:, None, :], sg_ln_g[:, None, :], sg_ln_b[:, None, :])
    sg_bias = jnp.broadcast_to(sg_b_s[:, :, :, None], sg_b_s.shape + (SG_GROUP_CH,))

    for i in range(DEPTH):
        ctx_full = i < last_ctx_layer
        if i % N_MIXERS == 0:
            a = i // N_MIXERS
            q, k, v = _qkv(xl, mods, i, n1, w_qkv, a, q_gain, k_gain,
                           latent=True, need_q=True, tm=LAT_TM)
            ctx_proj = _qkv(xc, mods, i, n1, w_qkv, a, q_gain, k_gain,
                            latent=False, need_q=ctx_full, tm=CTX_TM)
            kc, vc = ctx_proj[-2], ctx_proj[-1]
            att = _attn_lat(q, k, v, kc, vc, bias, a, logit_bounds[a])
            xl = _mlp(xl, mods, i, n2, w1, w2, latent=True, tm=LAT_TM, attn=att, w_o=w_ao, mixer=a)
            if ctx_full:
                att_c = _attn_ctx(ctx_proj[0], kc, vc)
                xc = _mlp(xc, mods, i, n2, w1, w2, latent=False, tm=CTX_TM, attn=att_c, w_o=w_ao,
                          mixer=a)
        else:
            s = i // N_MIXERS
            sg_args = (s, w_in, *sg_vecs, w_s, sg_bias, w_so)
            xl = _sg(xl, mods, i, n1, *sg_args, latent=True, tm=SG_TM)
            xl = _mlp(xl, mods, i, n2, w1, w2, latent=True, tm=LAT_TM)
            if ctx_full:
                xc = _sg(xc, mods, i, n1, *sg_args, latent=False, tm=CTX_TM)
                xc = _mlp(xc, mods, i, n2, w1, w2, latent=False, tm=CTX_TM)
    return xl.reshape(BATCH, SEQ, D_MODEL)
```

```python
import functools

import numpy as np
import jax
import jax.numpy as jnp
from jax import lax
from jax.experimental import pallas as pl
from jax.experimental.pallas import tpu as pltpu

D_MODEL = 1024
BATCH = 8
SEQ = 4096
DEPTH = 4
GRID_W = 64
GRID_ROWS = SEQ // GRID_W
CTX_LEN = 256
N_MIXERS = 2
NA_HEADS = 16
NA_HEAD_DIM = D_MODEL // NA_HEADS
NA_KH = 8
NA_KW = 16
NA_KB = 2 * NA_KW
NA_NCB = GRID_W // NA_KW
SG_CHUNK = 128
SG_HALF = 3 * D_MODEL
SG_GROUP_CH = 128
SG_GROUPS = SG_HALF // SG_GROUP_CH
MLP_HIDDEN = 4 * D_MODEL
EPS = 1e-6
NEG_INF = -1e30
LOG2_E = float(np.log2(np.e))
EXP2_SAFE_BOUND = 60.0

F32 = jnp.float32
BF16 = jnp.bfloat16

LANES = 128
HEAD_PAIRS = NA_HEADS // 2
MOD_ROWS = 16
CTX_MOD_ROW = BATCH
Q_ROWS = 8
Q_PATCH = Q_ROWS * NA_KW
KEY_ROW_GROUP = 4
KEY_GROUPS = 4
KEY_ROWS = KEY_ROW_GROUP * KEY_GROUPS
LAT_KEYS = KEY_ROWS * NA_KB
ATTN_BATCHES = 4
VMEM_LIMIT = 56 * 1024 * 1024


def _dot(a, b):
    return jnp.dot(a, b, preferred_element_type=F32)


def _dot_nt(a, b):
    return lax.dot_general(a, b, (((1,), (1,)), ((), ())), preferred_element_type=F32)


def _rms_mod(x, g, shift, scale):
    ms = jnp.mean(x * x, axis=-1, keepdims=True)
    h = x * lax.rsqrt(ms + EPS) * g
    return h * (1.0 + scale) + shift


def _gelu(z):
    half = np.asarray(0.5, z.dtype)
    return half * z * (np.asarray(1.0, z.dtype) + lax.erf(z * np.asarray(np.sqrt(0.5), z.dtype)))


def _split_bf16(a):
    hi = a.astype(BF16)
    lo = (a - hi.astype(F32)).astype(BF16)
    return hi, lo


def _params(n_axes):
    return pltpu.CompilerParams(dimension_semantics=("arbitrary",) * n_axes,
                                vmem_limit_bytes=VMEM_LIMIT)


def _resident(shape, stack_index=None):
    nd = len(shape)
    if stack_index is None:
        return pl.BlockSpec(shape, lambda *_: (0,) * nd, pipeline_mode=pl.Buffered(1))
    return pl.BlockSpec((None,) + tuple(shape), lambda *_: (stack_index,) + (0,) * nd,
                        pipeline_mode=pl.Buffered(1))


def _adaln_kernel(c_ref, w_ref, b_ref, o_ref):
    c = c_ref[...]
    s = c * (1.0 / (1.0 + jnp.exp(-c)))
    s_hi, s_lo = _split_bf16(s)
    w_hi, w_lo = _split_bf16(w_ref[...])
    o_ref[...] = _dot(s_hi, w_hi) + (_dot(s_lo, w_hi) + _dot(s_hi, w_lo)) + b_ref[...]


def _adaln(cond, ada_w, ada_b):
    tn = D_MODEL
    return pl.pallas_call(
        _adaln_kernel,
        grid=(DEPTH, 6 * D_MODEL // tn),
        in_specs=[
            pl.BlockSpec((MOD_ROWS, D_MODEL), lambda i, n: (0, 0)),
            pl.BlockSpec((None, D_MODEL, tn), lambda i, n: (i, 0, n)),
            pl.BlockSpec((None, 1, tn), lambda i, n: (i, 0, n)),
        ],
        out_specs=pl.BlockSpec((None, MOD_ROWS, tn), lambda i, n: (i, 0, n)),
        out_shape=jax.ShapeDtypeStruct((DEPTH, MOD_ROWS, 6 * D_MODEL), F32),
        compiler_params=_params(2),
        name="adaln",
    )(cond, ada_w, ada_b.reshape(DEPTH, 1, 6 * D_MODEL))


def _mod_spec(layer, k, tiles_per_batch):
    base = layer * MOD_ROWS * 6 + k
    if tiles_per_batch is None:
        return pl.BlockSpec((None, 1, D_MODEL), lambda t: (base + CTX_MOD_ROW * 6, 0, 0))
    return pl.BlockSpec((None, 1, D_MODEL), lambda t: (base + (t // tiles_per_batch) * 6, 0, 0))


def _vec_spec(n, row):
    return pl.BlockSpec((None, 1, n), lambda t: (row, 0, 0))


def _key_col_start(j):
    return int(np.clip(j * NA_KW - NA_KW // 2, 0, GRID_W - NA_KB))


def _qkv_kernel(x_ref, g_ref, sh_ref, sc_ref, w_ref, qg_ref, kg_ref, *out_refs, need_q, gather):
    h = _rms_mod(x_ref[...], g_ref[...], sh_ref[...], sc_ref[...]).astype(BF16)
    first = lax.broadcasted_iota(jnp.int32, (1, LANES), 1) < NA_HEAD_DIM

    def head_norm(y, gain):
        parts = []
        for c in range(0, D_MODEL, LANES):
            t = y[:, c:c + LANES]
            sq = t * t
            s_first = jnp.sum(jnp.where(first, sq, 0.0), axis=-1, keepdims=True)
            s_both = jnp.sum(sq, axis=-1, keepdims=True)
            ms = jnp.where(first, s_first, s_both - s_first) * (1.0 / NA_HEAD_DIM)
            parts.append(t * lax.rsqrt(ms + EPS))
        return jnp.concatenate(parts, axis=1) * gain

    refs = list(out_refs)
    if need_q:
        refs.pop(0)[...] = head_norm(_dot(h, w_ref[:, 0:D_MODEL]), qg_ref[...]).astype(BF16)
    k_ref, v_ref = refs
    k = head_norm(_dot(h, w_ref[:, D_MODEL:2 * D_MODEL]), kg_ref[...])
    v = _dot(h, w_ref[:, 2 * D_MODEL:3 * D_MODEL])
    if not gather:
        k_ref[...] = k.astype(BF16)
        v_ref[...] = v.astype(BF16)
        return
    rows = k.shape[0] // GRID_W
    k3 = k.reshape(rows, GRID_W, D_MODEL)
    v3 = v.reshape(rows, GRID_W, D_MODEL)
    for j in range(NA_NCB):
        c0 = _key_col_start(j)
        k_ref[:, j, :, :] = k3[:, c0:c0 + NA_KB, :].astype(BF16)
        v_ref[:, j, :, :] = v3[:, c0:c0 + NA_KB, :].astype(BF16)


def _qkv(x, mods, layer, norm_g, w_qkv, mixer, q_gain, k_gain, *, latent, need_q, tm):
    n = x.shape[0]
    tiles_per_batch = (SEQ // tm) if latent else None
    tok = pl.BlockSpec((tm, D_MODEL), lambda t: (t, 0))
    out_specs, out_shapes = [], []
    if need_q:
        out_specs.append(tok)
        out_shapes.append(jax.ShapeDtypeStruct((n, D_MODEL), BF16))
    if latent:
        rows = tm // GRID_W
        kv_spec = pl.BlockSpec((rows, NA_NCB, NA_KB, D_MODEL), lambda t: (t, 0, 0, 0))
        kv_shape = jax.ShapeDtypeStruct((n // GRID_W, NA_NCB, NA_KB, D_MODEL), BF16)
    else:
        kv_spec, kv_shape = tok, jax.ShapeDtypeStruct((n, D_MODEL), BF16)
    out_specs += [kv_spec, kv_spec]
    out_shapes += [kv_shape, kv_shape]
    return pl.pallas_call(
        functools.partial(_qkv_kernel, need_q=need_q, gather=latent),
        grid=(n // tm,),
        in_specs=[
            tok,
            _vec_spec(D_MODEL, layer),
            _mod_spec(layer, 0, tiles_per_batch),
            _mod_spec(layer, 1, tiles_per_batch),
            _resident((D_MODEL, 3 * D_MODEL), mixer),
            _vec_spec(D_MODEL, mixer),
            _vec_spec(D_MODEL, mixer),
        ],
        out_specs=out_specs,
        out_shape=out_shapes,
        compiler_params=_params(1),
        name="qkv_lat" if latent else "qkv_ctx",
    )(x, norm_g, mods, mods, w_qkv, q_gain, k_gain)


def _pair_scores(q2, keys, bias_pair=None, shift=None):
    lane = lax.broadcasted_iota(jnp.int32, (1, LANES), 1)
    first = lane < NA_HEAD_DIM
    zero = jnp.zeros_like(q2)
    q4 = jnp.concatenate([jnp.where(first, q2, zero), jnp.where(first, zero, q2)], axis=0)
    s = _dot_nt(q4, keys)
    if bias_pair is None:
        return s
    nb = bias_pair[0].shape[1]
    rest = [s[:, i:i + LANES] - shift for i in range(nb, s.shape[1], LANES)]
    return jnp.concatenate([s[:, :nb] + jnp.concatenate(bias_pair, axis=0)] + rest, axis=1)


def _pair_softmax(s, row_max=True):
    tiles = [s[:, i:i + LANES] for i in range(0, s.shape[1], LANES)]
    if row_max:
        mx = jnp.max(functools.reduce(jnp.maximum, tiles), axis=-1, keepdims=True)
        tiles = [t - mx for t in tiles]
    probs = [jnp.exp2(t) for t in tiles]
    denom = jnp.sum(functools.reduce(lambda a, b: a + b, probs), axis=-1, keepdims=True)
    return jnp.concatenate([p.astype(BF16) for p in probs], axis=1), denom


def _pair_pv(p, denom, values):
    m_rows = p.shape[0] // 2
    o = _dot(p, values) / denom
    lane = lax.broadcasted_iota(jnp.int32, (1, LANES), 1)
    return jnp.where(lane < NA_HEAD_DIM, o[:m_rows], o[m_rows:])


def _attn_lat_kernel(q_ref, k0, k1, k2, k3, v0, v1, v2, v3, kc_ref, vc_ref, bias_ref, shift_ref,
                     o_ref, *, row_max):
    kv_rows = KEY_ROW_GROUP * NA_KB

    def lanes(hp):
        return slice(hp * LANES, (hp + 1) * LANES)

    def gather(refs, ctx_ref, t):
        bi, hp = divmod(t, HEAD_PAIRS)
        return jnp.concatenate(
            [r[bi, :, :, lanes(hp)].reshape(kv_rows, LANES) for r in refs] + [ctx_ref[bi, :, lanes(hp)]],
            axis=0)

    def scores(t):
        bi, hp = divmod(t, HEAD_PAIRS)
        q2 = q_ref[bi, :, :, lanes(hp)].reshape(Q_PATCH, LANES)
        return _pair_scores(q2, gather((k0, k1, k2, k3), kc_ref, t),
                            (bias_ref[2 * hp], bias_ref[2 * hp + 1]), shift_ref[...])

    n_stages = q_ref.shape[0] * HEAD_PAIRS
    s_next, prev = scores(0), None
    for t in range(n_stages + 1):
        s = s_next
        if t + 1 < n_stages:
            s_next = scores(t + 1)
        if prev is not None:
            bi, hp = divmod(t - 1, HEAD_PAIRS)
            o = _pair_pv(*prev, gather((v0, v1, v2, v3), vc_ref, t - 1))
            o_ref[bi, :, :, lanes(hp)] = o.reshape(Q_ROWS, NA_KW, LANES).astype(BF16)
        if t < n_stages:
            prev = _pair_softmax(s, row_max)


def _key_group_start(rb):
    return jnp.clip(2 * rb - 1, 0, GRID_ROWS // KEY_ROW_GROUP - KEY_GROUPS)


def _attn_lat(q, k, v, kc, vc, bias, attn_layer, logit_bound):
    n_rb = GRID_ROWS // Q_ROWS
    nb = ATTN_BATCHES
    q4 = q.reshape(BATCH, GRID_ROWS, GRID_W, D_MODEL)
    k5 = k.reshape(BATCH, GRID_ROWS, NA_NCB, NA_KB, D_MODEL)
    v5 = v.reshape(BATCH, GRID_ROWS, NA_NCB, NA_KB, D_MODEL)
    kc3 = kc.reshape(BATCH, CTX_LEN, D_MODEL)
    vc3 = vc.reshape(BATCH, CTX_LEN, D_MODEL)
    q_spec = pl.BlockSpec((nb, Q_ROWS, NA_KW, D_MODEL), lambda rb, j, b: (b, rb, j, 0))

    def kv_spec(i):
        return pl.BlockSpec(
            (nb, KEY_ROW_GROUP, None, NA_KB, D_MODEL),
            lambda rb, j, b: (b, _key_group_start(rb) + i, j, 0, 0))

    ctx_spec = pl.BlockSpec((nb, CTX_LEN, D_MODEL), lambda rb, j, b: (b, 0, 0))

    def bias_index(rb, j, b):
        variant = (rb > 0).astype(jnp.int32) + (rb == n_rb - 1).astype(jnp.int32)
        return (variant * NA_NCB + j, attn_layer, 0, 0)

    bias_spec = pl.BlockSpec((None, NA_HEADS, Q_PATCH, LAT_KEYS), bias_index)
    shift = jnp.full((1, LANES), logit_bound, F32)

    def run(row_max):
        return pl.pallas_call(
            functools.partial(_attn_lat_kernel, row_max=row_max),
            grid=(n_rb, NA_NCB, BATCH // nb),
            in_specs=([q_spec] + [kv_spec(i) for i in range(KEY_GROUPS)] * 2
                      + [ctx_spec, ctx_spec, bias_spec, pl.BlockSpec((1, LANES), lambda rb, j, b: (0, 0))]),
            out_specs=q_spec,
            out_shape=jax.ShapeDtypeStruct(q4.shape, BF16),
            compiler_params=_params(3),
            name="attn_lat_rowmax" if row_max else "attn_lat",
        )(q4, k5, k5, k5, k5, v5, v5, v5, v5, kc3, vc3, bias, shift)

    out = lax.cond(logit_bound <= EXP2_SAFE_BOUND, lambda: run(False), lambda: run(True))
    return out.reshape(BATCH * SEQ, D_MODEL)


def _attn_ctx_kernel(q_ref, k_ref, v_ref, o_ref):
    for hp in range(HEAD_PAIRS):
        sl = slice(hp * LANES, (hp + 1) * LANES)
        p, denom = _pair_softmax(_pair_scores(q_ref[:, sl], k_ref[:, sl], None))
        o_ref[:, sl] = _pair_pv(p, denom, v_ref[:, sl]).astype(BF16)


def _attn_ctx(qc, kc, vc):
    spec = pl.BlockSpec((CTX_LEN, D_MODEL), lambda b: (b, 0))
    return pl.pallas_call(
        _attn_ctx_kernel,
        grid=(BATCH,),
        in_specs=[spec, spec, spec],
        out_specs=spec,
        out_shape=jax.ShapeDtypeStruct(qc.shape, BF16),
        compiler_params=_params(1),
        name="attn_ctx",
    )(qc, kc, vc)


def _attention_bias_tables(rpb):
    n_rb = GRID_ROWS // Q_ROWS
    n_heads = rpb.shape[0]
    col_pad, row_pad = NA_KW, Q_ROWS
    rpb_p = jnp.pad(rpb, ((0, 0), (row_pad, row_pad), (col_pad, col_pad)))
    col_off = np.array([_key_col_start(j) - j * NA_KW + NA_KW - 1 + col_pad for j in range(NA_NCB)])
    src_col = (np.arange(NA_KB)[None, None, :] - np.arange(NA_KW)[None, :, None]
               + col_off[:, None, None])
    pick = (src_col[..., None] == np.arange(rpb_p.shape[2])).astype(np.float32)
    by_col = jnp.einsum('hab,jqmb->jhqam', rpb_p, pick, precision=lax.Precision.HIGHEST)
    by_col = by_col.reshape(NA_NCB, n_heads, NA_KW, -1)
    by_col = jnp.pad(by_col, ((0, 0), (0, 0), (0, 0), (0, BIAS_SRC_LANES - by_col.shape[3])))
    valid, row_offs = [], []
    for rb in (0, 1, n_rb - 1):
        g0 = int(np.clip(2 * rb - 1, 0, GRID_ROWS // KEY_ROW_GROUP - KEY_GROUPS))
        row_offs.append(g0 * KEY_ROW_GROUP - rb * Q_ROWS + NA_KH - 1 + row_pad)
        for j in range(NA_NCB):
            qrow = (rb * Q_ROWS + np.arange(Q_ROWS))[:, None, None, None]
            qcol = (j * NA_KW + np.arange(NA_KW))[None, :, None, None]
            krow = (g0 * KEY_ROW_GROUP + np.arange(KEY_ROWS))[None, None, :, None]
            kcol = (_key_col_start(j) + np.arange(NA_KB))[None, None, None, :]
            rs = np.clip(qrow - NA_KH // 2, 0, GRID_ROWS - NA_KH)
            ws = np.clip(qcol - NA_KW // 2, 0, GRID_W - NA_KW)
            ok = (krow >= rs) & (krow < rs + NA_KH) & (kcol >= ws) & (kcol < ws + NA_KW)
            valid.append(np.broadcast_to(ok, (Q_ROWS, NA_KW, KEY_ROWS, NA_KB)).reshape(Q_PATCH, LAT_KEYS))
    valid = jnp.asarray(np.stack(valid).reshape(3, NA_NCB, Q_PATCH, LAT_KEYS), F32)

    heads_per_step = 8
    out = pl.pallas_call(
        functools.partial(_bias_table_kernel, row_offs=tuple(row_offs)),
        grid=(NA_NCB, n_heads // heads_per_step),
        in_specs=[
            pl.BlockSpec((None, heads_per_step, NA_KW, BIAS_SRC_LANES), lambda j, hb: (j, hb, 0, 0)),
            pl.BlockSpec((3, None, Q_PATCH, LAT_KEYS), lambda j, hb: (0, j, 0, 0)),
        ],
        out_specs=pl.BlockSpec((3, None, heads_per_step, Q_PATCH, LAT_KEYS),
                               lambda j, hb: (0, j, hb, 0, 0)),
        out_shape=jax.ShapeDtypeStruct((3, NA_NCB, n_heads, Q_PATCH, LAT_KEYS), F32),
        compiler_params=_params(2),
        name="bias_table",
    )(by_col, valid)
    return out.reshape(3 * NA_NCB, n_heads, Q_PATCH, LAT_KEYS)


BIAS_SRC_LANES = 1024


def _bias_table_kernel(src_ref, valid_ref, o_ref, *, row_offs):
    for v, row_off in enumerate(row_offs):
        for ql in range(Q_ROWS):
            start = (row_off - ql) * NA_KB
            rows = slice(ql * NA_KW, (ql + 1) * NA_KW)
            ok = valid_ref[v, rows, :] > 0.0
            for h in range(src_ref.shape[0]):
                src = src_ref[h]
                if start:
                    src = pltpu.roll(src, BIAS_SRC_LANES - start, axis=1)
                o_ref[v, h, rows, :] = jnp.where(ok, src[:, :LAT_KEYS], NEG_INF)


def _mlp_kernel(*refs, with_proj):
    if with_proj:
        x_ref, a_ref, wo_ref, g1_ref, refs = refs[0], refs[1], refs[2], refs[3], refs[4:]
        x = x_ref[...] + g1_ref[...] * _dot(a_ref[...], wo_ref[...])
    else:
        x_ref, refs = refs[0], refs[1:]
        x = x_ref[...]
    n_ref, sh_ref, sc_ref, g2_ref, w1_ref, w2_ref, o_ref = refs
    h = _rms_mod(x, n_ref[...], sh_ref[...], sc_ref[...]).astype(BF16)
    t = jnp.maximum(_dot(h, w1_ref[...]), 0.0)
    t = (t * t).astype(BF16)
    o_ref[...] = x + g2_ref[...] * _dot(t, w2_ref[...])


def _mlp(x, mods, layer, norm_g, w1, w2, *, latent, tm, attn=None, w_o=None, mixer=None):
    n = x.shape[0]
    tiles_per_batch = (SEQ // tm) if latent else None
    tok = pl.BlockSpec((tm, D_MODEL), lambda t: (t, 0))
    with_proj = attn is not None
    in_specs, args = [tok], [x]
    if with_proj:
        in_specs += [tok, _resident((D_MODEL, D_MODEL), mixer), _mod_spec(layer, 2, tiles_per_batch)]
        args += [attn, w_o, mods]
    in_specs += [
        _vec_spec(D_MODEL, layer),
        _mod_spec(layer, 3, tiles_per_batch),
        _mod_spec(layer, 4, tiles_per_batch),
        _mod_spec(layer, 5, tiles_per_batch),
        _resident((D_MODEL, MLP_HIDDEN), layer),
        _resident((MLP_HIDDEN, D_MODEL), layer),
    ]
    args += [norm_g, mods, mods, mods, w1, w2]
    return pl.pallas_call(
        functools.partial(_mlp_kernel, with_proj=with_proj),
        grid=(n // tm,),
        in_specs=in_specs,
        out_specs=tok,
        out_shape=jax.ShapeDtypeStruct((n, D_MODEL), F32),
        compiler_params=_params(1),
        name=("proj_mlp" if with_proj else "mlp") + ("_lat" if latent else "_ctx"),
    )(*args)


SG_COLS = 1024


def _sg_kernel(x_ref, n_ref, sh_ref, sc_ref, g1_ref, win_ref, bin_ref, lng_ref, lnb_ref,
               ws_ref, bs_ref, wo_ref, o_ref, v_scr, t_scr):
    x = x_ref[...]
    tm = x.shape[0]
    n_chunks = tm // SG_CHUNK
    h = _rms_mod(x, n_ref[...], sh_ref[...], sc_ref[...]).astype(BF16)
    n_steps = SG_HALF // SG_COLS
    groups_per_step = SG_COLS // SG_GROUP_CH

    def cols(c):
        return slice(c * SG_COLS, (c + 1) * SG_COLS)

    def lane_tile_sum(a):
        return functools.reduce(lambda p, q: p + q,
                                [a[:, i:i + LANES] for i in range(0, a.shape[1], LANES)])

    s1 = s2 = None
    for c in range(n_steps):
        vc = slice(SG_HALF + c * SG_COLS, SG_HALF + (c + 1) * SG_COLS)
        v_blk = _gelu(_dot(h, win_ref[:, vc]) + bin_ref[:, vc])
        v_scr[:, cols(c)] = v_blk
        p1, p2 = lane_tile_sum(v_blk), lane_tile_sum(v_blk * v_blk)
        s1, s2 = (p1, p2) if s1 is None else (s1 + p1, s2 + p2)
    mu = jnp.sum(s1, axis=-1, keepdims=True) * (1.0 / SG_HALF)
    var = jnp.sum(s2, axis=-1, keepdims=True) * (1.0 / SG_HALF) - mu * mu
    rstd = lax.rsqrt(var + EPS)

    for c in range(n_steps):
        u = _gelu((_dot(h, win_ref[:, cols(c)]) + bin_ref[:, cols(c)]).astype(BF16))
        vn = ((v_scr[:, cols(c)] - mu) * rstd * lng_ref[:, cols(c)] + lnb_ref[:, cols(c)]).astype(BF16)
        for gg in range(groups_per_step):
            g = c * groups_per_step + gg
            gl = slice(gg * SG_GROUP_CH, (gg + 1) * SG_GROUP_CH)
            rhs = jnp.concatenate(
                [vn[i * SG_CHUNK:(i + 1) * SG_CHUNK, gl] for i in range(n_chunks)], axis=1)
            s = _dot(ws_ref[g], rhs)
            for i in range(n_chunks):
                rows = slice(i * SG_CHUNK, (i + 1) * SG_CHUNK)
                s_i = s[:, i * SG_GROUP_CH:(i + 1) * SG_GROUP_CH] + bs_ref[g]
                t_scr[rows, g * SG_GROUP_CH:(g + 1) * SG_GROUP_CH] = u[rows, gl] * s_i.astype(BF16)
    o_ref[...] = x + g1_ref[...] * _dot(t_scr[...], wo_ref[...])


def _sg(x, mods, layer, norm_g, mixer, w_in, b_in, ln_g, ln_b, w_s, b_s, w_o, *, latent, tm):
    n = x.shape[0]
    tiles_per_batch = (SEQ // tm) if latent else None
    tok = pl.BlockSpec((tm, D_MODEL), lambda t: (t, 0))
    return pl.pallas_call(
        _sg_kernel,
        grid=(n // tm,),
        in_specs=[
            tok,
            _vec_spec(D_MODEL, layer),
            _mod_spec(layer, 0, tiles_per_batch),
            _mod_spec(layer, 1, tiles_per_batch),
            _mod_spec(layer, 2, tiles_per_batch),
            _resident((D_MODEL, 2 * SG_HALF), mixer),
            _vec_spec(2 * SG_HALF, mixer),
            _vec_spec(SG_HALF, mixer),
            _vec_spec(SG_HALF, mixer),
            _resident((SG_GROUPS, SG_CHUNK, SG_CHUNK), mixer),
            _resident((SG_GROUPS, SG_CHUNK, SG_GROUP_CH), mixer),
            _resident((SG_HALF, D_MODEL), mixer),
        ],
        out_specs=tok,
        out_shape=jax.ShapeDtypeStruct((n, D_MODEL), F32),
        scratch_shapes=[pltpu.VMEM((tm, SG_HALF), F32), pltpu.VMEM((tm, SG_HALF), BF16)],
        compiler_params=_params(1),
        name="sg_lat" if latent else "sg_ctx",
    )(x, norm_g, mods, mods, mods, w_in, b_in, ln_g, ln_b, w_s, b_s, w_o)


LAT_TM = 1024
SG_TM = 512
CTX_TM = 512


def kernel(x, c, ctx, c_ctx, ada_w, ada_b, norm1_g, norm2_g, mlp_w1, mlp_w2,
           na_w_qkv, na_q_norm, na_k_norm, na_rpb, na_w_o,
           sg_w_in, sg_b_in, sg_ln_g, sg_ln_b, sg_w_s, sg_b_s, sg_w_o):
    last_ctx_layer = ((DEPTH - 1) // N_MIXERS) * N_MIXERS
    xl = x.reshape(BATCH * SEQ, D_MODEL)
    xc = ctx.reshape(BATCH * CTX_LEN, D_MODEL)

    cond = jnp.concatenate(
        [c, c_ctx[None, :], jnp.zeros((MOD_ROWS - BATCH - 1, D_MODEL), F32)], axis=0)
    mods = _adaln(cond, ada_w, ada_b).reshape(DEPTH * MOD_ROWS * 6, 1, D_MODEL)

    q_gains = na_q_norm * (NA_HEAD_DIM ** -0.5 * LOG2_E)
    rpb2 = na_rpb * LOG2_E
    logit_bounds = (NA_HEAD_DIM * 1.01 * jnp.max(jnp.abs(q_gains), axis=1)
                    * jnp.max(jnp.abs(na_k_norm), axis=1) + jnp.max(jnp.abs(rpb2), axis=(1, 2, 3)))
    bias = _attention_bias_tables(
        (rpb2 - logit_bounds[:, None, None, None]).reshape((-1,) + na_rpb.shape[2:]))

    w1, w2 = mlp_w1.astype(BF16), mlp_w2.astype(BF16)
    w_qkv, w_ao = na_w_qkv.astype(BF16), na_w_o.astype(BF16)
    w_in, w_s, w_so = sg_w_in.astype(BF16), sg_w_s.astype(BF16), sg_w_o.astype(BF16)

    n1, n2 = norm1_g[:, None, :], norm2_g[:, None, :]
    q_gain = jnp.tile(q_gains, (1, NA_HEADS))[:, None, :]
    k_gain = jnp.tile(na_k_norm, (1, NA_HEADS))[:, None, :]
    sg_vecs = (sg_b_in[:, None, :], sg_ln_g[:, None, :], sg_ln_b[:, None, :])
    sg_bias = jnp.broadcast_to(sg_b_s[:, :, :, None], sg_b_s.shape + (SG_GROUP_CH,))

    for i in range(DEPTH):
        ctx_full = i < last_ctx_layer
        if i % N_MIXERS == 0:
            a = i // N_MIXERS
            q, k, v = _qkv(xl, mods, i, n1, w_qkv, a, q_gain, k_gain,
                           latent=True, need_q=True, tm=LAT_TM)
            ctx_proj = _qkv(xc, mods, i, n1, w_qkv, a, q_gain, k_gain,
                            latent=False, need_q=ctx_full, tm=CTX_TM)
            kc, vc = ctx_proj[-2], ctx_proj[-1]
            att = _attn_lat(q, k, v, kc, vc, bias, a, logit_bounds[a])
            xl = _mlp(xl, mods, i, n2, w1, w2, latent=True, tm=LAT_TM, attn=att, w_o=w_ao, mixer=a)
            if ctx_full:
                att_c = _attn_ctx(ctx_proj[0], kc, vc)
                xc = _mlp(xc, mods, i, n2, w1, w2, latent=False, tm=CTX_TM, attn=att_c, w_o=w_ao,
                          mixer=a)
        else:
            s = i // N_MIXERS
            sg_args = (s, w_in, *sg_vecs, w_s, sg_bias, w_so)
            xl = _sg(xl, mods, i, n1, *sg_args, latent=True, tm=SG_TM)
            xl = _mlp(xl, mods, i, n2, w1, w2, latent=True, tm=LAT_TM)
            if ctx_full:
                xc = _sg(xc, mods, i, n1, *sg_args, latent=False, tm=CTX_TM)
                xc = _mlp(xc, mods, i, n2, w1, w2, latent=False, tm=CTX_TM)
    return xl.reshape(BATCH, SEQ, D_MODEL)
```

```python
import functools

import numpy as np
import jax
import jax.numpy as jnp
from jax import lax
from jax.experimental import pallas as pl
from jax.experimental.pallas import tpu as pltpu

D_MODEL = 1024
BATCH = 8
SEQ = 4096
DEPTH = 4
GRID_W = 64
GRID_ROWS = SEQ // GRID_W
CTX_LEN = 256
N_MIXERS = 2
NA_HEADS = 16
NA_HEAD_DIM = D_MODEL // NA_HEADS
NA_KH = 8
NA_KW = 16
NA_KB = 2 * NA_KW
NA_NCB = GRID_W // NA_KW
SG_CHUNK = 128
SG_HALF = 3 * D_MODEL
SG_GROUP_CH = 128
SG_GROUPS = SG_HALF // SG_GROUP_CH
MLP_HIDDEN = 4 * D_MODEL
EPS = 1e-6
NEG_INF = -1e30
LOG2_E = float(np.log2(np.e))
EXP2_SAFE_BOUND = 60.0

F32 = jnp.float32
BF16 = jnp.bfloat16

LANES = 128
HEAD_PAIRS = NA_HEADS // 2
MOD_ROWS = 16
CTX_MOD_ROW = BATCH
Q_ROWS = 8
Q_PATCH = Q_ROWS * NA_KW
KEY_ROW_GROUP = 4
KEY_GROUPS = 4
KEY_ROWS = KEY_ROW_GROUP * KEY_GROUPS
LAT_KEYS = KEY_ROWS * NA_KB
ATTN_BATCHES = 4
BIAS_ROW_SHIFTS = 2 * NA_KH + 2 * Q_ROWS
BIAS_SRC_LANES = BIAS_ROW_SHIFTS * NA_KB
BF16_DOT_SLACK = 1.01
VMEM_LIMIT = 56 * 1024 * 1024


def _dot(a, b):
    return jnp.dot(a, b, preferred_element_type=F32)


def _dot_nt(a, b):
    return lax.dot_general(a, b, (((1,), (1,)), ((), ())), preferred_element_type=F32)


def _rms_mod(x, g, shift, scale):
    ms = jnp.mean(x * x, axis=-1, keepdims=True)
    h = x * lax.rsqrt(ms + EPS) * g
    return h * (1.0 + scale) + shift


def _gelu(z):
    half = np.asarray(0.5, z.dtype)
    return half * z * (np.asarray(1.0, z.dtype) + lax.erf(z * np.asarray(np.sqrt(0.5), z.dtype)))


def _split_bf16(a):
    hi = a.astype(BF16)
    lo = (a - hi.astype(F32)).astype(BF16)
    return hi, lo


def _params(n_axes):
    return pltpu.CompilerParams(dimension_semantics=("arbitrary",) * n_axes,
                                vmem_limit_bytes=VMEM_LIMIT)


def _resident(shape, stack_index=None):
    nd = len(shape)
    if stack_index is None:
        return pl.BlockSpec(shape, lambda *_: (0,) * nd, pipeline_mode=pl.Buffered(1))
    return pl.BlockSpec((None,) + tuple(shape), lambda *_: (stack_index,) + (0,) * nd,
                        pipeline_mode=pl.Buffered(1))


def _adaln_kernel(c_ref, w_ref, b_ref, o_ref):
    c = c_ref[...]
    s = c * (1.0 / (1.0 + jnp.exp(-c)))
    s_hi, s_lo = _split_bf16(s)
    w_hi, w_lo = _split_bf16(w_ref[...])
    o_ref[...] = _dot(s_hi, w_hi) + (_dot(s_lo, w_hi) + _dot(s_hi, w_lo)) + b_ref[...]


def _adaln(cond, ada_w, ada_b):
    tn = 2 * D_MODEL
    return pl.pallas_call(
        _adaln_kernel,
        grid=(DEPTH, 6 * D_MODEL // tn),
        in_specs=[
            pl.BlockSpec((MOD_ROWS, D_MODEL), lambda i, n: (0, 0)),
            pl.BlockSpec((None, D_MODEL, tn), lambda i, n: (i, 0, n)),
            pl.BlockSpec((None, 1, tn), lambda i, n: (i, 0, n)),
        ],
        out_specs=pl.BlockSpec((None, MOD_ROWS, tn), lambda i, n: (i, 0, n)),
        out_shape=jax.ShapeDtypeStruct((DEPTH, MOD_ROWS, 6 * D_MODEL), F32),
        compiler_params=_params(2),
        name="adaln",
    )(cond, ada_w, ada_b.reshape(DEPTH, 1, 6 * D_MODEL))


def _mod_spec(layer, k, tiles_per_batch):
    base = layer * MOD_ROWS * 6 + k
    if tiles_per_batch is None:
        return pl.BlockSpec((None, 1, D_MODEL), lambda t: (base + CTX_MOD_ROW * 6, 0, 0))
    return pl.BlockSpec((None, 1, D_MODEL), lambda t: (base + (t // tiles_per_batch) * 6, 0, 0))


def _vec_spec(n, row):
    return pl.BlockSpec((None, 1, n), lambda t: (row, 0, 0))


def _key_col_start(j):
    return int(np.clip(j * NA_KW - NA_KW // 2, 0, GRID_W - NA_KB))


def _qkv_kernel(x_ref, g_ref, sh_ref, sc_ref, w_ref, qg_ref, kg_ref, *out_refs, need_q, gather):
    h = _rms_mod(x_ref[...], g_ref[...], sh_ref[...], sc_ref[...]).astype(BF16)
    first = lax.broadcasted_iota(jnp.int32, (1, LANES), 1) < NA_HEAD_DIM

    def head_norm(y, gain):
        parts = []
        for c in range(0, D_MODEL, LANES):
            t = y[:, c:c + LANES]
            sq = t * t
            s_first = jnp.sum(jnp.where(first, sq, 0.0), axis=-1, keepdims=True)
            s_both = jnp.sum(sq, axis=-1, keepdims=True)
            ms = jnp.where(first, s_first, s_both - s_first) * (1.0 / NA_HEAD_DIM)
            parts.append(t * lax.rsqrt(ms + EPS))
        return jnp.concatenate(parts, axis=1) * gain

    refs = list(out_refs)
    if need_q:
        refs.pop(0)[...] = head_norm(_dot(h, w_ref[:, 0:D_MODEL]), qg_ref[...]).astype(BF16)
    k_ref, v_ref = refs
    k = head_norm(_dot(h, w_ref[:, D_MODEL:2 * D_MODEL]), kg_ref[...])
    v = _dot(h, w_ref[:, 2 * D_MODEL:3 * D_MODEL])
    if not gather:
        k_ref[...] = k.astype(BF16)
        v_ref[...] = v.astype(BF16)
        return
    rows = k.shape[0] // GRID_W
    k3 = k.reshape(rows, GRID_W, D_MODEL)
    v3 = v.reshape(rows, GRID_W, D_MODEL)
    for j in range(NA_NCB):
        c0 = _key_col_start(j)
        k_ref[:, j, :, :] = k3[:, c0:c0 + NA_KB, :].astype(BF16)
        v_ref[:, j, :, :] = v3[:, c0:c0 + NA_KB, :].astype(BF16)


def _qkv(x, mods, layer, norm_g, w_qkv, mixer, q_gain, k_gain, *, latent, need_q, tm):
    n = x.shape[0]
    tiles_per_batch = (SEQ // tm) if latent else None
    tok = pl.BlockSpec((tm, D_MODEL), lambda t: (t, 0))
    out_specs, out_shapes = [], []
    if need_q:
        out_specs.append(tok)
        out_shapes.append(jax.ShapeDtypeStruct((n, D_MODEL), BF16))
    if latent:
        rows = tm // GRID_W
        kv_spec = pl.BlockSpec((rows, NA_NCB, NA_KB, D_MODEL), lambda t: (t, 0, 0, 0))
        kv_shape = jax.ShapeDtypeStruct((n // GRID_W, NA_NCB, NA_KB, D_MODEL), BF16)
    else:
        kv_spec, kv_shape = tok, jax.ShapeDtypeStruct((n, D_MODEL), BF16)
    out_specs += [kv_spec, kv_spec]
    out_shapes += [kv_shape, kv_shape]
    return pl.pallas_call(
        functools.partial(_qkv_kernel, need_q=need_q, gather=latent),
        grid=(n // tm,),
        in_specs=[
            tok,
            _vec_spec(D_MODEL, layer),
            _mod_spec(layer, 0, tiles_per_batch),
            _mod_spec(layer, 1, tiles_per_batch),
            _resident((D_MODEL, 3 * D_MODEL), mixer),
            _vec_spec(D_MODEL, mixer),
            _vec_spec(D_MODEL, mixer),
        ],
        out_specs=out_specs,
        out_shape=out_shapes,
        compiler_params=_params(1),
        name="qkv_lat" if latent else "qkv_ctx",
    )(x, norm_g, mods, mods, w_qkv, q_gain, k_gain)


def _pair_scores(q2, keys, bias_pair=None, shift=None):
    lane = lax.broadcasted_iota(jnp.int32, (1, LANES), 1)
    first = lane < NA_HEAD_DIM
    zero = jnp.zeros_like(q2)
    q4 = jnp.concatenate([jnp.where(first, q2, zero), jnp.where(first, zero, q2)], axis=0)
    s = _dot_nt(q4, keys)
    if bias_pair is None:
        return s
    nb = bias_pair[0].shape[1]
    rest = [s[:, i:i + LANES] - shift for i in range(nb, s.shape[1], LANES)]
    return jnp.concatenate([s[:, :nb] + jnp.concatenate(bias_pair, axis=0)] + rest, axis=1)


def _pair_softmax(s, row_max=True):
    tiles = [s[:, i:i + LANES] for i in range(0, s.shape[1], LANES)]
    if row_max:
        mx = jnp.max(functools.reduce(jnp.maximum, tiles), axis=-1, keepdims=True)
        tiles = [t - mx for t in tiles]
    probs = [jnp.exp2(t) for t in tiles]
    denom = jnp.sum(functools.reduce(lambda a, b: a + b, probs), axis=-1, keepdims=True)
    return jnp.concatenate([p.astype(BF16) for p in probs], axis=1), denom


def _pair_pv(p, denom, values):
    m_rows = p.shape[0] // 2
    o = _dot(p, values) / denom
    lane = lax.broadcasted_iota(jnp.int32, (1, LANES), 1)
    return jnp.where(lane < NA_HEAD_DIM, o[:m_rows], o[m_rows:])


def _attn_lat_kernel(q_ref, k0, k1, k2, k3, v0, v1, v2, v3, kc_ref, vc_ref, bias_ref, shift_ref,
                     o_ref, *, row_max):
    kv_rows = KEY_ROW_GROUP * NA_KB

    def lanes(hp):
        return slice(hp * LANES, (hp + 1) * LANES)

    def gather(refs, ctx_ref, t):
        bi, hp = divmod(t, HEAD_PAIRS)
        return jnp.concatenate(
            [r[bi, :, :, lanes(hp)].reshape(kv_rows, LANES) for r in refs] + [ctx_ref[bi, :, lanes(hp)]],
            axis=0)

    def scores(t):
        bi, hp = divmod(t, HEAD_PAIRS)
        q2 = q_ref[bi, :, :, lanes(hp)].reshape(Q_PATCH, LANES)
        return _pair_scores(q2, gather((k0, k1, k2, k3), kc_ref, t),
                            (bias_ref[2 * hp], bias_ref[2 * hp + 1]), shift_ref[...])

    n_stages = q_ref.shape[0] * HEAD_PAIRS
    s_next, prev = scores(0), None
    for t in range(n_stages + 1):
        s = s_next
        if t + 1 < n_stages:
            s_next = scores(t + 1)
        if prev is not None:
            bi, hp = divmod(t - 1, HEAD_PAIRS)
            o = _pair_pv(*prev, gather((v0, v1, v2, v3), vc_ref, t - 1))
            o_ref[bi, :, :, lanes(hp)] = o.reshape(Q_ROWS, NA_KW, LANES).astype(BF16)
        if t < n_stages:
            prev = _pair_softmax(s, row_max)


def _key_group_start(rb):
    return jnp.clip(2 * rb - 1, 0, GRID_ROWS // KEY_ROW_GROUP - KEY_GROUPS)


def _attn_lat(q, k, v, kc, vc, bias, attn_layer, logit_bound):
    n_rb = GRID_ROWS // Q_ROWS
    nb = ATTN_BATCHES
    q4 = q.reshape(BATCH, GRID_ROWS, GRID_W, D_MODEL)
    k5 = k.reshape(BATCH, GRID_ROWS, NA_NCB, NA_KB, D_MODEL)
    v5 = v.reshape(BATCH, GRID_ROWS, NA_NCB, NA_KB, D_MODEL)
    kc3 = kc.reshape(BATCH, CTX_LEN, D_MODEL)
    vc3 = vc.reshape(BATCH, CTX_LEN, D_MODEL)
    q_spec = pl.BlockSpec((nb, Q_ROWS, NA_KW, D_MODEL), lambda rb, j, b: (b, rb, j, 0))

    def kv_spec(i):
        return pl.BlockSpec(
            (nb, KEY_ROW_GROUP, None, NA_KB, D_MODEL),
            lambda rb, j, b: (b, _key_group_start(rb) + i, j, 0, 0))

    ctx_spec = pl.BlockSpec((nb, CTX_LEN, D_MODEL), lambda rb, j, b: (b, 0, 0))

    def bias_index(rb, j, b):
        variant = (rb > 0).astype(jnp.int32) + (rb == n_rb - 1).astype(jnp.int32)
        return (variant * NA_NCB + j, attn_layer, 0, 0)

    bias_spec = pl.BlockSpec((None, NA_HEADS, Q_PATCH, LAT_KEYS), bias_index)
    shift = jnp.full((1, LANES), logit_bound, F32)

    def run(row_max):
        return pl.pallas_call(
            functools.partial(_attn_lat_kernel, row_max=row_max),
            grid=(n_rb, NA_NCB, BATCH // nb),
            in_specs=([q_spec] + [kv_spec(i) for i in range(KEY_GROUPS)] * 2
                      + [ctx_spec, ctx_spec, bias_spec, pl.BlockSpec((1, LANES), lambda rb, j, b: (0, 0))]),
            out_specs=q_spec,
            out_shape=jax.ShapeDtypeStruct(q4.shape, BF16),
            compiler_params=_params(3),
            name="attn_lat_rowmax" if row_max else "attn_lat",
        )(q4, k5, k5, k5, k5, v5, v5, v5, v5, kc3, vc3, bias, shift)

    out = lax.cond(logit_bound <= EXP2_SAFE_BOUND, lambda: run(False), lambda: run(True))
    return out.reshape(BATCH * SEQ, D_MODEL)


def _attn_ctx_kernel(q_ref, k_ref, v_ref, o_ref):
    for hp in range(HEAD_PAIRS):
        sl = slice(hp * LANES, (hp + 1) * LANES)
        p, denom = _pair_softmax(_pair_scores(q_ref[:, sl], k_ref[:, sl], None))
        o_ref[:, sl] = _pair_pv(p, denom, v_ref[:, sl]).astype(BF16)


def _attn_ctx(qc, kc, vc):
    spec = pl.BlockSpec((CTX_LEN, D_MODEL), lambda b: (b, 0))
    return pl.pallas_call(
        _attn_ctx_kernel,
        grid=(BATCH,),
        in_specs=[spec, spec, spec],
        out_specs=spec,
        out_shape=jax.ShapeDtypeStruct(qc.shape, BF16),
        compiler_params=_params(1),
        name="attn_ctx",
    )(qc, kc, vc)


def _attention_bias_tables(rpb):
    n_rb = GRID_ROWS // Q_ROWS
    n_heads = rpb.shape[0]
    col_pad, row_pad = NA_KW, Q_ROWS
    rpb_p = jnp.pad(rpb, ((0, 0), (row_pad, BIAS_ROW_SHIFTS - rpb.shape[1] - row_pad),
                          (col_pad, col_pad)))
    col_off = np.array([_key_col_start(j) - j * NA_KW + NA_KW - 1 + col_pad for j in range(NA_NCB)])
    src_col = (np.arange(NA_KB)[None, None, :] - np.arange(NA_KW)[None, :, None]
               + col_off[:, None, None])
    pick = (src_col[..., None] == np.arange(rpb_p.shape[2])).astype(np.float32)
    by_col = jnp.einsum('hab,jqmb->jhqam', rpb_p, pick, precision=lax.Precision.HIGHEST)
    by_col = by_col.reshape(NA_NCB, n_heads, NA_KW, BIAS_SRC_LANES)
    valid, row_offs = [], []
    for rb in (0, 1, n_rb - 1):
        g0 = int(np.clip(2 * rb - 1, 0, GRID_ROWS // KEY_ROW_GROUP - KEY_GROUPS))
        row_offs.append(g0 * KEY_ROW_GROUP - rb * Q_ROWS + NA_KH - 1 + row_pad)
        for j in range(NA_NCB):
            qrow = (rb * Q_ROWS + np.arange(Q_ROWS))[:, None, None, None]
            qcol = (j * NA_KW + np.arange(NA_KW))[None, :, None, None]
            krow = (g0 * KEY_ROW_GROUP + np.arange(KEY_ROWS))[None, None, :, None]
            kcol = (_key_col_start(j) + np.arange(NA_KB))[None, None, None, :]
            rs = np.clip(qrow - NA_KH // 2, 0, GRID_ROWS - NA_KH)
            ws = np.clip(qcol - NA_KW // 2, 0, GRID_W - NA_KW)
            ok = (krow >= rs) & (krow < rs + NA_KH) & (kcol >= ws) & (kcol < ws + NA_KW)
            valid.append(np.broadcast_to(ok, (Q_ROWS, NA_KW, KEY_ROWS, NA_KB)).reshape(Q_PATCH, LAT_KEYS))
    valid = jnp.asarray(np.stack(valid).reshape(3, NA_NCB, Q_PATCH, LAT_KEYS), F32)

    heads_per_step = 8
    out = pl.pallas_call(
        functools.partial(_bias_table_kernel, row_offs=tuple(row_offs)),
        grid=(NA_NCB, n_heads // heads_per_step),
        in_specs=[
            pl.BlockSpec((None, heads_per_step, NA_KW, BIAS_SRC_LANES), lambda j, hb: (j, hb, 0, 0)),
            pl.BlockSpec((3, None, Q_PATCH, LAT_KEYS), lambda j, hb: (0, j, 0, 0)),
        ],
        out_specs=pl.BlockSpec((3, None, heads_per_step, Q_PATCH, LAT_KEYS),
                               lambda j, hb: (0, j, hb, 0, 0)),
        out_shape=jax.ShapeDtypeStruct((3, NA_NCB, n_heads, Q_PATCH, LAT_KEYS), F32),
        compiler_params=_params(2),
        name="bias_table",
    )(by_col, valid)
    return out.reshape(3 * NA_NCB, n_heads, Q_PATCH, LAT_KEYS)


def _bias_table_kernel(src_ref, valid_ref, o_ref, *, row_offs):
    for v, row_off in enumerate(row_offs):
        for ql in range(Q_ROWS):
            start = (row_off - ql) * NA_KB
            rows = slice(ql * NA_KW, (ql + 1) * NA_KW)
            ok = valid_ref[v, rows, :] > 0.0
            for h in range(src_ref.shape[0]):
                src = src_ref[h]
                if start:
                    src = pltpu.roll(src, BIAS_SRC_LANES - start, axis=1)
                o_ref[v, h, rows, :] = jnp.where(ok, src[:, :LAT_KEYS], NEG_INF)


def _mlp_kernel(*refs, with_proj):
    if with_proj:
        x_ref, a_ref, wo_ref, g1_ref, refs = refs[0], refs[1], refs[2], refs[3], refs[4:]
        x = x_ref[...] + g1_ref[...] * _dot(a_ref[...], wo_ref[...])
    else:
        x_ref, refs = refs[0], refs[1:]
        x = x_ref[...]
    n_ref, sh_ref, sc_ref, g2_ref, w1_ref, w2_ref, o_ref = refs
    h = _rms_mod(x, n_ref[...], sh_ref[...], sc_ref[...]).astype(BF16)
    t = jnp.maximum(_dot(h, w1_ref[...]), 0.0)
    t = (t * t).astype(BF16)
    o_ref[...] = x + g2_ref[...] * _dot(t, w2_ref[...])


def _mlp(x, mods, layer, norm_g, w1, w2, *, latent, tm, attn=None, w_o=None, mixer=None):
    n = x.shape[0]
    tiles_per_batch = (SEQ // tm) if latent else None
    tok = pl.BlockSpec((tm, D_MODEL), lambda t: (t, 0))
    with_proj = attn is not None
    in_specs, args = [tok], [x]
    if with_proj:
        in_specs += [tok, _resident((D_MODEL, D_MODEL), mixer), _mod_spec(layer, 2, tiles_per_batch)]
        args += [attn, w_o, mods]
    in_specs += [
        _vec_spec(D_MODEL, layer),
        _mod_spec(layer, 3, tiles_per_batch),
        _mod_spec(layer, 4, tiles_per_batch),
        _mod_spec(layer, 5, tiles_per_batch),
        _resident((D_MODEL, MLP_HIDDEN), layer),
        _resident((MLP_HIDDEN, D_MODEL), layer),
    ]
    args += [norm_g, mods, mods, mods, w1, w2]
    return pl.pallas_call(
        functools.partial(_mlp_kernel, with_proj=with_proj),
        grid=(n // tm,),
        in_specs=in_specs,
        out_specs=tok,
        out_shape=jax.ShapeDtypeStruct((n, D_MODEL), F32),
        compiler_params=_params(1),
        name=("proj_mlp" if with_proj else "mlp") + ("_lat" if latent else "_ctx"),
    )(*args)


SG_COLS = 1024


def _sg_kernel(x_ref, n_ref, sh_ref, sc_ref, g1_ref, win_ref, bin_ref, lng_ref, lnb_ref,
               ws_ref, bs_ref, wo_ref, o_ref, v_scr, t_scr):
    x = x_ref[...]
    tm = x.shape[0]
    n_chunks = tm // SG_CHUNK
    h = _rms_mod(x, n_ref[...], sh_ref[...], sc_ref[...]).astype(BF16)
    n_steps = SG_HALF // SG_COLS
    groups_per_step = SG_COLS // SG_GROUP_CH

    def cols(c):
        return slice(c * SG_COLS, (c + 1) * SG_COLS)

    def lane_tile_sum(a):
        return functools.reduce(lambda p, q: p + q,
                                [a[:, i:i + LANES] for i in range(0, a.shape[1], LANES)])

    s1 = s2 = None
    for c in range(n_steps):
        vc = slice(SG_HALF + c * SG_COLS, SG_HALF + (c + 1) * SG_COLS)
        v_blk = _gelu(_dot(h, win_ref[:, vc]) + bin_ref[:, vc])
        v_scr[:, cols(c)] = v_blk
        p1, p2 = lane_tile_sum(v_blk), lane_tile_sum(v_blk * v_blk)
        s1, s2 = (p1, p2) if s1 is None else (s1 + p1, s2 + p2)
    mu = jnp.sum(s1, axis=-1, keepdims=True) * (1.0 / SG_HALF)
    var = jnp.sum(s2, axis=-1, keepdims=True) * (1.0 / SG_HALF) - mu * mu
    rstd = lax.rsqrt(var + EPS)

    for c in range(n_steps):
        u = _gelu((_dot(h, win_ref[:, cols(c)]) + bin_ref[:, cols(c)]).astype(BF16))
        vn = ((v_scr[:, cols(c)] - mu) * rstd * lng_ref[:, cols(c)] + lnb_ref[:, cols(c)]).astype(BF16)
        for gg in range(groups_per_step):
            g = c * groups_per_step + gg
            gl = slice(gg * SG_GROUP_CH, (gg + 1) * SG_GROUP_CH)
            rhs = jnp.concatenate(
                [vn[i * SG_CHUNK:(i + 1) * SG_CHUNK, gl] for i in range(n_chunks)], axis=1)
            s = _dot(ws_ref[g], rhs)
            for i in range(n_chunks):
                rows = slice(i * SG_CHUNK, (i + 1) * SG_CHUNK)
                s_i = s[:, i * SG_GROUP_CH:(i + 1) * SG_GROUP_CH] + bs_ref[g]
                t_scr[rows, g * SG_GROUP_CH:(g + 1) * SG_GROUP_CH] = u[rows, gl] * s_i.astype(BF16)
    o_ref[...] = x + g1_ref[...] * _dot(t_scr[...], wo_ref[...])


def _sg(x, mods, layer, norm_g, mixer, w_in, b_in, ln_g, ln_b, w_s, b_s, w_o, *, latent, tm):
    n = x.shape[0]
    tiles_per_batch = (SEQ // tm) if latent else None
    tok = pl.BlockSpec((tm, D_MODEL), lambda t: (t, 0))
    return pl.pallas_call(
        _sg_kernel,
        grid=(n // tm,),
        in_specs=[
            tok,
            _vec_spec(D_MODEL, layer),
            _mod_spec(layer, 0, tiles_per_batch),
            _mod_spec(layer, 1, tiles_per_batch),
            _mod_spec(layer, 2, tiles_per_batch),
            _resident((D_MODEL, 2 * SG_HALF), mixer),
            _vec_spec(2 * SG_HALF, mixer),
            _vec_spec(SG_HALF, mixer),
            _vec_spec(SG_HALF, mixer),
            _resident((SG_GROUPS, SG_CHUNK, SG_CHUNK), mixer),
            _resident((SG_GROUPS, SG_CHUNK, SG_GROUP_CH), mixer),
            _resident((SG_HALF, D_MODEL), mixer),
        ],
        out_specs=tok,
        out_shape=jax.ShapeDtypeStruct((n, D_MODEL), F32),
        scratch_shapes=[pltpu.VMEM((tm, SG_HALF), F32), pltpu.VMEM((tm, SG_HALF), BF16)],
        compiler_params=_params(1),
        name="sg_lat" if latent else "sg_ctx",
    )(x, norm_g, mods, mods, mods, w_in, b_in, ln_g, ln_b, w_s, b_s, w_o)


LAT_TM = 1024
SG_TM = 512
CTX_TM = 512


def kernel(x, c, ctx, c_ctx, ada_w, ada_b, norm1_g, norm2_g, mlp_w1, mlp_w2,
           na_w_qkv, na_q_norm, na_k_norm, na_rpb, na_w_o,
           sg_w_in, sg_b_in, sg_ln_g, sg_ln_b, sg_w_s, sg_b_s, sg_w_o):
    last_ctx_layer = ((DEPTH - 1) // N_MIXERS) * N_MIXERS
    xl = x.reshape(BATCH * SEQ, D_MODEL)
    xc = ctx.reshape(BATCH * CTX_LEN, D_MODEL)

    cond = jnp.concatenate(
        [c, c_ctx[None, :], jnp.zeros((MOD_ROWS - BATCH - 1, D_MODEL), F32)], axis=0)
    mods = _adaln(cond, ada_w, ada_b).reshape(DEPTH * MOD_ROWS * 6, 1, D_MODEL)

    q_gains = na_q_norm * (NA_HEAD_DIM ** -0.5 * LOG2_E)
    rpb2 = na_rpb * LOG2_E
    logit_bounds = (NA_HEAD_DIM * BF16_DOT_SLACK * jnp.max(jnp.abs(q_gains), axis=1)
                    * jnp.max(jnp.abs(na_k_norm), axis=1) + jnp.max(jnp.abs(rpb2), axis=(1, 2, 3)))
    bias = _attention_bias_tables(
        (rpb2 - logit_bounds[:, None, None, None]).reshape((-1,) + na_rpb.shape[2:]))

    w1, w2 = mlp_w1.astype(BF16), mlp_w2.astype(BF16)
    w_qkv, w_ao = na_w_qkv.astype(BF16), na_w_o.astype(BF16)
    w_in, w_s, w_so = sg_w_in.astype(BF16), sg_w_s.astype(BF16), sg_w_o.astype(BF16)

    n1, n2 = norm1_g[:, None, :], norm2_g[:, None, :]
    q_gain = jnp.tile(q_gains, (1, NA_HEADS))[:, None, :]
    k_gain = jnp.tile(na_k_norm, (1, NA_HEADS))[:, None, :]
    sg_vecs = (sg_b_in[:, None, :], sg_ln_g[:, None, :], sg_ln_b[:, None, :])
    sg_bias = jnp.broadcast_to(sg_b_s[:, :, :, None], sg_b_s.shape + (SG_GROUP_CH,))

    for i in range(DEPTH):
        ctx_full = i < last_ctx_layer
        if i % N_MIXERS == 0:
            a = i // N_MIXERS
            q, k, v = _qkv(xl, mods, i, n1, w_qkv, a, q_gain, k_gain,
                           latent=True, need_q=True, tm=LAT_TM)
            ctx_proj = _qkv(xc, mods, i, n1, w_qkv, a, q_gain, k_gain,
                            latent=False, need_q=ctx_full, tm=CTX_TM)
            kc, vc = ctx_proj[-2], ctx_proj[-1]
            att = _attn_lat(q, k, v, kc, vc, bias, a, logit_bounds[a])
            xl = _mlp(xl, mods, i, n2, w1, w2, latent=True, tm=LAT_TM, attn=att, w_o=w_ao, mixer=a)
            if ctx_full:
                att_c = _attn_ctx(ctx_proj[0], kc, vc)
                xc = _mlp(xc, mods, i, n2, w1, w2, latent=False, tm=CTX_TM, attn=att_c, w_o=w_ao,
                          mixer=a)
        else:
            s = i // N_MIXERS
            sg_args = (s, w_in, *sg_vecs, w_s, sg_bias, w_so)
            xl = _sg(xl, mods, i, n1, *sg_args, latent=True, tm=SG_TM)
            xl = _mlp(xl, mods, i, n2, w1, w2, latent=True, tm=LAT_TM)
            if ctx_full:
                xc = _sg(xc, mods, i, n1, *sg_args, latent=False, tm=CTX_TM)
                xc = _mlp(xc, mods, i, n2, w1, w2, latent=False, tm=CTX_TM)
    return xl.reshape(BATCH, SEQ, D_MODEL)
```

```python
import functools

import numpy as np
import jax
import jax.numpy as jnp
from jax import lax
from jax.experimental import pallas as pl
from jax.experimental.pallas import tpu as pltpu

D_MODEL = 1024
BATCH = 8
SEQ = 4096
DEPTH = 4
GRID_W = 64
GRID_ROWS = SEQ // GRID_W
CTX_LEN = 256
N_MIXERS = 2
NA_HEADS = 16
NA_HEAD_DIM = D_MODEL // NA_HEADS
NA_KH = 8
NA_KW = 16
NA_KB = 2 * NA_KW
NA_NCB = GRID_W // NA_KW
SG_CHUNK = 128
SG_HALF = 3 * D_MODEL
SG_GROUP_CH = 128
SG_GROUPS = SG_HALF // SG_GROUP_CH
MLP_HIDDEN = 4 * D_MODEL
EPS = 1e-6
NEG_INF = -1e30
LOG2_E = float(np.log2(np.e))
EXP2_SAFE_BOUND = 60.0

F32 = jnp.float32
BF16 = jnp.bfloat16

LANES = 128
HEAD_PAIRS = NA_HEADS // 2
MOD_ROWS = 16
CTX_MOD_ROW = BATCH
Q_ROWS = 8
Q_PATCH = Q_ROWS * NA_KW
KEY_ROW_GROUP = 4
KEY_GROUPS = 4
KEY_ROWS = KEY_ROW_GROUP * KEY_GROUPS
LAT_KEYS = KEY_ROWS * NA_KB
ATTN_BATCHES = 4
BIAS_ROW_SHIFTS = 2 * NA_KH + 2 * Q_ROWS
BIAS_SRC_LANES = BIAS_ROW_SHIFTS * NA_KB
BF16_DOT_SLACK = 1.01
VMEM_LIMIT = 56 * 1024 * 1024


def _dot(a, b):
    return jnp.dot(a, b, preferred_element_type=F32)


def _dot_nt(a, b):
    return lax.dot_general(a, b, (((1,), (1,)), ((), ())), preferred_element_type=F32)


def _rms_mod(x, g, shift, scale):
    ms = jnp.mean(x * x, axis=-1, keepdims=True)
    h = x * lax.rsqrt(ms + EPS) * g
    return h * (1.0 + scale) + shift


def _gelu(z):
    half = np.asarray(0.5, z.dtype)
    return half * z * (np.asarray(1.0, z.dtype) + lax.erf(z * np.asarray(np.sqrt(0.5), z.dtype)))


def _split_bf16(a):
    hi = a.astype(BF16)
    lo = (a - hi.astype(F32)).astype(BF16)
    return hi, lo


def _params(n_axes):
    return pltpu.CompilerParams(dimension_semantics=("arbitrary",) * n_axes,
                                vmem_limit_bytes=VMEM_LIMIT)


def _resident(shape, stack_index=None):
    nd = len(shape)
    if stack_index is None:
        return pl.BlockSpec(shape, lambda *_: (0,) * nd, pipeline_mode=pl.Buffered(1))
    return pl.BlockSpec((None,) + tuple(shape), lambda *_: (stack_index,) + (0,) * nd,
                        pipeline_mode=pl.Buffered(1))


def _adaln_kernel(c_ref, w_ref, b_ref, o_ref):
    c = c_ref[...]
    s = c * (1.0 / (1.0 + jnp.exp(-c)))
    s_hi, s_lo = _split_bf16(s)
    w_hi, w_lo = _split_bf16(w_ref[...])
    o_ref[...] = _dot(s_hi, w_hi) + (_dot(s_lo, w_hi) + _dot(s_hi, w_lo)) + b_ref[...]


def _adaln(cond, ada_w, ada_b):
    tn = 2 * D_MODEL
    return pl.pallas_call(
        _adaln_kernel,
        grid=(DEPTH, 6 * D_MODEL // tn),
        in_specs=[
            pl.BlockSpec((MOD_ROWS, D_MODEL), lambda i, n: (0, 0)),
            pl.BlockSpec((None, D_MODEL, tn), lambda i, n: (i, 0, n)),
            pl.BlockSpec((None, 1, tn), lambda i, n: (i, 0, n)),
        ],
        out_specs=pl.BlockSpec((None, MOD_ROWS, tn), lambda i, n: (i, 0, n)),
        out_shape=jax.ShapeDtypeStruct((DEPTH, MOD_ROWS, 6 * D_MODEL), F32),
        compiler_params=_params(2),
        name="adaln",
    )(cond, ada_w, ada_b.reshape(DEPTH, 1, 6 * D_MODEL))


def _mod_spec(layer, k, tiles_per_batch):
    base = layer * MOD_ROWS * 6 + k
    if tiles_per_batch is None:
        return pl.BlockSpec((None, 1, D_MODEL), lambda t: (base + CTX_MOD_ROW * 6, 0, 0))
    return pl.BlockSpec((None, 1, D_MODEL), lambda t: (base + (t // tiles_per_batch) * 6, 0, 0))


def _vec_spec(n, row):
    return pl.BlockSpec((None, 1, n), lambda t: (row, 0, 0))


def _key_col_start(j):
    return int(np.clip(j * NA_KW - NA_KW // 2, 0, GRID_W - NA_KB))


def _qkv_kernel(x_ref, g_ref, sh_ref, sc_ref, w_ref, qg_ref, kg_ref, *out_refs, need_q, gather):
    h = _rms_mod(x_ref[...], g_ref[...], sh_ref[...], sc_ref[...]).astype(BF16)
    first = lax.broadcasted_iota(jnp.int32, (1, LANES), 1) < NA_HEAD_DIM

    def head_norm(y, gain):
        parts = []
        for c in range(0, D_MODEL, LANES):
            t = y[:, c:c + LANES]
            sq = t * t
            s_first = jnp.sum(jnp.where(first, sq, 0.0), axis=-1, keepdims=True)
            s_both = jnp.sum(sq, axis=-1, keepdims=True)
            ms = jnp.where(first, s_first, s_both - s_first) * (1.0 / NA_HEAD_DIM)
            parts.append(t * lax.rsqrt(ms + EPS))
        return jnp.concatenate(parts, axis=1) * gain

    refs = list(out_refs)
    if need_q:
        refs.pop(0)[...] = head_norm(_dot(h, w_ref[:, 0:D_MODEL]), qg_ref[...]).astype(BF16)
    k_ref, v_ref = refs
    k = head_norm(_dot(h, w_ref[:, D_MODEL:2 * D_MODEL]), kg_ref[...])
    v = _dot(h, w_ref[:, 2 * D_MODEL:3 * D_MODEL])
    if not gather:
        k_ref[...] = k.astype(BF16)
        v_ref[...] = v.astype(BF16)
        return
    rows = k.shape[0] // GRID_W
    k3 = k.reshape(rows, GRID_W, D_MODEL)
    v3 = v.reshape(rows, GRID_W, D_MODEL)
    for j in range(NA_NCB):
        c0 = _key_col_start(j)
        k_ref[:, j, :, :] = k3[:, c0:c0 + NA_KB, :].astype(BF16)
        v_ref[:, j, :, :] = v3[:, c0:c0 + NA_KB, :].astype(BF16)


def _qkv(x, mods, layer, norm_g, w_qkv, mixer, q_gain, k_gain, *, latent, need_q, tm):
    n = x.shape[0]
    tiles_per_batch = (SEQ // tm) if latent else None
    tok = pl.BlockSpec((tm, D_MODEL), lambda t: (t, 0))
    out_specs, out_shapes = [], []
    if need_q:
        out_specs.append(tok)
        out_shapes.append(jax.ShapeDtypeStruct((n, D_MODEL), BF16))
    if latent:
        rows = tm // GRID_W
        kv_spec = pl.BlockSpec((rows, NA_NCB, NA_KB, D_MODEL), lambda t: (t, 0, 0, 0))
        kv_shape = jax.ShapeDtypeStruct((n // GRID_W, NA_NCB, NA_KB, D_MODEL), BF16)
    else:
        kv_spec, kv_shape = tok, jax.ShapeDtypeStruct((n, D_MODEL), BF16)
    out_specs += [kv_spec, kv_spec]
    out_shapes += [kv_shape, kv_shape]
    return pl.pallas_call(
        functools.partial(_qkv_kernel, need_q=need_q, gather=latent),
        grid=(n // tm,),
        in_specs=[
            tok,
            _vec_spec(D_MODEL, layer),
            _mod_spec(layer, 0, tiles_per_batch),
            _mod_spec(layer, 1, tiles_per_batch),
            _resident((D_MODEL, 3 * D_MODEL), mixer),
            _vec_spec(D_MODEL, mixer),
            _vec_spec(D_MODEL, mixer),
        ],
        out_specs=out_specs,
        out_shape=out_shapes,
        compiler_params=_params(1),
        name="qkv_lat" if latent else "qkv_ctx",
    )(x, norm_g, mods, mods, w_qkv, q_gain, k_gain)


def _pair_scores(q2, keys, bias_pair=None, shift=None):
    lane = lax.broadcasted_iota(jnp.int32, (1, LANES), 1)
    first = lane < NA_HEAD_DIM
    zero = jnp.zeros_like(q2)
    q4 = jnp.concatenate([jnp.where(first, q2, zero), jnp.where(first, zero, q2)], axis=0)
    s = _dot_nt(q4, keys)
    if bias_pair is None:
        return s
    nb = bias_pair[0].shape[1]
    rest = [s[:, i:i + LANES] - shift for i in range(nb, s.shape[1], LANES)]
    return jnp.concatenate([s[:, :nb] + jnp.concatenate(bias_pair, axis=0)] + rest, axis=1)


def _pair_softmax(s, row_max=True):
    tiles = [s[:, i:i + LANES] for i in range(0, s.shape[1], LANES)]
    if row_max:
        mx = jnp.max(functools.reduce(jnp.maximum, tiles), axis=-1, keepdims=True)
        tiles = [t - mx for t in tiles]
    probs = [jnp.exp2(t) for t in tiles]
    denom = jnp.sum(functools.reduce(lambda a, b: a + b, probs), axis=-1, keepdims=True)
    return jnp.concatenate([p.astype(BF16) for p in probs], axis=1), denom


def _pair_pv(p, denom, values):
    m_rows = p.shape[0] // 2
    o = _dot(p, values) / denom
    lane = lax.broadcasted_iota(jnp.int32, (1, LANES), 1)
    return jnp.where(lane < NA_HEAD_DIM, o[:m_rows], o[m_rows:])


def _attn_lat_kernel(q_ref, k0, k1, k2, k3, v0, v1, v2, v3, kc_ref, vc_ref, bias_ref, shift_ref,
                     o_ref, *, row_max):
    kv_rows = KEY_ROW_GROUP * NA_KB

    def lanes(hp):
        return slice(hp * LANES, (hp + 1) * LANES)

    def gather(refs, ctx_ref, t):
        bi, hp = divmod(t, HEAD_PAIRS)
        return jnp.concatenate(
            [r[bi, :, :, lanes(hp)].reshape(kv_rows, LANES) for r in refs] + [ctx_ref[bi, :, lanes(hp)]],
            axis=0)

    def scores(t):
        bi, hp = divmod(t, HEAD_PAIRS)
        q2 = q_ref[bi, :, :, lanes(hp)].reshape(Q_PATCH, LANES)
        return _pair_scores(q2, gather((k0, k1, k2, k3), kc_ref, t),
                            (bias_ref[2 * hp], bias_ref[2 * hp + 1]), shift_ref[...])

    n_stages = q_ref.shape[0] * HEAD_PAIRS
    s_next, prev = scores(0), None
    for t in range(n_stages + 1):
        s = s_next
        if t + 1 < n_stages:
            s_next = scores(t + 1)
        if prev is not None:
            bi, hp = divmod(t - 1, HEAD_PAIRS)
            o = _pair_pv(*prev, gather((v0, v1, v2, v3), vc_ref, t - 1))
            o_ref[bi, :, :, lanes(hp)] = o.reshape(Q_ROWS, NA_KW, LANES).astype(BF16)
        if t < n_stages:
            prev = _pair_softmax(s, row_max)


def _key_group_start(rb):
    return jnp.clip(2 * rb - 1, 0, GRID_ROWS // KEY_ROW_GROUP - KEY_GROUPS)


def _attn_lat(q, k, v, kc, vc, bias, attn_layer, logit_bound):
    n_rb = GRID_ROWS // Q_ROWS
    nb = ATTN_BATCHES
    q4 = q.reshape(BATCH, GRID_ROWS, GRID_W, D_MODEL)
    k5 = k.reshape(BATCH, GRID_ROWS, NA_NCB, NA_KB, D_MODEL)
    v5 = v.reshape(BATCH, GRID_ROWS, NA_NCB, NA_KB, D_MODEL)
    kc3 = kc.reshape(BATCH, CTX_LEN, D_MODEL)
    vc3 = vc.reshape(BATCH, CTX_LEN, D_MODEL)
    q_spec = pl.BlockSpec((nb, Q_ROWS, NA_KW, D_MODEL), lambda rb, j, b: (b, rb, j, 0))

    def kv_spec(i):
        return pl.BlockSpec(
            (nb, KEY_ROW_GROUP, None, NA_KB, D_MODEL),
            lambda rb, j, b: (b, _key_group_start(rb) + i, j, 0, 0))

    ctx_spec = pl.BlockSpec((nb, CTX_LEN, D_MODEL), lambda rb, j, b: (b, 0, 0))

    def bias_index(rb, j, b):
        variant = (rb > 0).astype(jnp.int32) + (rb == n_rb - 1).astype(jnp.int32)
        return (variant * NA_NCB + j, attn_layer, 0, 0)

    bias_spec = pl.BlockSpec((None, NA_HEADS, Q_PATCH, LAT_KEYS), bias_index)
    shift = jnp.full((1, LANES), logit_bound, F32)

    def run(row_max):
        return pl.pallas_call(
            functools.partial(_attn_lat_kernel, row_max=row_max),
            grid=(n_rb, NA_NCB, BATCH // nb),
            in_specs=([q_spec] + [kv_spec(i) for i in range(KEY_GROUPS)] * 2
                      + [ctx_spec, ctx_spec, bias_spec, pl.BlockSpec((1, LANES), lambda rb, j, b: (0, 0))]),
            out_specs=q_spec,
            out_shape=jax.ShapeDtypeStruct(q4.shape, BF16),
            compiler_params=_params(3),
            name="attn_lat_rowmax" if row_max else "attn_lat",
        )(q4, k5, k5, k5, k5, v5, v5, v5, v5, kc3, vc3, bias, shift)

    out = lax.cond(logit_bound <= EXP2_SAFE_BOUND, lambda: run(False), lambda: run(True))
    return out.reshape(BATCH * SEQ, D_MODEL)


def _attn_ctx_kernel(q_ref, k_ref, v_ref, o_ref):
    for hp in range(HEAD_PAIRS):
        sl = slice(hp * LANES, (hp + 1) * LANES)
        p, denom = _pair_softmax(_pair_scores(q_ref[:, sl], k_ref[:, sl], None))
        o_ref[:, sl] = _pair_pv(p, denom, v_ref[:, sl]).astype(BF16)


def _attn_ctx(qc, kc, vc):
    spec = pl.BlockSpec((CTX_LEN, D_MODEL), lambda b: (b, 0))
    return pl.pallas_call(
        _attn_ctx_kernel,
        grid=(BATCH,),
        in_specs=[spec, spec, spec],
        out_specs=spec,
        out_shape=jax.ShapeDtypeStruct(qc.shape, BF16),
        compiler_params=_params(1),
        name="attn_ctx",
    )(qc, kc, vc)


def _attention_bias_tables(rpb):
    n_rb = GRID_ROWS // Q_ROWS
    n_heads = rpb.shape[0]
    col_pad, row_pad = NA_KW, Q_ROWS
    rpb_p = jnp.pad(rpb, ((0, 0), (row_pad, row_pad), (col_pad, col_pad)))
    col_off = np.array([_key_col_start(j) - j * NA_KW + NA_KW - 1 + col_pad for j in range(NA_NCB)])
    src_col = (np.arange(NA_KB)[None, None, :] - np.arange(NA_KW)[None, :, None]
               + col_off[:, None, None])
    pick = (src_col[..., None] == np.arange(rpb_p.shape[2])).astype(np.float32)
    by_col = jnp.einsum('hab,jqmb->jhqam', rpb_p, pick, precision=lax.Precision.HIGHEST)
    by_col = by_col.reshape(NA_NCB, n_heads, NA_KW, -1)
    by_col = jnp.pad(by_col, ((0, 0), (0, 0), (0, 0), (0, BIAS_SRC_LANES - by_col.shape[3])))
    valid, row_offs = [], []
    for rb in (0, 1, n_rb - 1):
        g0 = int(np.clip(2 * rb - 1, 0, GRID_ROWS // KEY_ROW_GROUP - KEY_GROUPS))
        row_offs.append(g0 * KEY_ROW_GROUP - rb * Q_ROWS + NA_KH - 1 + row_pad)
        for j in range(NA_NCB):
            qrow = (rb * Q_ROWS + np.arange(Q_ROWS))[:, None, None, None]
            qcol = (j * NA_KW + np.arange(NA_KW))[None, :, None, None]
            krow = (g0 * KEY_ROW_GROUP + np.arange(KEY_ROWS))[None, None, :, None]
            kcol = (_key_col_start(j) + np.arange(NA_KB))[None, None, None, :]
            rs = np.clip(qrow - NA_KH // 2, 0, GRID_ROWS - NA_KH)
            ws = np.clip(qcol - NA_KW // 2, 0, GRID_W - NA_KW)
            ok = (krow >= rs) & (krow < rs + NA_KH) & (kcol >= ws) & (kcol < ws + NA_KW)
            valid.append(np.broadcast_to(ok, (Q_ROWS, NA_KW, KEY_ROWS, NA_KB)).reshape(Q_PATCH, LAT_KEYS))
    valid = jnp.asarray(np.stack(valid).reshape(3, NA_NCB, Q_PATCH, LAT_KEYS), F32)

    heads_per_step = 8
    out = pl.pallas_call(
        functools.partial(_bias_table_kernel, row_offs=tuple(row_offs)),
        grid=(NA_NCB, n_heads // heads_per_step),
        in_specs=[
            pl.BlockSpec((None, heads_per_step, NA_KW, BIAS_SRC_LANES), lambda j, hb: (j, hb, 0, 0)),
            pl.BlockSpec((3, None, Q_PATCH, LAT_KEYS), lambda j, hb: (0, j, 0, 0)),
        ],
        out_specs=pl.BlockSpec((3, None, heads_per_step, Q_PATCH, LAT_KEYS),
                               lambda j, hb: (0, j, hb, 0, 0)),
        out_shape=jax.ShapeDtypeStruct((3, NA_NCB, n_heads, Q_PATCH, LAT_KEYS), F32),
        compiler_params=_params(2),
        name="bias_table",
    )(by_col, valid)
    return out.reshape(3 * NA_NCB, n_heads, Q_PATCH, LAT_KEYS)


def _bias_table_kernel(src_ref, valid_ref, o_ref, *, row_offs):
    for v, row_off in enumerate(row_offs):
        for ql in range(Q_ROWS):
            start = (row_off - ql) * NA_KB
            rows = slice(ql * NA_KW, (ql + 1) * NA_KW)
            ok = valid_ref[v, rows, :] > 0.0
            for h in range(src_ref.shape[0]):
                src = src_ref[h]
                if start:
                    src = pltpu.roll(src, BIAS_SRC_LANES - start, axis=1)
                o_ref[v, h, rows, :] = jnp.where(ok, src[:, :LAT_KEYS], NEG_INF)


def _mlp_kernel(*refs, with_proj):
    if with_proj:
        x_ref, a_ref, wo_ref, g1_ref, refs = refs[0], refs[1], refs[2], refs[3], refs[4:]
        x = x_ref[...] + g1_ref[...] * _dot(a_ref[...], wo_ref[...])
    else:
        x_ref, refs = refs[0], refs[1:]
        x = x_ref[...]
    n_ref, sh_ref, sc_ref, g2_ref, w1_ref, w2_ref, o_ref = refs
    h = _rms_mod(x, n_ref[...], sh_ref[...], sc_ref[...]).astype(BF16)
    t = jnp.maximum(_dot(h, w1_ref[...]), 0.0)
    t = (t * t).astype(BF16)
    o_ref[...] = x + g2_ref[...] * _dot(t, w2_ref[...])


def _mlp(x, mods, layer, norm_g, w1, w2, *, latent, tm, attn=None, w_o=None, mixer=None):
    n = x.shape[0]
    tiles_per_batch = (SEQ // tm) if latent else None
    tok = pl.BlockSpec((tm, D_MODEL), lambda t: (t, 0))
    with_proj = attn is not None
    in_specs, args = [tok], [x]
    if with_proj:
        in_specs += [tok, _resident((D_MODEL, D_MODEL), mixer), _mod_spec(layer, 2, tiles_per_batch)]
        args += [attn, w_o, mods]
    in_specs += [
        _vec_spec(D_MODEL, layer),
        _mod_spec(layer, 3, tiles_per_batch),
        _mod_spec(layer, 4, tiles_per_batch),
        _mod_spec(layer, 5, tiles_per_batch),
        _resident((D_MODEL, MLP_HIDDEN), layer),
        _resident((MLP_HIDDEN, D_MODEL), layer),
    ]
    args += [norm_g, mods, mods, mods, w1, w2]
    return pl.pallas_call(
        functools.partial(_mlp_kernel, with_proj=with_proj),
        grid=(n // tm,),
        in_specs=in_specs,
        out_specs=tok,
        out_shape=jax.ShapeDtypeStruct((n, D_MODEL), F32),
        compiler_params=_params(1),
        name=("proj_mlp" if with_proj else "mlp") + ("_lat" if latent else "_ctx"),
    )(*args)


SG_COLS = 1024


def _sg_kernel(x_ref, n_ref, sh_ref, sc_ref, g1_ref, win_ref, bin_ref, lng_ref, lnb_ref,
               ws_ref, bs_ref, wo_ref, o_ref, v_scr, t_scr):
    x = x_ref[...]
    tm = x.shape[0]
    n_chunks = tm // SG_CHUNK
    h = _rms_mod(x, n_ref[...], sh_ref[...], sc_ref[...]).astype(BF16)
    n_steps = SG_HALF // SG_COLS
    groups_per_step = SG_COLS // SG_GROUP_CH

    def cols(c):
        return slice(c * SG_COLS, (c + 1) * SG_COLS)

    def lane_tile_sum(a):
        return functools.reduce(lambda p, q: p + q,
                                [a[:, i:i + LANES] for i in range(0, a.shape[1], LANES)])

    s1 = s2 = None
    for c in range(n_steps):
        vc = slice(SG_HALF + c * SG_COLS, SG_HALF + (c + 1) * SG_COLS)
        v_blk = _gelu(_dot(h, win_ref[:, vc]) + bin_ref[:, vc])
        v_scr[:, cols(c)] = v_blk
        p1, p2 = lane_tile_sum(v_blk), lane_tile_sum(v_blk * v_blk)
        s1, s2 = (p1, p2) if s1 is None else (s1 + p1, s2 + p2)
    mu = jnp.sum(s1, axis=-1, keepdims=True) * (1.0 / SG_HALF)
    var = jnp.sum(s2, axis=-1, keepdims=True) * (1.0 / SG_HALF) - mu * mu
    rstd = lax.rsqrt(var + EPS)

    for c in range(n_steps):
        u = _gelu((_dot(h, win_ref[:, cols(c)]) + bin_ref[:, cols(c)]).astype(BF16))
        vn = ((v_scr[:, cols(c)] - mu) * rstd * lng_ref[:, cols(c)] + lnb_ref[:, cols(c)]).astype(BF16)
        for gg in range(groups_per_step):
            g = c * groups_per_step + gg
            gl = slice(gg * SG_GROUP_CH, (gg + 1) * SG_GROUP_CH)
            rhs = jnp.concatenate(
                [vn[i * SG_CHUNK:(i + 1) * SG_CHUNK, gl] for i in range(n_chunks)], axis=1)
            s = _dot(ws_ref[g], rhs)
            for i in range(n_chunks):
                rows = slice(i * SG_CHUNK, (i + 1) * SG_CHUNK)
                s_i = s[:, i * SG_GROUP_CH:(i + 1) * SG_GROUP_CH] + bs_ref[g]
                t_scr[rows, g * SG_GROUP_CH:(g + 1) * SG_GROUP_CH] = u[rows, gl] * s_i.astype(BF16)
    o_ref[...] = x + g1_ref[...] * _dot(t_scr[...], wo_ref[...])


def _sg(x, mods, layer, norm_g, mixer, w_in, b_in, ln_g, ln_b, w_s, b_s, w_o, *, latent, tm):
    n = x.shape[0]
    tiles_per_batch = (SEQ // tm) if latent else None
    tok = pl.BlockSpec((tm, D_MODEL), lambda t: (t, 0))
    return pl.pallas_call(
        _sg_kernel,
        grid=(n // tm,),
        in_specs=[
            tok,
            _vec_spec(D_MODEL, layer),
            _mod_spec(layer, 0, tiles_per_batch),
            _mod_spec(layer, 1, tiles_per_batch),
            _mod_spec(layer, 2, tiles_per_batch),
            _resident((D_MODEL, 2 * SG_HALF), mixer),
            _vec_spec(2 * SG_HALF, mixer),
            _vec_spec(SG_HALF, mixer),
            _vec_spec(SG_HALF, mixer),
            _resident((SG_GROUPS, SG_CHUNK, SG_CHUNK), mixer),
            _resident((SG_GROUPS, SG_CHUNK, SG_GROUP_CH), mixer),
            _resident((SG_HALF, D_MODEL), mixer),
        ],
        out_specs=tok,
        out_shape=jax.ShapeDtypeStruct((n, D_MODEL), F32),
        scratch_shapes=[pltpu.VMEM((tm, SG_HALF), F32), pltpu.VMEM((tm, SG_HALF), BF16)],
        compiler_params=_params(1),
        name="sg_lat" if latent else "sg_ctx",
    )(x, norm_g, mods, mods, mods, w_in, b_in, ln_g, ln_b, w_s, b_s, w_o)


LAT_TM = 1024
SG_TM = 512
CTX_TM = 512


def kernel(x, c, ctx, c_ctx, ada_w, ada_b, norm1_g, norm2_g, mlp_w1, mlp_w2,
           na_w_qkv, na_q_norm, na_k_norm, na_rpb, na_w_o,
           sg_w_in, sg_b_in, sg_ln_g, sg_ln_b, sg_w_s, sg_b_s, sg_w_o):
    last_ctx_layer = ((DEPTH - 1) // N_MIXERS) * N_MIXERS
    xl = x.reshape(BATCH * SEQ, D_MODEL)
    xc = ctx.reshape(BATCH * CTX_LEN, D_MODEL)

    cond = jnp.concatenate(
        [c, c_ctx[None, :], jnp.zeros((MOD_ROWS - BATCH - 1, D_MODEL), F32)], axis=0)
    mods = _adaln(cond, ada_w, ada_b).reshape(DEPTH * MOD_ROWS * 6, 1, D_MODEL)

    q_gains = na_q_norm * (NA_HEAD_DIM ** -0.5 * LOG2_E)
    rpb2 = na_rpb * LOG2_E
    logit_bounds = (NA_HEAD_DIM * BF16_DOT_SLACK * jnp.max(jnp.abs(q_gains), axis=1)
                    * jnp.max(jnp.abs(na_k_norm), axis=1) + jnp.max(jnp.abs(rpb2), axis=(1, 2, 3)))
    bias = _attention_bias_tables(
        (rpb2 - logit_bounds[:, None, None, None]).reshape((-1,) + na_rpb.shape[2:]))

    w1, w2 = mlp_w1.astype(BF16), mlp_w2.astype(BF16)
    w_qkv, w_ao = na_w_qkv.astype(BF16), na_w_o.astype(BF16)
    w_in, w_s, w_so = sg_w_in.astype(BF16), sg_w_s.astype(BF16), sg_w_o.astype(BF16)

    n1, n2 = norm1_g[:, None, :], norm2_g[:, None, :]
    q_gain = jnp.tile(q_gains, (1, NA_HEADS))[:, None, :]
    k_gain = jnp.tile(na_k_norm, (1, NA_HEADS))[:, None, :]
    sg_vecs = (sg_b_in[:, None, :], sg_ln_g[:, None, :], sg_ln_b[:, None, :])
    sg_bias = jnp.broadcast_to(sg_b_s[:, :, :, None], sg_b_s.shape + (SG_GROUP_CH,))

    for i in range(DEPTH):
        ctx_full = i < last_ctx_layer
        if i % N_MIXERS == 0:
            a = i // N_MIXERS
            q, k, v = _qkv(xl, mods, i, n1, w_qkv, a, q_gain, k_gain,
                           latent=True, need_q=True, tm=LAT_TM)
            ctx_proj = _qkv(xc, mods, i, n1, w_qkv, a, q_gain, k_gain,
                            latent=False, need_q=ctx_full, tm=CTX_TM)
            kc, vc = ctx_proj[-2], ctx_proj[-1]
            att = _attn_lat(q, k, v, kc, vc, bias, a, logit_bounds[a])
            xl = _mlp(xl, mods, i, n2, w1, w2, latent=True, tm=LAT_TM, attn=att, w_o=w_ao, mixer=a)
            if ctx_full:
                att_c = _attn_ctx(ctx_proj[0], kc, vc)
                xc = _mlp(xc, mods, i, n2, w1, w2, latent=False, tm=CTX_TM, attn=att_c, w_o=w_ao,
                          mixer=a)
        else:
            s = i // N_MIXERS
            sg_args = (s, w_in, *sg_vecs, w_s, sg_bias, w_so)
            xl = _sg(xl, mods, i, n1, *sg_args, latent=True, tm=SG_TM)
            xl = _mlp(xl, mods, i, n2, w1, w2, latent=True, tm=LAT_TM)
            if ctx_full:
                xc = _sg(xc, mods, i, n1, *sg_args, latent=False, tm=CTX_TM)
                xc = _mlp(xc, mods, i, n2, w1, w2, latent=False, tm=CTX_TM)
    return xl.reshape(BATCH, SEQ, D_MODEL)
```

```python
import functools

import numpy as np
import jax
import jax.numpy as jnp
from jax import lax
from jax.experimental import pallas as pl
from jax.experimental.pallas import tpu as pltpu

D_MODEL = 1024
BATCH = 8
SEQ = 4096
DEPTH = 4
GRID_W = 64
GRID_ROWS = SEQ // GRID_W
CTX_LEN = 256
N_MIXERS = 2
NA_HEADS = 16
NA_HEAD_DIM = D_MODEL // NA_HEADS
NA_KH = 8
NA_KW = 16
NA_KB = 2 * NA_KW
NA_NCB = GRID_W // NA_KW
SG_CHUNK = 128
SG_HALF = 3 * D_MODEL
SG_GROUP_CH = 128
SG_GROUPS = SG_HALF // SG_GROUP_CH
MLP_HIDDEN = 4 * D_MODEL
EPS = 1e-6
NEG_INF = -1e30
LOG2_E = float(np.log2(np.e))
EXP2_SAFE_BOUND = 60.0

F32 = jnp.float32
BF16 = jnp.bfloat16

LANES = 128
HEAD_PAIRS = NA_HEADS // 2
MOD_ROWS = 16
CTX_MOD_ROW = BATCH
Q_ROWS = 8
Q_PATCH = Q_ROWS * NA_KW
KEY_ROW_GROUP = 4
KEY_GROUPS = 4
KEY_ROWS = KEY_ROW_GROUP * KEY_GROUPS
LAT_KEYS = KEY_ROWS * NA_KB
ATTN_BATCHES = 4
BIAS_ROW_SHIFTS = 2 * NA_KH + 2 * Q_ROWS
BIAS_SRC_LANES = BIAS_ROW_SHIFTS * NA_KB
BF16_DOT_SLACK = 1.01
VMEM_LIMIT = 56 * 1024 * 1024


def _dot(a, b):
    return jnp.dot(a, b, preferred_element_type=F32)


def _dot_nt(a, b):
    return lax.dot_general(a, b, (((1,), (1,)), ((), ())), preferred_element_type=F32)


def _rms_mod(x, g, shift, scale):
    ms = jnp.mean(x * x, axis=-1, keepdims=True)
    h = x * lax.rsqrt(ms + EPS) * g
    return h * (1.0 + scale) + shift


def _gelu(z):
    half = np.asarray(0.5, z.dtype)
    return half * z * (np.asarray(1.0, z.dtype) + lax.erf(z * np.asarray(np.sqrt(0.5), z.dtype)))


def _split_bf16(a):
    hi = a.astype(BF16)
    lo = (a - hi.astype(F32)).astype(BF16)
    return hi, lo


def _params(n_axes):
    return pltpu.CompilerParams(dimension_semantics=("arbitrary",) * n_axes,
                                vmem_limit_bytes=VMEM_LIMIT)


def _resident(shape, stack_index=None):
    nd = len(shape)
    if stack_index is None:
        return pl.BlockSpec(shape, lambda *_: (0,) * nd, pipeline_mode=pl.Buffered(1))
    return pl.BlockSpec((None,) + tuple(shape), lambda *_: (stack_index,) + (0,) * nd,
                        pipeline_mode=pl.Buffered(1))


def _adaln_kernel(c_ref, w_ref, b_ref, o_ref):
    c = c_ref[...]
    s = c * (1.0 / (1.0 + jnp.exp(-c)))
    s_hi, s_lo = _split_bf16(s)
    w_hi, w_lo = _split_bf16(w_ref[...])
    o_ref[...] = _dot(s_hi, w_hi) + (_dot(s_lo, w_hi) + _dot(s_hi, w_lo)) + b_ref[...]


def _adaln(cond, ada_w, ada_b):
    tn = 2 * D_MODEL
    return pl.pallas_call(
        _adaln_kernel,
        grid=(DEPTH, 6 * D_MODEL // tn),
        in_specs=[
            pl.BlockSpec((MOD_ROWS, D_MODEL), lambda i, n: (0, 0)),
            pl.BlockSpec((None, D_MODEL, tn), lambda i, n: (i, 0, n)),
            pl.BlockSpec((None, 1, tn), lambda i, n: (i, 0, n)),
        ],
        out_specs=pl.BlockSpec((None, MOD_ROWS, tn), lambda i, n: (i, 0, n)),
        out_shape=jax.ShapeDtypeStruct((DEPTH, MOD_ROWS, 6 * D_MODEL), F32),
        compiler_params=_params(2),
        name="adaln",
    )(cond, ada_w, ada_b.reshape(DEPTH, 1, 6 * D_MODEL))


def _mod_spec(layer, k, tiles_per_batch):
    base = layer * MOD_ROWS * 6 + k
    if tiles_per_batch is None:
        return pl.BlockSpec((None, 1, D_MODEL), lambda t: (base + CTX_MOD_ROW * 6, 0, 0))
    return pl.BlockSpec((None, 1, D_MODEL), lambda t: (base + (t // tiles_per_batch) * 6, 0, 0))


def _vec_spec(n, row):
    return pl.BlockSpec((None, 1, n), lambda t: (row, 0, 0))


def _key_col_start(j):
    return int(np.clip(j * NA_KW - NA_KW // 2, 0, GRID_W - NA_KB))


def _qkv_kernel(x_ref, g_ref, sh_ref, sc_ref, w_ref, qg_ref, kg_ref, *out_refs, need_q, gather):
    h = _rms_mod(x_ref[...], g_ref[...], sh_ref[...], sc_ref[...]).astype(BF16)
    first = lax.broadcasted_iota(jnp.int32, (1, LANES), 1) < NA_HEAD_DIM

    def head_norm(y, gain):
        parts = []
        for c in range(0, D_MODEL, LANES):
            t = y[:, c:c + LANES]
            sq = t * t
            s_first = jnp.sum(jnp.where(first, sq, 0.0), axis=-1, keepdims=True)
            s_both = jnp.sum(sq, axis=-1, keepdims=True)
            ms = jnp.where(first, s_first, s_both - s_first) * (1.0 / NA_HEAD_DIM)
            parts.append(t * lax.rsqrt(ms + EPS))
        return jnp.concatenate(parts, axis=1) * gain

    refs = list(out_refs)
    if need_q:
        refs.pop(0)[...] = head_norm(_dot(h, w_ref[:, 0:D_MODEL]), qg_ref[...]).astype(BF16)
    k_ref, v_ref = refs
    k = head_norm(_dot(h, w_ref[:, D_MODEL:2 * D_MODEL]), kg_ref[...])
    v = _dot(h, w_ref[:, 2 * D_MODEL:3 * D_MODEL])
    if not gather:
        k_ref[...] = k.astype(BF16)
        v_ref[...] = v.astype(BF16)
        return
    rows = k.shape[0] // GRID_W
    k3 = k.reshape(rows, GRID_W, D_MODEL)
    v3 = v.reshape(rows, GRID_W, D_MODEL)
    for j in range(NA_NCB):
        c0 = _key_col_start(j)
        k_ref[:, j, :, :] = k3[:, c0:c0 + NA_KB, :].astype(BF16)
        v_ref[:, j, :, :] = v3[:, c0:c0 + NA_KB, :].astype(BF16)


def _qkv(x, mods, layer, norm_g, w_qkv, mixer, q_gain, k_gain, *, latent, need_q, tm):
    n = x.shape[0]
    tiles_per_batch = (SEQ // tm) if latent else None
    tok = pl.BlockSpec((tm, D_MODEL), lambda t: (t, 0))
    out_specs, out_shapes = [], []
    if need_q:
        out_specs.append(tok)
        out_shapes.append(jax.ShapeDtypeStruct((n, D_MODEL), BF16))
    if latent:
        rows = tm // GRID_W
        kv_spec = pl.BlockSpec((rows, NA_NCB, NA_KB, D_MODEL), lambda t: (t, 0, 0, 0))
        kv_shape = jax.ShapeDtypeStruct((n // GRID_W, NA_NCB, NA_KB, D_MODEL), BF16)
    else:
        kv_spec, kv_shape = tok, jax.ShapeDtypeStruct((n, D_MODEL), BF16)
    out_specs += [kv_spec, kv_spec]
    out_shapes += [kv_shape, kv_shape]
    return pl.pallas_call(
        functools.partial(_qkv_kernel, need_q=need_q, gather=latent),
        grid=(n // tm,),
        in_specs=[
            tok,
            _vec_spec(D_MODEL, layer),
            _mod_spec(layer, 0, tiles_per_batch),
            _mod_spec(layer, 1, tiles_per_batch),
            _resident((D_MODEL, 3 * D_MODEL), mixer),
            _vec_spec(D_MODEL, mixer),
            _vec_spec(D_MODEL, mixer),
        ],
        out_specs=out_specs,
        out_shape=out_shapes,
        compiler_params=_params(1),
        name="qkv_lat" if latent else "qkv_ctx",
    )(x, norm_g, mods, mods, w_qkv, q_gain, k_gain)


def _pair_scores(q2, keys, bias_pair=None, shift=None):
    lane = lax.broadcasted_iota(jnp.int32, (1, LANES), 1)
    first = lane < NA_HEAD_DIM
    zero = jnp.zeros_like(q2)
    q4 = jnp.concatenate([jnp.where(first, q2, zero), jnp.where(first, zero, q2)], axis=0)
    s = _dot_nt(q4, keys)
    if bias_pair is None:
        return s
    nb = bias_pair[0].shape[1]
    rest = [s[:, i:i + LANES] - shift for i in range(nb, s.shape[1], LANES)]
    return jnp.concatenate([s[:, :nb] + jnp.concatenate(bias_pair, axis=0)] + rest, axis=1)


def _pair_softmax(s, row_max=True):
    tiles = [s[:, i:i + LANES] for i in range(0, s.shape[1], LANES)]
    if row_max:
        mx = jnp.max(functools.reduce(jnp.maximum, tiles), axis=-1, keepdims=True)
        tiles = [t - mx for t in tiles]
    probs = [jnp.exp2(t) for t in tiles]
    denom = jnp.sum(functools.reduce(lambda a, b: a + b, probs), axis=-1, keepdims=True)
    return jnp.concatenate([p.astype(BF16) for p in probs], axis=1), denom


def _pair_pv(p, denom, values):
    m_rows = p.shape[0] // 2
    ot = lax.dot_general(values, p, (((0,), (1,)), ((), ())), preferred_element_type=F32)
    row = lax.broadcasted_iota(jnp.int32, (LANES, 1), 0)
    o = jnp.where(row < NA_HEAD_DIM, ot[:, :m_rows], ot[:, m_rows:]).T
    lane = lax.broadcasted_iota(jnp.int32, (1, LANES), 1)
    return o / jnp.where(lane < NA_HEAD_DIM, denom[:m_rows], denom[m_rows:])


def _attn_lat_kernel(q_ref, k0, k1, k2, k3, v0, v1, v2, v3, kc_ref, vc_ref, bias_ref, shift_ref,
                     o_ref, *, row_max):
    kv_rows = KEY_ROW_GROUP * NA_KB

    def lanes(hp):
        return slice(hp * LANES, (hp + 1) * LANES)

    def gather(refs, ctx_ref, t):
        bi, hp = divmod(t, HEAD_PAIRS)
        return jnp.concatenate(
            [r[bi, :, :, lanes(hp)].reshape(kv_rows, LANES) for r in refs] + [ctx_ref[bi, :, lanes(hp)]],
            axis=0)

    def scores(t):
        bi, hp = divmod(t, HEAD_PAIRS)
        q2 = q_ref[bi, :, :, lanes(hp)].reshape(Q_PATCH, LANES)
        return _pair_scores(q2, gather((k0, k1, k2, k3), kc_ref, t),
                            (bias_ref[2 * hp], bias_ref[2 * hp + 1]), shift_ref[...])

    n_stages = q_ref.shape[0] * HEAD_PAIRS
    s_next, prev = scores(0), None
    for t in range(n_stages + 1):
        s = s_next
        if t + 1 < n_stages:
            s_next = scores(t + 1)
        if prev is not None:
            bi, hp = divmod(t - 1, HEAD_PAIRS)
            o = _pair_pv(*prev, gather((v0, v1, v2, v3), vc_ref, t - 1))
            o_ref[bi, :, :, lanes(hp)] = o.reshape(Q_ROWS, NA_KW, LANES).astype(BF16)
        if t < n_stages:
            prev = _pair_softmax(s, row_max)


def _key_group_start(rb):
    return jnp.clip(2 * rb - 1, 0, GRID_ROWS // KEY_ROW_GROUP - KEY_GROUPS)


def _attn_lat(q, k, v, kc, vc, bias, attn_layer, logit_bound):
    n_rb = GRID_ROWS // Q_ROWS
    nb = ATTN_BATCHES
    q4 = q.reshape(BATCH, GRID_ROWS, GRID_W, D_MODEL)
    k5 = k.reshape(BATCH, GRID_ROWS, NA_NCB, NA_KB, D_MODEL)
    v5 = v.reshape(BATCH, GRID_ROWS, NA_NCB, NA_KB, D_MODEL)
    kc3 = kc.reshape(BATCH, CTX_LEN, D_MODEL)
    vc3 = vc.reshape(BATCH, CTX_LEN, D_MODEL)
    q_spec = pl.BlockSpec((nb, Q_ROWS, NA_KW, D_MODEL), lambda rb, j, b: (b, rb, j, 0))

    def kv_spec(i):
        return pl.BlockSpec(
            (nb, KEY_ROW_GROUP, None, NA_KB, D_MODEL),
            lambda rb, j, b: (b, _key_group_start(rb) + i, j, 0, 0))

    ctx_spec = pl.BlockSpec((nb, CTX_LEN, D_MODEL), lambda rb, j, b: (b, 0, 0))

    def bias_index(rb, j, b):
        variant = (rb > 0).astype(jnp.int32) + (rb == n_rb - 1).astype(jnp.int32)
        return (variant * NA_NCB + j, attn_layer, 0, 0)

    bias_spec = pl.BlockSpec((None, NA_HEADS, Q_PATCH, LAT_KEYS), bias_index)
    shift = jnp.full((1, LANES), logit_bound, F32)

    def run(row_max):
        return pl.pallas_call(
            functools.partial(_attn_lat_kernel, row_max=row_max),
            grid=(n_rb, NA_NCB, BATCH // nb),
            in_specs=([q_spec] + [kv_spec(i) for i in range(KEY_GROUPS)] * 2
                      + [ctx_spec, ctx_spec, bias_spec, pl.BlockSpec((1, LANES), lambda rb, j, b: (0, 0))]),
            out_specs=q_spec,
            out_shape=jax.ShapeDtypeStruct(q4.shape, BF16),
            compiler_params=_params(3),
            name="attn_lat_rowmax" if row_max else "attn_lat",
        )(q4, k5, k5, k5, k5, v5, v5, v5, v5, kc3, vc3, bias, shift)

    out = lax.cond(logit_bound <= EXP2_SAFE_BOUND, lambda: run(False), lambda: run(True))
    return out.reshape(BATCH * SEQ, D_MODEL)


def _attn_ctx_kernel(q_ref, k_ref, v_ref, o_ref):
    for hp in range(HEAD_PAIRS):
        sl = slice(hp * LANES, (hp + 1) * LANES)
        p, denom = _pair_softmax(_pair_scores(q_ref[:, sl], k_ref[:, sl], None))
        o_ref[:, sl] = _pair_pv(p, denom, v_ref[:, sl]).astype(BF16)


def _attn_ctx(qc, kc, vc):
    spec = pl.BlockSpec((CTX_LEN, D_MODEL), lambda b: (b, 0))
    return pl.pallas_call(
        _attn_ctx_kernel,
        grid=(BATCH,),
        in_specs=[spec, spec, spec],
        out_specs=spec,
        out_shape=jax.ShapeDtypeStruct(qc.shape, BF16),
        compiler_params=_params(1),
        name="attn_ctx",
    )(qc, kc, vc)


def _attention_bias_tables(rpb):
    n_rb = GRID_ROWS // Q_ROWS
    n_heads = rpb.shape[0]
    col_pad, row_pad = NA_KW, Q_ROWS
    rpb_p = jnp.pad(rpb, ((0, 0), (row_pad, row_pad), (col_pad, col_pad)))
    col_off = np.array([_key_col_start(j) - j * NA_KW + NA_KW - 1 + col_pad for j in range(NA_NCB)])
    src_col = (np.arange(NA_KB)[None, None, :] - np.arange(NA_KW)[None, :, None]
               + col_off[:, None, None])
    pick = (src_col[..., None] == np.arange(rpb_p.shape[2])).astype(np.float32)
    by_col = jnp.einsum('hab,jqmb->jhqam', rpb_p, pick, precision=lax.Precision.HIGHEST)
    by_col = by_col.reshape(NA_NCB, n_heads, NA_KW, -1)
    by_col = jnp.pad(by_col, ((0, 0), (0, 0), (0, 0), (0, BIAS_SRC_LANES - by_col.shape[3])))
    valid, row_offs = [], []
    for rb in (0, 1, n_rb - 1):
        g0 = int(np.clip(2 * rb - 1, 0, GRID_ROWS // KEY_ROW_GROUP - KEY_GROUPS))
        row_offs.append(g0 * KEY_ROW_GROUP - rb * Q_ROWS + NA_KH - 1 + row_pad)
        for j in range(NA_NCB):
            qrow = (rb * Q_ROWS + np.arange(Q_ROWS))[:, None, None, None]
            qcol = (j * NA_KW + np.arange(NA_KW))[None, :, None, None]
            krow = (g0 * KEY_ROW_GROUP + np.arange(KEY_ROWS))[None, None, :, None]
            kcol = (_key_col_start(j) + np.arange(NA_KB))[None, None, None, :]
            rs = np.clip(qrow - NA_KH // 2, 0, GRID_ROWS - NA_KH)
            ws = np.clip(qcol - NA_KW // 2, 0, GRID_W - NA_KW)
            ok = (krow >= rs) & (krow < rs + NA_KH) & (kcol >= ws) & (kcol < ws + NA_KW)
            valid.append(np.broadcast_to(ok, (Q_ROWS, NA_KW, KEY_ROWS, NA_KB)).reshape(Q_PATCH, LAT_KEYS))
    valid = jnp.asarray(np.stack(valid).reshape(3, NA_NCB, Q_PATCH, LAT_KEYS), F32)

    heads_per_step = 8
    out = pl.pallas_call(
        functools.partial(_bias_table_kernel, row_offs=tuple(row_offs)),
        grid=(NA_NCB, n_heads // heads_per_step),
        in_specs=[
            pl.BlockSpec((None, heads_per_step, NA_KW, BIAS_SRC_LANES), lambda j, hb: (j, hb, 0, 0)),
            pl.BlockSpec((3, None, Q_PATCH, LAT_KEYS), lambda j, hb: (0, j, 0, 0)),
        ],
        out_specs=pl.BlockSpec((3, None, heads_per_step, Q_PATCH, LAT_KEYS),
                               lambda j, hb: (0, j, hb, 0, 0)),
        out_shape=jax.ShapeDtypeStruct((3, NA_NCB, n_heads, Q_PATCH, LAT_KEYS), F32),
        compiler_params=_params(2),
        name="bias_table",
    )(by_col, valid)
    return out.reshape(3 * NA_NCB, n_heads, Q_PATCH, LAT_KEYS)


def _bias_table_kernel(src_ref, valid_ref, o_ref, *, row_offs):
    for v, row_off in enumerate(row_offs):
        for ql in range(Q_ROWS):
            start = (row_off - ql) * NA_KB
            rows = slice(ql * NA_KW, (ql + 1) * NA_KW)
            ok = valid_ref[v, rows, :] > 0.0
            for h in range(src_ref.shape[0]):
                src = src_ref[h]
                if start:
                    src = pltpu.roll(src, BIAS_SRC_LANES - start, axis=1)
                o_ref[v, h, rows, :] = jnp.where(ok, src[:, :LAT_KEYS], NEG_INF)


def _mlp_kernel(*refs, with_proj):
    if with_proj:
        x_ref, a_ref, wo_ref, g1_ref, refs = refs[0], refs[1], refs[2], refs[3], refs[4:]
        x = x_ref[...] + g1_ref[...] * _dot(a_ref[...], wo_ref[...])
    else:
        x_ref, refs = refs[0], refs[1:]
        x = x_ref[...]
    n_ref, sh_ref, sc_ref, g2_ref, w1_ref, w2_ref, o_ref = refs
    h = _rms_mod(x, n_ref[...], sh_ref[...], sc_ref[...]).astype(BF16)
    t = jnp.maximum(_dot(h, w1_ref[...]), 0.0)
    t = (t * t).astype(BF16)
    o_ref[...] = x + g2_ref[...] * _dot(t, w2_ref[...])


def _mlp(x, mods, layer, norm_g, w1, w2, *, latent, tm, attn=None, w_o=None, mixer=None):
    n = x.shape[0]
    tiles_per_batch = (SEQ // tm) if latent else None
    tok = pl.BlockSpec((tm, D_MODEL), lambda t: (t, 0))
    with_proj = attn is not None
    in_specs, args = [tok], [x]
    if with_proj:
        in_specs += [tok, _resident((D_MODEL, D_MODEL), mixer), _mod_spec(layer, 2, tiles_per_batch)]
        args += [attn, w_o, mods]
    in_specs += [
        _vec_spec(D_MODEL, layer),
        _mod_spec(layer, 3, tiles_per_batch),
        _mod_spec(layer, 4, tiles_per_batch),
        _mod_spec(layer, 5, tiles_per_batch),
        _resident((D_MODEL, MLP_HIDDEN), layer),
        _resident((MLP_HIDDEN, D_MODEL), layer),
    ]
    args += [norm_g, mods, mods, mods, w1, w2]
    return pl.pallas_call(
        functools.partial(_mlp_kernel, with_proj=with_proj),
        grid=(n // tm,),
        in_specs=in_specs,
        out_specs=tok,
        out_shape=jax.ShapeDtypeStruct((n, D_MODEL), F32),
        compiler_params=_params(1),
        name=("proj_mlp" if with_proj else "mlp") + ("_lat" if latent else "_ctx"),
    )(*args)


SG_COLS = 1024


def _sg_kernel(x_ref, n_ref, sh_ref, sc_ref, g1_ref, win_ref, bin_ref, lng_ref, lnb_ref,
               ws_ref, bs_ref, wo_ref, o_ref, v_scr, t_scr):
    x = x_ref[...]
    tm = x.shape[0]
    n_chunks = tm // SG_CHUNK
    h = _rms_mod(x, n_ref[...], sh_ref[...], sc_ref[...]).astype(BF16)
    n_steps = SG_HALF // SG_COLS
    groups_per_step = SG_COLS // SG_GROUP_CH

    def cols(c):
        return slice(c * SG_COLS, (c + 1) * SG_COLS)

    def lane_tile_sum(a):
        return functools.reduce(lambda p, q: p + q,
                                [a[:, i:i + LANES] for i in range(0, a.shape[1], LANES)])

    s1 = s2 = None
    for c in range(n_steps):
        vc = slice(SG_HALF + c * SG_COLS, SG_HALF + (c + 1) * SG_COLS)
        v_blk = _gelu(_dot(h, win_ref[:, vc]) + bin_ref[:, vc])
        v_scr[:, cols(c)] = v_blk
        p1, p2 = lane_tile_sum(v_blk), lane_tile_sum(v_blk * v_blk)
        s1, s2 = (p1, p2) if s1 is None else (s1 + p1, s2 + p2)
    mu = jnp.sum(s1, axis=-1, keepdims=True) * (1.0 / SG_HALF)
    var = jnp.sum(s2, axis=-1, keepdims=True) * (1.0 / SG_HALF) - mu * mu
    rstd = lax.rsqrt(var + EPS)

    for c in range(n_steps):
        u = _gelu((_dot(h, win_ref[:, cols(c)]) + bin_ref[:, cols(c)]).astype(BF16))
        vn = ((v_scr[:, cols(c)] - mu) * rstd * lng_ref[:, cols(c)] + lnb_ref[:, cols(c)]).astype(BF16)
        for gg in range(groups_per_step):
            g = c * groups_per_step + gg
            gl = slice(gg * SG_GROUP_CH, (gg + 1) * SG_GROUP_CH)
            rhs = jnp.concatenate(
                [vn[i * SG_CHUNK:(i + 1) * SG_CHUNK, gl] for i in range(n_chunks)], axis=1)
            s = _dot(ws_ref[g], rhs)
            for i in range(n_chunks):
                rows = slice(i * SG_CHUNK, (i + 1) * SG_CHUNK)
                s_i = s[:, i * SG_GROUP_CH:(i + 1) * SG_GROUP_CH] + bs_ref[g]
                t_scr[rows, g * SG_GROUP_CH:(g + 1) * SG_GROUP_CH] = u[rows, gl] * s_i.astype(BF16)
    o_ref[...] = x + g1_ref[...] * _dot(t_scr[...], wo_ref[...])


def _sg(x, mods, layer, norm_g, mixer, w_in, b_in, ln_g, ln_b, w_s, b_s, w_o, *, latent, tm):
    n = x.shape[0]
    tiles_per_batch = (SEQ // tm) if latent else None
    tok = pl.BlockSpec((tm, D_MODEL), lambda t: (t, 0))
    return pl.pallas_call(
        _sg_kernel,
        grid=(n // tm,),
        in_specs=[
            tok,
            _vec_spec(D_MODEL, layer),
            _mod_spec(layer, 0, tiles_per_batch),
            _mod_spec(layer, 1, tiles_per_batch),
            _mod_spec(layer, 2, tiles_per_batch),
            _resident((D_MODEL, 2 * SG_HALF), mixer),
            _vec_spec(2 * SG_HALF, mixer),
            _vec_spec(SG_HALF, mixer),
            _vec_spec(SG_HALF, mixer),
            _resident((SG_GROUPS, SG_CHUNK, SG_CHUNK), mixer),
            _resident((SG_GROUPS, SG_CHUNK, SG_GROUP_CH), mixer),
            _resident((SG_HALF, D_MODEL), mixer),
        ],
        out_specs=tok,
        out_shape=jax.ShapeDtypeStruct((n, D_MODEL), F32),
        scratch_shapes=[pltpu.VMEM((tm, SG_HALF), F32), pltpu.VMEM((tm, SG_HALF), BF16)],
        compiler_params=_params(1),
        name="sg_lat" if latent else "sg_ctx",
    )(x, norm_g, mods, mods, mods, w_in, b_in, ln_g, ln_b, w_s, b_s, w_o)


LAT_TM = 1024
SG_TM = 512
CTX_TM = 512


def kernel(x, c, ctx, c_ctx, ada_w, ada_b, norm1_g, norm2_g, mlp_w1, mlp_w2,
           na_w_qkv, na_q_norm, na_k_norm, na_rpb, na_w_o,
           sg_w_in, sg_b_in, sg_ln_g, sg_ln_b, sg_w_s, sg_b_s, sg_w_o):
    last_ctx_layer = ((DEPTH - 1) // N_MIXERS) * N_MIXERS
    xl = x.reshape(BATCH * SEQ, D_MODEL)
    xc = ctx.reshape(BATCH * CTX_LEN, D_MODEL)

    cond = jnp.concatenate(
        [c, c_ctx[None, :], jnp.zeros((MOD_ROWS - BATCH - 1, D_MODEL), F32)], axis=0)
    mods = _adaln(cond, ada_w, ada_b).reshape(DEPTH * MOD_ROWS * 6, 1, D_MODEL)

    q_gains = na_q_norm * (NA_HEAD_DIM ** -0.5 * LOG2_E)
    rpb2 = na_rpb * LOG2_E
    logit_bounds = (NA_HEAD_DIM * BF16_DOT_SLACK * jnp.max(jnp.abs(q_gains), axis=1)
                    * jnp.max(jnp.abs(na_k_norm), axis=1) + jnp.max(jnp.abs(rpb2), axis=(1, 2, 3)))
    bias = _attention_bias_tables(
        (rpb2 - logit_bounds[:, None, None, None]).reshape((-1,) + na_rpb.shape[2:]))

    w1, w2 = mlp_w1.astype(BF16), mlp_w2.astype(BF16)
    w_qkv, w_ao = na_w_qkv.astype(BF16), na_w_o.astype(BF16)
    w_in, w_s, w_so = sg_w_in.astype(BF16), sg_w_s.astype(BF16), sg_w_o.astype(BF16)

    n1, n2 = norm1_g[:, None, :], norm2_g[:, None, :]
    q_gain = jnp.tile(q_gains, (1, NA_HEADS))[:, None, :]
    k_gain = jnp.tile(na_k_norm, (1, NA_HEADS))[:, None, :]
    sg_vecs = (sg_b_in[:, None, :], sg_ln_g[:, None, :], sg_ln_b[:, None, :])
    sg_bias = jnp.broadcast_to(sg_b_s[:, :, :, None], sg_b_s.shape + (SG_GROUP_CH,))

    for i in range(DEPTH):
        ctx_full = i < last_ctx_layer
        if i % N_MIXERS == 0:
            a = i // N_MIXERS
            q, k, v = _qkv(xl, mods, i, n1, w_qkv, a, q_gain, k_gain,
                           latent=True, need_q=True, tm=LAT_TM)
            ctx_proj = _qkv(xc, mods, i, n1, w_qkv, a, q_gain, k_gain,
                            latent=False, need_q=ctx_full, tm=CTX_TM)
            kc, vc = ctx_proj[-2], ctx_proj[-1]
            att = _attn_lat(q, k, v, kc, vc, bias, a, logit_bounds[a])
            xl = _mlp(xl, mods, i, n2, w1, w2, latent=True, tm=LAT_TM, attn=att, w_o=w_ao, mixer=a)
            if ctx_full:
                att_c = _attn_ctx(ctx_proj[0], kc, vc)
                xc = _mlp(xc, mods, i, n2, w1, w2, latent=False, tm=CTX_TM, attn=att_c, w_o=w_ao,
                          mixer=a)
        else:
            s = i // N_MIXERS
            sg_args = (s, w_in, *sg_vecs, w_s, sg_bias, w_so)
            xl = _sg(xl, mods, i, n1, *sg_args, latent=True, tm=SG_TM)
            xl = _mlp(xl, mods, i, n2, w1, w2, latent=True, tm=LAT_TM)
            if ctx_full:
                xc = _sg(xc, mods, i, n1, *sg_args, latent=False, tm=CTX_TM)
                xc = _mlp(xc, mods, i, n2, w1, w2, latent=False, tm=CTX_TM)
    return xl.reshape(BATCH, SEQ, D_MODEL)
```
